```python
import math
import jax
import jax.numpy as jnp
from jax import lax
import numpy as np

D_MODEL = 1024
BATCH = 8
SEQ = 2048
DEPTH = 2
DEC_BATCH = 128
DEC_SEQ = 1
PAST_LEN = 2048
PAGE_SIZE = 128

EPS = 1e-6
DN_HEADS = 8
DN_DK = 128
DN_DV = 128
CONV_W = 4
CONV_DIM = 2 * DN_HEADS * DN_DK + DN_HEADS * DN_DV
DN_CHUNK = 64
NSA_HEADS = 8
NSA_KV = 2
NSA_G = NSA_HEADS // NSA_KV
NSA_DH = 128
NSA_SCALE = NSA_DH ** -0.5
NSA_BLK = 64
NSA_TOPN = 16
NSA_WINDOW = 512
NSA_CMP_HID = 128
SEL_QBLK = 32
WIN_QBLK = 128
FORCE_SCORE = 1e4
NEG_INF = -1e30
NUM_BUCKETS = 32
MAX_DISTANCE = 128
D_FF = 256 * (-(-8 * D_MODEL // (3 * 256)))
IN_SIZES = (CONV_DIM, DN_HEADS * DN_DV, DN_HEADS, DN_HEADS, NSA_HEADS * NSA_DH,
            6 * NSA_KV * NSA_DH, 3 * NSA_HEADS, 2 * D_MODEL)
IN_DIM = sum(IN_SIZES)
POOL_NUM = 5
POOL_DEN = 4

kernel_name = 'hybrid_gdn_nsa_decode_step'


def _rms(x, g):
    xf = x.astype(jnp.float32)
    y = xf * lax.rsqrt(jnp.mean(xf * xf, axis=-1, keepdims=True) + EPS)
    return (y * g.astype(jnp.float32)).astype(x.dtype)


def _l2n(x):
    xf = x.astype(jnp.float32)
    return xf * lax.rsqrt(jnp.sum(xf * xf, axis=-1, keepdims=True) + EPS)


def _split(t, sizes):
    return jnp.split(t, [int(i) for i in np.cumsum(sizes)[:-1]], axis=-1)


def _t5_bucket(dist):
    n = jnp.maximum(dist, 0)
    exact = NUM_BUCKETS // 2
    big = exact + (jnp.log(jnp.maximum(n, 1).astype(jnp.float32) / exact)
                   / math.log(MAX_DISTANCE / exact) * (NUM_BUCKETS - exact)).astype(jnp.int32)
    return jnp.where(n < exact, n, jnp.minimum(big, NUM_BUCKETS - 1))


def _gather_pages(pool, page_table):
    g = pool[page_table]
    return g.reshape((g.shape[0], g.shape[1] * g.shape[2]) + g.shape[3:])


def _gated_delta(q, k, v, g, beta, s0):
    B, T, H, _ = q.shape
    Dv = v.shape[-1]
    C = min(DN_CHUNK, T)
    n = -(-T // C)
    pad = n * C - T

    def chunks(t):
        t = jnp.pad(t, ((0, 0), (0, pad)) + ((0, 0),) * (t.ndim - 2))
        t = t.reshape((B, n, C) + t.shape[2:])
        return jnp.transpose(t, (1, 0, 3, 2) + tuple(range(4, t.ndim)))

    qc, kc, vc, bc = chunks(q), chunks(k), chunks(v), chunks(beta)
    gc = jnp.cumsum(chunks(g), axis=-1)
    idx = jnp.arange(C)
    incl = idx[:, None] >= idx[None, :]
    decay = jnp.exp(jnp.where(incl, gc[..., :, None] - gc[..., None, :], -jnp.inf))
    kb = kc * bc[..., None]
    lower = jnp.where(idx[:, None] > idx[None, :],
                      jnp.einsum('nbhcd,nbhsd->nbhcs', kb, kc) * decay, 0.0)
    rhs = jnp.concatenate([vc * bc[..., None], kb * jnp.exp(gc)[..., None]], axis=-1)
    sol = lax.linalg.triangular_solve(lower, rhs, left_side=True, lower=True, unit_diagonal=True)
    u, w = sol[..., :Dv], sol[..., Dv:]
    qk = jnp.einsum('nbhcd,nbhsd->nbhcs', qc, kc) * decay

    def step(S, xs):
        q_i, k_i, u_i, w_i, g_i, qk_i = xs
        v_new = u_i - jnp.einsum('bhcd,bhde->bhce', w_i, S)
        o_i = (jnp.einsum('bhcd,bhde->bhce', q_i * jnp.exp(g_i)[..., None], S)
               + jnp.einsum('bhcs,bhse->bhce', qk_i, v_new))
        g_last = g_i[..., -1:]
        S = S * jnp.exp(g_last)[..., None] + jnp.einsum(
            'bhcd,bhce->bhde', k_i * jnp.exp(g_last - g_i)[..., None], v_new)
        return S, o_i

    S, o = lax.scan(step, s0, (qc, kc, u, w, gc, qk))
    o = jnp.transpose(o, (1, 0, 3, 2, 4)).reshape(B, n * C, H, Dv)[:, :T]
    return o, S


def _deltanet(qkv_pre, z, b, a, conv_buf, s0, conv_w, a_log, dt_bias, onorm_g):
    B, T, _ = qkv_pre.shape
    xp = jnp.concatenate([conv_buf.astype(qkv_pre.dtype), qkv_pre], axis=1)
    conv = xp[:, 0:T] * conv_w[0]
    for i in range(1, CONV_W):
        conv = conv + xp[:, i:i + T] * conv_w[i]
    q, k, v = jnp.split(jax.nn.silu(conv), [DN_HEADS * DN_DK, 2 * DN_HEADS * DN_DK], axis=-1)
    q = _l2n(q.reshape(B, T, DN_HEADS, DN_DK)) * (DN_DK ** -0.5)
    k = _l2n(k.reshape(B, T, DN_HEADS, DN_DK))
    v = v.reshape(B, T, DN_HEADS, DN_DV).astype(jnp.float32)
    beta = jax.nn.sigmoid(b.astype(jnp.float32))
    g = -jnp.exp(a_log.astype(jnp.float32)) * jax.nn.softplus(a.astype(jnp.float32) + dt_bias.astype(jnp.float32))
    o, s = _gated_delta(q, k, v, g, beta, s0.astype(jnp.float32))
    zr = z.reshape(B, T, DN_HEADS, DN_DV).astype(jnp.float32)
    o = o * lax.rsqrt(jnp.mean(o * o, axis=-1, keepdims=True) + EPS) * onorm_g.astype(jnp.float32) * jax.nn.silu(zr)
    return o.reshape(B, T, DN_HEADS * DN_DV).astype(qkv_pre.dtype), xp[:, T:], s.astype(qkv_pre.dtype)


def _cmp_branch(q, q_pos, cmp_full, pe, w1, w2, rel):
    B, L = cmp_full.shape[:2]
    n = L // NSA_BLK
    blocks = cmp_full[:, :n * NSA_BLK].reshape(B, n, NSA_BLK, 2, NSA_KV, NSA_DH) + pe[:, :, None, :].astype(q.dtype)
    hid = jax.nn.silu(jnp.einsum('bnlckd,cldf->bnckf', blocks, w1))
    kvc = jnp.einsum('bnckf,cfd->bnckd', hid, w2)
    end = jnp.arange(n) * NSA_BLK + (NSA_BLK - 1)
    dist = q_pos[:, None] - end[None, :]
    valid = dist >= 0
    bias = jnp.transpose(rel[_t5_bucket(dist)].astype(jnp.float32).reshape(dist.shape + (NSA_KV, NSA_G)), (2, 3, 0, 1))
    logits = jnp.einsum('btkgd,bnkd->bkgtn', q, kvc[:, :, 0], preferred_element_type=jnp.float32) * NSA_SCALE + bias
    p = jax.nn.softmax(jnp.where(valid, logits, NEG_INF), axis=-1) * valid
    o = jnp.einsum('bkgtn,bnkd->btkgd', p.astype(q.dtype), kvc[:, :, 1])
    return o, p.sum(axis=2)


def _select(imp, q_pos, n_blocks):
    imp = jnp.pad(imp, ((0, 0), (0, 0), (0, 0), (0, n_blocks - imp.shape[-1])))
    j = jnp.arange(n_blocks)[None, :]
    cur = (q_pos // NSA_BLK)[:, None]
    forced = (j == 0) | (j == cur) | (j == cur - 1)
    score = jnp.where(j <= cur, jnp.where(forced, FORCE_SCORE, imp), -1.0)
    return lax.top_k(score, min(NSA_TOPN, n_blocks))[1]


def _sel_branch(q, q_pos, sel_idx, sel_full, rel):
    B, Tq = q.shape[:2]
    n_top = sel_idx.shape[-1]
    nblk = sel_full.shape[1] // NSA_BLK
    kvr = jnp.transpose(sel_full.reshape(B, nblk, NSA_BLK, 2, NSA_KV, NSA_DH), (0, 4, 1, 2, 3, 5))
    qb = min(SEL_QBLK, Tq)
    nq = -(-Tq // qb)
    pad = nq * qb - Tq
    qp = jnp.transpose(jnp.pad(q, ((0, 0), (0, pad), (0, 0), (0, 0), (0, 0))).reshape(
        B, nq, qb, NSA_KV, NSA_G, NSA_DH), (1, 0, 2, 3, 4, 5))
    ip = jnp.transpose(jnp.pad(sel_idx, ((0, 0), (0, 0), (0, pad), (0, 0)), mode='edge').reshape(
        B, NSA_KV, nq, qb, n_top), (2, 0, 1, 3, 4))
    pp = jnp.pad(q_pos, (0, pad), mode='edge').reshape(nq, qb)
    rel_kv = jnp.transpose(rel.reshape(NUM_BUCKETS, NSA_KV, NSA_G), (1, 0, 2)).astype(jnp.float32)
    kv_ix = jnp.arange(NSA_KV)[None, :, None, None]

    def one(args):
        qi, ii, pi = args
        gath = jnp.take_along_axis(kvr, ii.reshape(B, NSA_KV, qb * n_top)[..., None, None, None], axis=2)
        gath = gath.reshape(B, NSA_KV, qb, n_top * NSA_BLK, 2, NSA_DH)
        tok = (ii[..., None] * NSA_BLK + jnp.arange(NSA_BLK)).reshape(B, NSA_KV, qb, n_top * NSA_BLK)
        dist = pi[None, None, :, None] - tok
        bias = rel_kv[kv_ix, _t5_bucket(dist)]
        logits = jnp.einsum('bqkgd,bkqsd->bkgqs', qi, gath[..., 0, :],
                            preferred_element_type=jnp.float32) * NSA_SCALE + jnp.moveaxis(bias, -1, 2)
        p = jax.nn.softmax(jnp.where((dist >= 0)[:, :, None], logits, NEG_INF), axis=-1)
        return jnp.einsum('bkgqs,bkqsd->bqkgd', p.astype(qi.dtype), gath[..., 1, :])

    o = lax.map(one, (qp, ip, pp))
    return jnp.transpose(o, (1, 0, 2, 3, 4, 5)).reshape(B, nq * qb, NSA_KV, NSA_G, NSA_DH)[:, :Tq]


def _win_branch(qw, kvw, qpos, kpos, rel):
    dist = qpos[:, :, None] - kpos[:, None, :]
    valid = (dist >= 0) & (dist < NSA_WINDOW) & (kpos[:, None, :] >= 0)
    bias = rel[_t5_bucket(dist)].astype(jnp.float32).reshape(dist.shape + (NSA_KV, NSA_G))
    bias = jnp.transpose(bias, (0, 3, 4, 1, 2))
    logits = jnp.einsum('bnqkgd,bnskd->bnkgqs', qw, kvw[:, :, :, 0],
                        preferred_element_type=jnp.float32) * NSA_SCALE + bias
    p = jax.nn.softmax(jnp.where(valid[:, None, None], logits, NEG_INF), axis=-1)
    return jnp.einsum('bnkgqs,bnskd->bnqkgd', p.astype(qw.dtype), kvw[:, :, :, 1])


def _nsa(q, gates, kv_new, q_pos, past, pe, w1, w2, rel):
    B, T = q.shape[:2]
    cmp_new, sel_new, win_new = kv_new[:, :, 0:2], kv_new[:, :, 2:4], kv_new[:, :, 4:6]
    if past is None:
        cmp_full, sel_full = cmp_new, sel_new
        n = T // WIN_QBLK
        kvp = jnp.pad(win_new, ((0, 0), (NSA_WINDOW, 0), (0, 0), (0, 0), (0, 0)))
        kidx = (jnp.arange(n) * WIN_QBLK)[:, None] + jnp.arange(NSA_WINDOW + WIN_QBLK)[None, :]
        kvw = kvp[:, kidx]
        kpos = kidx - NSA_WINDOW
        qw = q.reshape(B, n, WIN_QBLK, NSA_KV, NSA_G, NSA_DH)
        qpos_w = q_pos.reshape(n, WIN_QBLK)
        win_state = win_new[:, T - min(NSA_WINDOW, T):]
    else:
        cmp_past, sel_past, win_buf = past
        cmp_full = jnp.concatenate([cmp_past.astype(q.dtype), cmp_new], axis=1)
        sel_full = jnp.concatenate([sel_past.astype(q.dtype), sel_new], axis=1)
        wb = win_buf.shape[1]
        win_cat = jnp.concatenate([win_buf.astype(q.dtype), win_new], axis=1)
        kvw = win_cat[:, None]
        kpos = (q_pos[0] - wb + jnp.arange(wb + T))[None]
        qw = q[:, None]
        qpos_w = q_pos[None]
        win_state = win_cat[:, T:]
    L = sel_full.shape[1]
    n_blocks = -(-L // NSA_BLK)
    sel_full = jnp.pad(sel_full, ((0, 0), (0, n_blocks * NSA_BLK - L), (0, 0), (0, 0), (0, 0)))
    o_cmp, imp = _cmp_branch(q, q_pos, cmp_full, pe, w1, w2, rel)
    o_sel = _sel_branch(q, q_pos, _select(imp, q_pos, n_blocks), sel_full, rel)
    o_win = _win_branch(qw, kvw, qpos_w, kpos, rel).reshape(B, T, NSA_KV, NSA_G, NSA_DH)
    gt = jax.nn.sigmoid(gates).reshape(B, T, 3, NSA_KV, NSA_G, 1)
    o = gt[:, :, 0] * o_cmp + gt[:, :, 1] * o_sel + gt[:, :, 2] * o_win
    return o.reshape(B, T, NSA_HEADS * NSA_DH), win_state


def _layer(x, c, q_pos, past, P, rel):
    B, T, _ = x.shape
    mod = jnp.einsum('bd,de->be', jax.nn.silu(c), P['w_ada']) + P['b_ada']
    sh1, sc1, gt1, sh2, sc2, gt2 = [m[:, None, :] for m in jnp.split(mod, 6, axis=-1)]
    h = _rms(x, P['norm_g'][0]) * (1.0 + sc1) + sh1
    qkv_pre, z, b, a, q_nsa, kv_nsa, g_nsa, g_merge = _split(h @ P['w_in'], IN_SIZES)
    if past is None:
        conv_buf = jnp.zeros((B, CONV_W - 1, CONV_DIM), x.dtype)
        s0 = jnp.zeros((B, DN_HEADS, DN_DK, DN_DV), x.dtype)
        nsa_past = None
    else:
        conv_buf, s0, nsa_past = past[0], past[1], past[2:]
    o_a, conv_new, s_new = _deltanet(qkv_pre, z, b, a, conv_buf, s0, P['conv_w'], P['a_log'],
                                     P['dt_bias'], P['onorm_g'])
    kv_nsa = kv_nsa.reshape(B, T, 6, NSA_KV, NSA_DH)
    o_b, win_new = _nsa(q_nsa.reshape(B, T, NSA_KV, NSA_G, NSA_DH), g_nsa, kv_nsa, q_pos, nsa_past,
                        P['cmp_pe'], P['cmp_w1'], P['cmp_w2'], rel)
    gate_a, gate_b = jnp.split(jax.nn.sigmoid(g_merge), 2, axis=-1)
    mixed = (gate_a * (o_a @ P['w_oa']) + gate_b * (o_b @ P['w_ob'])) @ P['w_out']
    x = x + gt1 * _rms(mixed, P['norm_g'][1])
    h = _rms(x, P['norm_g'][2]) * (1.0 + sc2) + sh2
    gate, up = jnp.split(h @ P['w_gu'], 2, axis=-1)
    x = x + gt2 * _rms((jax.nn.silu(gate) * up) @ P['w_down'], P['norm_g'][3])
    return x, (kv_nsa[:, :, 0:2], kv_nsa[:, :, 2:4], win_new, s_new, conv_new)


def setup_inputs(seed: int = 0) -> dict:
    key = jax.random.key(seed)
    ks = jax.random.split(key, 28)
    f32 = jnp.float32

    def nrm(k, shape, scale):
        return jax.random.normal(k, shape, f32) * scale

    n_pages = PAST_LEN // PAGE_SIZE
    n_pool = (DEC_BATCH * n_pages * POOL_NUM) // POOL_DEN
    w_buf = min(NSA_WINDOW, PAST_LEN)
    page_table = jax.random.permutation(ks[0], n_pool)[:DEC_BATCH * n_pages].reshape(
        DEC_BATCH, n_pages).astype(jnp.int32)
    return {
        'x_prompt': nrm(ks[1], (BATCH, SEQ, D_MODEL), 1.0),
        'x_sample': nrm(ks[2], (DEC_BATCH, DEC_SEQ, D_MODEL), 1.0),
        'cache_cmp_kv': nrm(ks[3], (DEPTH, n_pool, PAGE_SIZE, 2, NSA_KV, NSA_DH), 1.0),
        'cache_sel_kv': nrm(ks[4], (DEPTH, n_pool, PAGE_SIZE, 2, NSA_KV, NSA_DH), 1.0),
        'cache_win_kv': nrm(ks[5], (DEPTH, DEC_BATCH, w_buf, 2, NSA_KV, NSA_DH), 1.0),
        'state_delta': nrm(ks[6], (DEPTH, DEC_BATCH, DN_HEADS, DN_DK, DN_DV), 0.5),
        'state_conv': nrm(ks[7], (DEPTH, DEC_BATCH, CONV_W - 1, CONV_DIM), 1.0),
        'page_table': page_table,
        'c_prompt': nrm(ks[8], (BATCH, D_MODEL), 1.0),
        'c_sample': nrm(ks[9], (DEC_BATCH, D_MODEL), 1.0),
        'w_ada': nrm(ks[10], (DEPTH, D_MODEL, 6 * D_MODEL), 0.5 * D_MODEL ** -0.5),
        'b_ada': nrm(ks[11], (DEPTH, 6 * D_MODEL), 0.02),
        'norm_g': 1.0 + nrm(ks[12], (DEPTH, 4, D_MODEL), 0.05),
        'w_in': nrm(ks[13], (DEPTH, D_MODEL, IN_DIM), D_MODEL ** -0.5),
        'conv_w': nrm(ks[14], (DEPTH, CONV_W, CONV_DIM), CONV_W ** -0.5),
        'a_log': jnp.log(jax.random.uniform(ks[15], (DEPTH, DN_HEADS), f32, 1.0, 16.0)),
        'dt_bias': nrm(ks[16], (DEPTH, DN_HEADS), 0.1),
        'onorm_g': 1.0 + nrm(ks[17], (DEPTH, DN_DV), 0.05),
        'w_oa': nrm(ks[18], (DEPTH, DN_HEADS * DN_DV, D_MODEL), (DN_HEADS * DN_DV) ** -0.5),
        'w_ob': nrm(ks[19], (DEPTH, NSA_HEADS * NSA_DH, D_MODEL), (NSA_HEADS * NSA_DH) ** -0.5),
        'w_out': nrm(ks[20], (DEPTH, D_MODEL, D_MODEL), D_MODEL ** -0.5),
        'cmp_pe': nrm(ks[21], (DEPTH, NSA_BLK, 2, NSA_DH), 0.1),
        'cmp_w1': nrm(ks[22], (DEPTH, 2, NSA_BLK, NSA_DH, NSA_CMP_HID), (NSA_BLK * NSA_DH) ** -0.5),
        'cmp_w2': nrm(ks[23], (DEPTH, 2, NSA_CMP_HID, NSA_DH), NSA_CMP_HID ** -0.5),
        'rel_bias': nrm(ks[24], (NUM_BUCKETS, NSA_HEADS), 0.3),
        'w_gu': nrm(ks[25], (DEPTH, D_MODEL, 2 * D_FF), D_MODEL ** -0.5),
        'w_down': nrm(ks[26], (DEPTH, D_FF, D_MODEL), D_FF ** -0.5),
    }


def reference(x_prompt, x_sample, cache_cmp_kv, cache_sel_kv, cache_win_kv, state_delta, state_conv,
              page_table, c_prompt, c_sample, w_ada, b_ada, norm_g, w_in, conv_w, a_log, dt_bias,
              onorm_g, w_oa, w_ob, w_out, cmp_pe, cmp_w1, cmp_w2, rel_bias, w_gu, w_down):
    past_len = page_table.shape[1] * cache_cmp_kv.shape[2]
    pos_p = jnp.arange(x_prompt.shape[1])
    pos_s = past_len + jnp.arange(x_sample.shape[1])
    yp, ys = x_prompt, x_sample
    p_st, s_st = [], []
    for l in range(DEPTH):
        P = {'w_ada': w_ada[l], 'b_ada': b_ada[l], 'norm_g': norm_g[l], 'w_in': w_in[l],
             'conv_w': conv_w[l], 'a_log': a_log[l], 'dt_bias': dt_bias[l], 'onorm_g': onorm_g[l],
             'w_oa': w_oa[l], 'w_ob': w_ob[l], 'w_out': w_out[l], 'cmp_pe': cmp_pe[l],
             'cmp_w1': cmp_w1[l], 'cmp_w2': cmp_w2[l], 'w_gu': w_gu[l], 'w_down': w_down[l]}
        yp, st = _layer(yp, c_prompt, pos_p, None, P, rel_bias)
        p_st.append(st)
        past = (state_conv[l], state_delta[l], _gather_pages(cache_cmp_kv[l], page_table),
                _gather_pages(cache_sel_kv[l], page_table), cache_win_kv[l])
        ys, st = _layer(ys, c_sample, pos_s, past, P, rel_bias)
        s_st.append(st)

    def stk(states, i):
        return jnp.stack([s[i] for s in states])

    return (yp, ys, stk(p_st, 0), stk(p_st, 1), stk(p_st, 2), stk(p_st, 3), stk(p_st, 4),
            stk(s_st, 0), stk(s_st, 1), stk(s_st, 2), stk(s_st, 3), stk(s_st, 4))
```

```python
import functools
import math

import numpy as np
import jax
import jax.numpy as jnp
from jax import lax
from jax.experimental import pallas as pl
from jax.experimental.pallas import tpu as pltpu

F32 = jnp.float32
BF16 = jnp.bfloat16

D_MODEL = 1024
EPS = 1e-6
DN_HEADS = 8
DN_DK = 128
DN_DV = 128
CONV_W = 4
CONV_DIM = 2 * DN_HEADS * DN_DK + DN_HEADS * DN_DV
DN_CHUNK = 64
NSA_HEADS = 8
NSA_KV = 2
NSA_G = NSA_HEADS // NSA_KV
NSA_DH = 128
NSA_SCALE = NSA_DH ** -0.5
NSA_BLK = 64
NSA_TOPN = 16
NSA_WINDOW = 512
NSA_CMP_HID = 128
FORCE_SCORE = 1e4
NEG_INF = -1e30
NUM_BUCKETS = 32
MAX_DISTANCE = 128
D_FF = 256 * (-(-8 * D_MODEL // (3 * 256)))
PAGE_SIZE = 128

C_QKV = 0
C_Z = 3072
C_QN = 4096
C_GM = 5120
C_KVN = 7168
C_SM = 8704
N_PROJ = 8832
SM_BETA = 0
SM_A = 8
SM_GATE = 16
KV_ROW = 2 * NSA_KV * NSA_DH

LANE = 128
SUBLANE = 8
VMEM_LIMIT = 48 * 1024 * 1024
TQ = 128


def _cparams(sem):
    return pltpu.CompilerParams(dimension_semantics=sem, vmem_limit_bytes=VMEM_LIMIT)


def _pick(n, pref, mult=SUBLANE):
    for t in range(min(pref, n), 0, -1):
        if n % t == 0 and t % mult == 0:
            return t
    return n


def _mm(a, b, dims=((1,), (0,))):
    return lax.dot_general(a.astype(BF16), b.astype(BF16), (dims, ((), ())),
                           preferred_element_type=F32)


NN = ((1,), (0,))
NT = ((1,), (1,))
TN = ((0,), (0,))


def _hilo(a):
    hi = a.astype(BF16)
    lo = (a - hi.astype(F32)).astype(BF16)
    return hi, lo


def _mm3(a, b, dims=NN):
    ah, al = _hilo(a)
    bh, bl = _hilo(b)
    return _mm(ah, bh, dims) + (_mm(ah, bl, dims) + _mm(al, bh, dims))


def _mm_exact_lhs(a01, b):
    b0 = b.astype(BF16)
    r1 = b - b0.astype(F32)
    b1 = r1.astype(BF16)
    b2 = (r1 - b1.astype(F32)).astype(BF16)
    return _mm(a01, b0) + (_mm(a01, b1) + _mm(a01, b2))


def _silu(x):
    return x * jax.nn.sigmoid(x)


def _softplus(x):
    return jnp.maximum(x, 0.0) + jnp.log1p(jnp.exp(-jnp.abs(x)))


def _rms_rows(x):
    return x * lax.rsqrt(jnp.mean(x * x, axis=-1, keepdims=True) + EPS)


def _ada_kernel(c_ref, w_ref, b_ref, o_ref):
    o_ref[...] = _mm(_silu(c_ref[...]), w_ref[...]) + b_ref[...]


def _ada(c_all, w, b):
    m, d = c_all.shape
    n = w.shape[1]
    tn = _pick(n, 1536, LANE)
    return pl.pallas_call(
        _ada_kernel,
        grid=(n // tn,),
        in_specs=[pl.BlockSpec((m, d), lambda j: (0, 0)),
                  pl.BlockSpec((d, tn), lambda j: (0, j)),
                  pl.BlockSpec((1, tn), lambda j: (0, j))],
        out_specs=pl.BlockSpec((m, tn), lambda j: (0, j)),
        out_shape=jax.ShapeDtypeStruct((m, n), F32),
        compiler_params=_cparams(("parallel",)),
        name="ada",
    )(c_all, w, b.reshape(1, n))


def _proj_kernel(x_ref, g_ref, sc_ref, sh_ref, w_ref, o_ref, h_ref):
    @pl.when(pl.program_id(2) == 0)
    def _():
        y = _rms_rows(x_ref[0]) * g_ref[...]
        h_ref[...] = (y * (1.0 + sc_ref[0]) + sh_ref[0]).astype(BF16)

    o_ref[0] = jnp.dot(h_ref[...], w_ref[...], preferred_element_type=F32)


def _mod_spec(mod, tm):
    if mod.shape[1] == 1:
        return pl.BlockSpec((1, 1, D_MODEL), lambda b, i, j: (b, 0, 0))
    return pl.BlockSpec((1, tm, D_MODEL), lambda b, i, j: (b, i, 0))


def _proj(x3, g, sc, sh, w):
    bb, t, d = x3.shape
    n = w.shape[1]
    tm = _pick(t, 512)
    tn = _pick(n, 2944, LANE)
    return pl.pallas_call(
        _proj_kernel,
        grid=(bb, t // tm, n // tn),
        in_specs=[pl.BlockSpec((1, tm, d), lambda b, i, j: (b, i, 0)),
                  pl.BlockSpec((1, d), lambda b, i, j: (0, 0)),
                  _mod_spec(sc, tm), _mod_spec(sh, tm),
                  pl.BlockSpec((d, tn), lambda b, i, j: (0, j))],
        out_specs=pl.BlockSpec((1, tm, tn), lambda b, i, j: (b, i, j)),
        out_shape=jax.ShapeDtypeStruct((bb, t, n), F32),
        scratch_shapes=[pltpu.VMEM((tm, d), BF16)],
        compiler_params=_cparams(("parallel", "parallel", "arbitrary")),
        name="proj",
    )(x3, g.reshape(1, d), sc, sh, w)


def _neumann_inverse(a, eye):
    t = eye - a
    p = _mm3(a, a)
    steps = int(math.log2(DN_CHUNK)) - 1
    for s in range(steps):
        t = t + _mm3(t, p)
        if s + 1 < steps:
            p = _mm3(p, p)
    return t


def _dn_prompt_kernel(qkv_ref, z_ref, sm_ref, cw_ref, hp_ref, on_ref, o_ref, s_ref, buf_ref, st_ref):
    c = pl.program_id(1)
    C = DN_CHUNK
    hist = SUBLANE

    @pl.when(c == 0)
    def _():
        buf_ref[0:hist, :] = jnp.zeros((hist, CONV_DIM), F32)
        st_ref[...] = jnp.zeros(st_ref.shape, F32)

    buf_ref[hist:hist + C, :] = qkv_ref[0]

    def conv_act(c0):
        acc = None
        for i in range(CONV_W):
            r0 = hist - (CONV_W - 1) + i
            term = buf_ref[r0:r0 + C, c0:c0 + LANE] * cw_ref[i:i + 1, c0:c0 + LANE]
            acc = term if acc is None else acc + term
        return _silu(acc)

    sm = sm_ref[0]
    beta_all = jax.nn.sigmoid(sm)
    g_all = -jnp.exp(hp_ref[0:1, :]) * _softplus(sm + hp_ref[1:2, :])
    ri = lax.broadcasted_iota(jnp.int32, (C, C), 0)
    ci = lax.broadcasted_iota(jnp.int32, (C, C), 1)
    incl = ri >= ci
    strict = ri > ci
    tri = jnp.where(incl, 1.0, 0.0).astype(BF16)
    eye = jnp.where(ri == ci, 1.0, 0.0)
    gc_all = _mm_exact_lhs(tri, g_all)
    gc_t = jnp.transpose(jnp.concatenate([gc_all, jnp.zeros((LANE - C, LANE), F32)], axis=0))
    eg_all = jnp.exp(gc_all)

    for h in range(DN_HEADS):
        q = conv_act(h * DN_DK)
        k = conv_act(DN_HEADS * DN_DK + h * DN_DK)
        v = conv_act(2 * DN_HEADS * DN_DK + h * DN_DV)
        q = q * lax.rsqrt(jnp.sum(q * q, axis=-1, keepdims=True) + EPS) * (DN_DK ** -0.5)
        k = k * lax.rsqrt(jnp.sum(k * k, axis=-1, keepdims=True) + EPS)
        la = SM_A + h
        beta = beta_all[:, SM_BETA + h:SM_BETA + h + 1]
        gc = gc_all[:, la:la + 1]
        eg = eg_all[:, la:la + 1]
        gcr = gc_t[la:la + 1, 0:C]
        decay = jnp.exp(jnp.where(incl, gc - gcr, NEG_INF))
        kb = k * beta
        a = jnp.where(strict, _mm(kb, k, NT) * decay, 0.0)
        tinv = _neumann_inverse(a, eye)
        u = _mm3(tinv, v * beta)
        w = _mm3(tinv, kb * eg)
        qk = _mm(q, k, NT) * decay
        s_old = st_ref[h]
        v_new = u - _mm(w, s_old)
        o = _mm(q * eg, s_old) + _mm(qk, v_new)
        g_last = gc_all[C - 1:C, la:la + 1]
        st_ref[h] = s_old * jnp.exp(g_last) + _mm(k * jnp.exp(g_last - gc), v_new, TN)
        z = z_ref[0, :, h * DN_DV:(h + 1) * DN_DV]
        o = _rms_rows(o) * on_ref[...] * _silu(z)
        o_ref[0, :, h * DN_DV:(h + 1) * DN_DV] = o

    buf_ref[0:hist, :] = buf_ref[C:C + hist, :]

    @pl.when(c == pl.num_programs(1) - 1)
    def _():
        s_ref[0] = st_ref[...]


def _dn_prompt(proj, cw, hp, on):
    b, t, _ = proj.shape
    C = DN_CHUNK
    assert t % C == 0
    return pl.pallas_call(
        _dn_prompt_kernel,
        grid=(b, t // C),
        in_specs=[pl.BlockSpec((1, C, CONV_DIM), lambda i, c: (i, c, C_QKV // CONV_DIM)),
                  pl.BlockSpec((1, C, 1024), lambda i, c: (i, c, C_Z // 1024)),
                  pl.BlockSpec((1, C, LANE), lambda i, c: (i, c, C_SM // LANE)),
                  pl.BlockSpec((CONV_W, CONV_DIM), lambda i, c: (0, 0)),
                  pl.BlockSpec((SUBLANE, LANE), lambda i, c: (0, 0)),
                  pl.BlockSpec((1, DN_DV), lambda i, c: (0, 0))],
        out_specs=[pl.BlockSpec((1, C, DN_HEADS * DN_DV), lambda i, c: (i, c, 0)),
                   pl.BlockSpec((1, DN_HEADS, DN_DK, DN_DV), lambda i, c: (i, 0, 0, 0))],
        out_shape=[jax.ShapeDtypeStruct((b, t, DN_HEADS * DN_DV), F32),
                   jax.ShapeDtypeStruct((b, DN_HEADS, DN_DK, DN_DV), F32)],
        scratch_shapes=[pltpu.VMEM((C + SUBLANE, CONV_DIM), F32),
                        pltpu.VMEM((DN_HEADS, DN_DK, DN_DV), F32)],
        compiler_params=_cparams(("parallel", "arbitrary")),
        name="dn_prompt",
    )(proj, proj, proj, cw, hp, on)


def _dn_sample_kernel(qkv_ref, z_ref, sm_ref, cb_ref, cw_ref, hp_ref, on_ref, sin_ref, o_ref, sout_ref):
    xb = cb_ref[0]
    conv = qkv_ref[0] * cw_ref[CONV_W - 1:CONV_W, :]
    for i in range(CONV_W - 1):
        conv = conv + xb[i:i + 1, :] * cw_ref[i:i + 1, :]
    act = _silu(conv)
    sm = sm_ref[0]
    beta_all = jax.nn.sigmoid(sm)
    eg_all = jnp.exp(-jnp.exp(hp_ref[0:1, :]) * _softplus(sm + hp_ref[1:2, :]))
    row = lax.broadcasted_iota(jnp.int32, (SUBLANE, DN_DK), 0)
    for h in range(DN_HEADS):
        q = act[:, h * DN_DK:(h + 1) * DN_DK]
        k = act[:, DN_HEADS * DN_DK + h * DN_DK:DN_HEADS * DN_DK + (h + 1) * DN_DK]
        v = act[:, 2 * DN_HEADS * DN_DK + h * DN_DV:2 * DN_HEADS * DN_DK + (h + 1) * DN_DV]
        q = q * lax.rsqrt(jnp.sum(q * q, axis=-1, keepdims=True) + EPS) * (DN_DK ** -0.5)
        k = k * lax.rsqrt(jnp.sum(k * k, axis=-1, keepdims=True) + EPS)
        beta = beta_all[:, SM_BETA + h:SM_BETA + h + 1]
        eg = eg_all[:, SM_A + h:SM_A + h + 1]
        s_old = sin_ref[0, h]
        kq = jnp.where(row == 0, jnp.broadcast_to(k, row.shape),
                       jnp.where(row == 1, jnp.broadcast_to(q, row.shape), 0.0))
        r = _mm3(kq, s_old)
        v_new = beta * (v - eg * r[0:1, :])
        qk = jnp.sum(q * k, axis=-1, keepdims=True)
        o = eg * r[1:2, :] + qk * v_new
        k_only = jnp.where(row == 0, jnp.broadcast_to(k, row.shape), 0.0)
        sout_ref[0, h] = s_old * eg + _mm3(k_only, jnp.broadcast_to(v_new, row.shape), TN)
        z = z_ref[0, :, h * DN_DV:(h + 1) * DN_DV]
        o_ref[0, :, h * DN_DV:(h + 1) * DN_DV] = _rms_rows(o) * on_ref[...] * _silu(z)


def _dn_sample(proj_rows, conv_buf, state, cw, hp, on):
    db = proj_rows.shape[0]
    return pl.pallas_call(
        _dn_sample_kernel,
        grid=(db,),
        in_specs=[pl.BlockSpec((1, 1, CONV_DIM), lambda i: (i, 0, C_QKV // CONV_DIM)),
                  pl.BlockSpec((1, 1, 1024), lambda i: (i, 0, C_Z // 1024)),
                  pl.BlockSpec((1, 1, LANE), lambda i: (i, 0, C_SM // LANE)),
                  pl.BlockSpec((1, CONV_W - 1, CONV_DIM), lambda i: (i, 0, 0)),
                  pl.BlockSpec((CONV_W, CONV_DIM), lambda i: (0, 0)),
                  pl.BlockSpec((SUBLANE, LANE), lambda i: (0, 0)),
                  pl.BlockSpec((1, DN_DV), lambda i: (0, 0)),
                  pl.BlockSpec((1, DN_HEADS, DN_DK, DN_DV), lambda i: (i, 0, 0, 0))],
        out_specs=[pl.BlockSpec((1, 1, DN_HEADS * DN_DV), lambda i: (i, 0, 0)),
                   pl.BlockSpec((1, DN_HEADS, DN_DK, DN_DV), lambda i: (i, 0, 0, 0))],
        out_shape=[jax.ShapeDtypeStruct((db, 1, DN_HEADS * DN_DV), F32),
                   jax.ShapeDtypeStruct((db, DN_HEADS, DN_DK, DN_DV), F32)],
        compiler_params=_cparams(("parallel",)),
        name="dn_sample",
    )(proj_rows, proj_rows, proj_rows, conv_buf, cw, hp, on, state)


CMP_LC = 8


def _compress_kernel(x_ref, pe_ref, w1_ref, w2_ref, o_ref, hid_ref):
    j = pl.program_id(1)

    @pl.when(j == 0)
    def _():
        hid_ref[...] = jnp.zeros(hid_ref.shape, F32)

    for ll in range(CMP_LC):
        for ck in range(2 * NSA_KV):
            c = ck // NSA_KV
            x = (x_ref[:, ll * KV_ROW + ck * NSA_DH:ll * KV_ROW + (ck + 1) * NSA_DH]
                 + pe_ref[:, (ll * 2 + c) * NSA_DH:(ll * 2 + c + 1) * NSA_DH])
            hid_ref[ck] += _mm(x, w1_ref[(ll * 2 + c) * NSA_DH:(ll * 2 + c + 1) * NSA_DH, :])

    @pl.when(j == pl.num_programs(1) - 1)
    def _():
        for ck in range(2 * NSA_KV):
            c = ck // NSA_KV
            o_ref[:, ck * NSA_DH:(ck + 1) * NSA_DH] = _mm(
                _silu(hid_ref[ck]), w2_ref[c * NSA_CMP_HID:(c + 1) * NSA_CMP_HID, :])


def _compress(x2, pe_flat, w1_r, w2_r):
    r = x2.shape[0]
    tb = _pick(r, 256)
    nj = NSA_BLK // CMP_LC
    return pl.pallas_call(
        _compress_kernel,
        grid=(r // tb, nj),
        in_specs=[pl.BlockSpec((tb, CMP_LC * KV_ROW), lambda i, j: (i, j)),
                  pl.BlockSpec((1, CMP_LC * 2 * NSA_DH), lambda i, j: (0, j)),
                  pl.BlockSpec((CMP_LC * 2 * NSA_DH, NSA_CMP_HID), lambda i, j: (j, 0)),
                  pl.BlockSpec((2 * NSA_CMP_HID, NSA_DH), lambda i, j: (0, 0))],
        out_specs=pl.BlockSpec((tb, KV_ROW), lambda i, j: (i, 0)),
        out_shape=jax.ShapeDtypeStruct((r, KV_ROW), F32),
        scratch_shapes=[pltpu.VMEM((2 * NSA_KV, tb, NSA_CMP_HID), F32)],
        compiler_params=_cparams(("parallel", "arbitrary")),
        name="compress",
    )(x2, pe_flat, w1_r, w2_r)


def _cmp_select_kernel(q_ref, kvc_ref, bias_ref, o_ref, selm_ref, *, n_top):
    qi = pl.program_id(1)
    tq = q_ref.shape[1]
    n = kvc_ref.shape[1]
    qpos = qi * tq + lax.broadcasted_iota(jnp.int32, (tq, n), 0)
    jn = lax.broadcasted_iota(jnp.int32, (tq, n), 1)
    valid = qpos >= jn * NSA_BLK + (NSA_BLK - 1)
    validf = jnp.where(valid, 1.0, 0.0)
    cur = qpos // NSA_BLK
    forced = jnp.where(jn == 0, 1.0, 0.0) + jnp.where(jn == cur, 1.0, 0.0) + jnp.where(jn == cur - 1, 1.0, 0.0)
    kvc = kvc_ref[0]
    for kv in range(NSA_KV):
        kc = kvc[:, kv * NSA_DH:(kv + 1) * NSA_DH]
        vc = kvc[:, (NSA_KV + kv) * NSA_DH:(NSA_KV + kv + 1) * NSA_DH]
        imp = jnp.zeros((tq, n), F32)
        for g in range(NSA_G):
            h = kv * NSA_G + g
            qh = q_ref[0, :, h * NSA_DH:(h + 1) * NSA_DH]
            lg = _mm3(qh, kc, NT) * NSA_SCALE + bias_ref[h]
            lg = jnp.where(valid, lg, NEG_INF)
            e = jnp.exp(lg - jnp.max(lg, axis=-1, keepdims=True))
            p = e / jnp.sum(e, axis=-1, keepdims=True) * validf
            o_ref[0, :, h * NSA_DH:(h + 1) * NSA_DH] = _mm(p, vc)
            imp = imp + p
        score = jnp.where(jn <= cur, jnp.where(forced > 0.5, FORCE_SCORE, imp), -1.0)
        rank = jnp.zeros((tq, n), F32)
        for i in range(n):
            si = score[:, i:i + 1]
            ahead = jnp.where(si > score, 1.0, jnp.where(si == score, jnp.where(jn > i, 1.0, 0.0), 0.0))
            rank = rank + ahead
        selm_ref[0, kv] = jnp.where(rank < n_top, 1.0, 0.0)


def _cmp_select(proj, kvc, bias_cmp):
    b, t, _ = proj.shape
    n = kvc.shape[1]
    tq = TQ
    kern = functools.partial(_cmp_select_kernel, n_top=min(NSA_TOPN, n))
    return pl.pallas_call(
        kern,
        grid=(b, t // tq),
        in_specs=[pl.BlockSpec((1, tq, 1024), lambda i, j: (i, j, C_QN // 1024)),
                  pl.BlockSpec((1, n, KV_ROW), lambda i, j: (i, 0, 0)),
                  pl.BlockSpec((NSA_HEADS, tq, n), lambda i, j: (0, j, 0))],
        out_specs=[pl.BlockSpec((1, tq, NSA_HEADS * NSA_DH), lambda i, j: (i, j, 0)),
                   pl.BlockSpec((1, NSA_KV, tq, n), lambda i, j: (i, 0, j, 0))],
        out_shape=[jax.ShapeDtypeStruct((b, t, NSA_HEADS * NSA_DH), F32),
                   jax.ShapeDtypeStruct((b, NSA_KV, t, n), F32)],
        compiler_params=_cparams(("parallel", "parallel")),
        name="cmp_select",
    )(proj, kvc, bias_cmp)


def _selwin_kernel(q_ref, ks_ref, vs_ref, kw_ref, vw_ref, selm_ref, bt_ref, oc_ref, sm_ref, o_ref,
                   m_ref, l_ref, acc_ref):
    qi = pl.program_id(1)
    tq = TQ
    nb = selm_ref.shape[3]
    ri = lax.broadcasted_iota(jnp.int32, (tq, tq), 0)
    ci = lax.broadcasted_iota(jnp.int32, (tq, tq), 1)
    rel = ri - ci
    eb = lax.broadcasted_iota(jnp.int32, (nb, tq), 0)
    ec = lax.broadcasted_iota(jnp.int32, (nb, tq), 1) // NSA_BLK
    blocks_per_tile = tq // NSA_BLK
    win_tiles = NSA_WINDOW // tq
    sm = sm_ref[0]

    def run(kv, k_ref, v_ref, lo, selected):
        q4 = jnp.concatenate([q_ref[0, :, (kv * NSA_G + g) * NSA_DH:(kv * NSA_G + g + 1) * NSA_DH]
                              for g in range(NSA_G)], axis=0).astype(BF16)
        m_ref[...] = jnp.full(m_ref.shape, NEG_INF, F32)
        l_ref[...] = jnp.zeros(l_ref.shape, F32)
        acc_ref[...] = jnp.zeros(acc_ref.shape, F32)
        selm = selm_ref[0, kv].astype(BF16) if selected else None

        def body(kt, carry):
            off = pl.multiple_of(kt * tq, tq)
            k = k_ref[0, pl.ds(off, tq), kv * NSA_DH:(kv + 1) * NSA_DH].astype(BF16)
            v = v_ref[0, pl.ds(off, tq), kv * NSA_DH:(kv + 1) * NSA_DH].astype(BF16)
            s_all = lax.dot_general(q4, k, ((NT), ((), ())), preferred_element_type=F32) * NSA_SCALE
            dist = (qi - kt) * tq + rel
            if selected:
                onehot = jnp.where(eb == kt * blocks_per_tile + ec, 1.0, 0.0).astype(BF16)
                picked = jnp.dot(selm, onehot, preferred_element_type=F32)
                allowed = jnp.where(dist >= 0, picked, 0.0) > 0.5
            else:
                allowed = jnp.where(dist >= 0, jnp.where(dist < NSA_WINDOW, 1.0, 0.0), 0.0) > 0.5
            didx = jnp.minimum(qi - kt, 2)
            for g in range(NSA_G):
                rows = slice(g * tq, (g + 1) * tq)
                s = jnp.where(allowed, s_all[rows] + bt_ref[kv * NSA_G + g, didx], NEG_INF)
                m_old = m_ref[rows]
                m_new = jnp.maximum(m_old, jnp.max(s, axis=-1, keepdims=True))
                alpha = jnp.exp(m_old - m_new)
                p = jnp.exp(s - m_new)
                l_ref[rows] = alpha * l_ref[rows] + jnp.sum(p, axis=-1, keepdims=True)
                acc_ref[rows] = alpha * acc_ref[rows] + jnp.dot(p.astype(BF16), v, preferred_element_type=F32)
                m_ref[rows] = m_new
            return carry

        lax.fori_loop(lo, qi + 1, body, 0)
        return [acc_ref[g * tq:(g + 1) * tq] / l_ref[g * tq:(g + 1) * tq] for g in range(NSA_G)]

    for kv in range(NSA_KV):
        o_sel = run(kv, ks_ref, vs_ref, 0, True)
        o_win = run(kv, kw_ref, vw_ref, jnp.maximum(qi - win_tiles, 0), False)
        for g in range(NSA_G):
            h = kv * NSA_G + g
            gate = [jax.nn.sigmoid(sm[:, SM_GATE + br * NSA_HEADS + h:SM_GATE + br * NSA_HEADS + h + 1])
                    for br in range(3)]
            cols = slice(h * NSA_DH, (h + 1) * NSA_DH)
            o_ref[0, :, cols] = gate[0] * oc_ref[0, :, cols] + gate[1] * o_sel[g] + gate[2] * o_win[g]


def _selwin(proj, selm, bias_tiles, o_cmp):
    b, t, _ = proj.shape
    tq = TQ
    nb = selm.shape[3]
    pair = NSA_KV * NSA_DH
    kvspec = lambda c6: pl.BlockSpec((1, t, pair), lambda i, j: (i, 0, C_KVN // pair + c6))
    return pl.pallas_call(
        _selwin_kernel,
        grid=(b, t // tq),
        in_specs=[pl.BlockSpec((1, tq, 1024), lambda i, j: (i, j, C_QN // 1024)),
                  kvspec(2), kvspec(3), kvspec(4), kvspec(5),
                  pl.BlockSpec((1, NSA_KV, tq, nb), lambda i, j: (i, 0, j, 0)),
                  pl.BlockSpec((NSA_HEADS, 3, tq, tq), lambda i, j: (0, 0, 0, 0)),
                  pl.BlockSpec((1, tq, 1024), lambda i, j: (i, j, 0)),
                  pl.BlockSpec((1, tq, LANE), lambda i, j: (i, j, C_SM // LANE))],
        out_specs=pl.BlockSpec((1, tq, NSA_HEADS * NSA_DH), lambda i, j: (i, j, 0)),
        out_shape=jax.ShapeDtypeStruct((b, t, NSA_HEADS * NSA_DH), F32),
        scratch_shapes=[pltpu.VMEM((NSA_G * tq, 1), F32), pltpu.VMEM((NSA_G * tq, 1), F32),
                        pltpu.VMEM((NSA_G * tq, NSA_DH), F32)],
        compiler_params=_cparams(("parallel", "arbitrary")),
        name="selwin",
    )(proj, proj, proj, proj, proj, selm, bias_tiles, o_cmp, proj)


def _nsa_sample_kernel(pt_ref, q_ref, kvn_ref, gt_ref, win_ref, bsel_ref, bwin_ref, bcmp_ref, bself_ref,
                       *rest, n_pages, n_top):
    del pt_ref
    sel_pages = rest[:n_pages]
    kvc_pages = rest[n_pages:2 * n_pages]
    o_ref = rest[2 * n_pages]
    kvc_s = rest[2 * n_pages + 1]
    n = n_pages * (PAGE_SIZE // NSA_BLK)
    past = n_pages * PAGE_SIZE
    wb = win_ref.shape[1]
    R = SUBLANE

    q8 = q_ref[0]
    rowkv = lax.broadcasted_iota(jnp.int32, (R, 1), 0) // NSA_G
    bself = bself_ref[:, 0:1]

    for i in range(n_pages):
        kvc_s[2 * i:2 * i + 2, :] = kvc_pages[i][0]
    kvc_s[n:LANE, :] = jnp.zeros((LANE - n, KV_ROW), F32)
    kvc = kvc_s[...]
    lane = lax.broadcasted_iota(jnp.int32, (R, LANE), 1)
    ii = lax.broadcasted_iota(jnp.int32, (LANE, LANE), 0)
    jj = lax.broadcasted_iota(jnp.int32, (LANE, LANE), 1)
    jrow = lax.broadcasted_iota(jnp.int32, (1, LANE), 1)
    eb = lax.broadcasted_iota(jnp.int32, (LANE, past), 0)
    ec = lax.broadcasted_iota(jnp.int32, (LANE, past), 1) // NSA_BLK
    expand = jnp.where(eb == ec, 1.0, 0.0).astype(BF16)
    wr = lax.broadcasted_iota(jnp.int32, (R, wb), 1)
    win_valid = (wb - wr) < NSA_WINDOW

    o_cmp, o_sel, o_win = [], [], []
    for kv in range(NSA_KV):
        kc = kvc[:, kv * NSA_DH:(kv + 1) * NSA_DH]
        vc = kvc[:, (NSA_KV + kv) * NSA_DH:(NSA_KV + kv + 1) * NSA_DH]
        lg = _mm3(q8, kc, NT) * NSA_SCALE + bcmp_ref[...]
        lg = jnp.where(lane < n, lg, NEG_INF)
        e = jnp.exp(lg - jnp.max(lg, axis=-1, keepdims=True))
        p = e / jnp.sum(e, axis=-1, keepdims=True)
        p = jnp.where(lane < n, p, 0.0)
        o_cmp.append(_mm(p, vc))
        imp = jnp.sum(jnp.where(rowkv == kv, p, 0.0), axis=0, keepdims=True)
        forced = jnp.where(jrow == 0, 1.0, 0.0) + jnp.where(jrow == n, 1.0, 0.0) + jnp.where(jrow == n - 1, 1.0, 0.0)
        score = jnp.where(jrow <= n, jnp.where(forced > 0.5, FORCE_SCORE, imp), -2.0)
        s_b = jnp.broadcast_to(score, (LANE, LANE))
        s_t = jnp.transpose(s_b)
        ahead = jnp.where(s_t > s_b, 1.0, jnp.where(s_t == s_b, jnp.where(ii < jj, 1.0, 0.0), 0.0))
        rank = jnp.sum(ahead, axis=0, keepdims=True)
        sel_row = jnp.where(rank < n_top, jnp.where(jrow <= n, 1.0, 0.0), 0.0)
        sel8 = jnp.broadcast_to(sel_row, (R, LANE))
        lgs = [_mm(q8, sel_pages[i][0, :, kv * NSA_DH:(kv + 1) * NSA_DH], NT) for i in range(n_pages)]
        lg = jnp.concatenate(lgs, axis=1) * NSA_SCALE + bsel_ref[...]
        picked = jnp.dot(sel8.astype(BF16), expand, preferred_element_type=F32)
        lg = jnp.where(picked > 0.5, lg, NEG_INF)
        k_new = kvn_ref[0, 2 * NSA_KV + kv:2 * NSA_KV + kv + 1, :]
        v_new = kvn_ref[0, 3 * NSA_KV + kv:3 * NSA_KV + kv + 1, :]
        l_new = jnp.sum(q8 * k_new, axis=-1, keepdims=True) * NSA_SCALE + bself
        l_new = jnp.where(sel8[:, n:n + 1] > 0.5, l_new, NEG_INF)
        m = jnp.maximum(jnp.max(lg, axis=-1, keepdims=True), l_new)
        e = jnp.exp(lg - m)
        e_new = jnp.exp(l_new - m)
        den = jnp.sum(e, axis=-1, keepdims=True) + e_new
        pv = e_new * v_new
        for i in range(n_pages):
            pv = pv + _mm(e[:, i * PAGE_SIZE:(i + 1) * PAGE_SIZE],
                          sel_pages[i][0, :, (NSA_KV + kv) * NSA_DH:(NSA_KV + kv + 1) * NSA_DH])
        o_sel.append(pv / den)
        lg = _mm(q8, win_ref[0, :, kv * NSA_DH:(kv + 1) * NSA_DH], NT) * NSA_SCALE + bwin_ref[...]
        lg = jnp.where(win_valid, lg, NEG_INF)
        k_new = kvn_ref[0, 4 * NSA_KV + kv:4 * NSA_KV + kv + 1, :]
        v_new = kvn_ref[0, 5 * NSA_KV + kv:5 * NSA_KV + kv + 1, :]
        l_new = jnp.sum(q8 * k_new, axis=-1, keepdims=True) * NSA_SCALE + bself
        m = jnp.maximum(jnp.max(lg, axis=-1, keepdims=True), l_new)
        e = jnp.exp(lg - m)
        e_new = jnp.exp(l_new - m)
        den = jnp.sum(e, axis=-1, keepdims=True) + e_new
        pv = e_new * v_new + _mm(e, win_ref[0, :, (NSA_KV + kv) * NSA_DH:(NSA_KV + kv + 1) * NSA_DH])
        o_win.append(pv / den)

    gate = jax.nn.sigmoid(gt_ref[0])
    pick = lambda pair: jnp.where(rowkv == 0, pair[0], pair[1])
    o_ref[0] = gate[:, 0:1] * pick(o_cmp) + gate[:, 1:2] * pick(o_sel) + gate[:, 2:3] * pick(o_win)


def _nsa_sample(page_table, q8, kvn, gates, win_cache, sel_pool, kvc_pool, bsel, bwin, bcmp, bself):
    db, n_pages = page_table.shape
    wb = win_cache.shape[1]
    past = n_pages * PAGE_SIZE
    n_blocks = past // NSA_BLK + 1
    kern = functools.partial(_nsa_sample_kernel, n_pages=n_pages, n_top=min(NSA_TOPN, n_blocks))
    fixed = lambda shape: pl.BlockSpec(shape, lambda i, pt: (0,) * len(shape))
    in_specs = [pl.BlockSpec((1, NSA_HEADS, NSA_DH), lambda i, pt: (i, 0, 0)),
                pl.BlockSpec((1, 6 * NSA_KV, NSA_DH), lambda i, pt: (i, 0, 0)),
                pl.BlockSpec((1, NSA_HEADS, 3), lambda i, pt: (i, 0, 0)),
                pl.BlockSpec((1, wb, KV_ROW), lambda i, pt: (i, 0, 0)),
                fixed((NSA_HEADS, past)), fixed((NSA_HEADS, wb)), fixed((NSA_HEADS, LANE)),
                fixed((NSA_HEADS, LANE))]
    for p in range(n_pages):
        in_specs.append(pl.BlockSpec((1, PAGE_SIZE, KV_ROW), lambda i, pt, p=p: (pt[i, p], 0, 0)))
    for p in range(n_pages):
        in_specs.append(pl.BlockSpec((1, PAGE_SIZE // NSA_BLK, KV_ROW), lambda i, pt, p=p: (pt[i, p], 0, 0)))
    return pl.pallas_call(
        kern,
        grid_spec=pltpu.PrefetchScalarGridSpec(
            num_scalar_prefetch=1, grid=(db,), in_specs=in_specs,
            out_specs=pl.BlockSpec((1, NSA_HEADS, NSA_DH), lambda i, pt: (i, 0, 0)),
            scratch_shapes=[pltpu.VMEM((LANE, KV_ROW), F32)]),
        out_shape=jax.ShapeDtypeStruct((db, NSA_HEADS, NSA_DH), F32),
        compiler_params=_cparams(("arbitrary",)),
        name="nsa_sample",
    )(page_table, q8, kvn, gates, win_cache, bsel, bwin, bcmp, bself,
      *([sel_pool] * n_pages), *([kvc_pool] * n_pages))


def _mix_kernel(oa_ref, ob_ref, ga_ref, gb_ref, x_ref, gt_ref, g_ref, woa_ref, wob_ref, wout_ref, o_ref):
    a = _mm(oa_ref[0], woa_ref[...])
    b = _mm(ob_ref[0], wob_ref[...])
    m = jax.nn.sigmoid(ga_ref[0]) * a + jax.nn.sigmoid(gb_ref[0]) * b
    mixed = _mm(m, wout_ref[...])
    o_ref[0] = x_ref[0] + gt_ref[0] * (_rms_rows(mixed) * g_ref[...])


def _mix(o_a, o_b, proj, x3, gt, g, w_oa, w_ob, w_out):
    bb, t, d = x3.shape
    tm = _pick(t, 512)
    row = lambda c: pl.BlockSpec((1, tm, d), lambda b, i: (b, i, c))
    wspec = pl.BlockSpec((d, d), lambda b, i: (0, 0))
    gtspec = (pl.BlockSpec((1, 1, d), lambda b, i: (b, 0, 0)) if gt.shape[1] == 1
              else pl.BlockSpec((1, tm, d), lambda b, i: (b, i, 0)))
    return pl.pallas_call(
        _mix_kernel,
        grid=(bb, t // tm),
        in_specs=[row(0), row(0), row(C_GM // d), row(C_GM // d + 1), row(0), gtspec,
                  pl.BlockSpec((1, d), lambda b, i: (0, 0)), wspec, wspec, wspec],
        out_specs=row(0),
        out_shape=jax.ShapeDtypeStruct((bb, t, d), F32),
        compiler_params=_cparams(("parallel", "parallel")),
        name="mix",
    )(o_a, o_b, proj, proj, x3, gt, g.reshape(1, d), w_oa, w_ob, w_out)


def _ffn_kernel(x_ref, g2_ref, sc_ref, sh_ref, gt_ref, g3_ref, wg_ref, wu_ref, wd_ref, o_ref, h_ref, acc_ref):
    f = pl.program_id(2)

    @pl.when(f == 0)
    def _():
        y = _rms_rows(x_ref[0]) * g2_ref[...]
        h_ref[...] = (y * (1.0 + sc_ref[0]) + sh_ref[0]).astype(BF16)
        acc_ref[...] = jnp.zeros(acc_ref.shape, F32)

    h = h_ref[...]
    gate = jnp.dot(h, wg_ref[...], preferred_element_type=F32)
    up = jnp.dot(h, wu_ref[...], preferred_element_type=F32)
    acc_ref[...] += _mm(_silu(gate) * up, wd_ref[...])

    @pl.when(f == pl.num_programs(2) - 1)
    def _():
        o_ref[0] = x_ref[0] + gt_ref[0] * (_rms_rows(acc_ref[...]) * g3_ref[...])


def _ffn(x3, g2, sc, sh, gt, g3, w_gu, w_down):
    bb, t, d = x3.shape
    tm = _pick(t, 512)
    nf = 2
    tf = D_FF // nf
    assert tf % LANE == 0
    return pl.pallas_call(
        _ffn_kernel,
        grid=(bb, t // tm, nf),
        in_specs=[pl.BlockSpec((1, tm, d), lambda b, i, f: (b, i, 0)),
                  pl.BlockSpec((1, d), lambda b, i, f: (0, 0)),
                  _mod_spec(sc, tm), _mod_spec(sh, tm), _mod_spec(gt, tm),
                  pl.BlockSpec((1, d), lambda b, i, f: (0, 0)),
                  pl.BlockSpec((d, tf), lambda b, i, f: (0, f)),
                  pl.BlockSpec((d, tf), lambda b, i, f: (0, nf + f)),
                  pl.BlockSpec((tf, d), lambda b, i, f: (f, 0))],
        out_specs=pl.BlockSpec((1, tm, d), lambda b, i, f: (b, i, 0)),
        out_shape=jax.ShapeDtypeStruct((bb, t, d), F32),
        scratch_shapes=[pltpu.VMEM((tm, d), BF16), pltpu.VMEM((tm, d), F32)],
        compiler_params=_cparams(("parallel", "parallel", "arbitrary")),
        name="ffn",
    )(x3, g2.reshape(1, d), sc, sh, gt, g3.reshape(1, d), w_gu, w_gu, w_down)


def _bucket_np(n):
    n = np.maximum(n, 0)
    exact = NUM_BUCKETS // 2
    big = exact + (np.log(np.maximum(n, 1).astype(np.float32) / np.float32(exact))
                   / np.float32(math.log(MAX_DISTANCE / exact))
                   * np.float32(NUM_BUCKETS - exact)).astype(np.int32)
    return np.where(n < exact, n, np.minimum(big, NUM_BUCKETS - 1)).astype(np.int32)


def _bias_tables(rel_bias, t, past, wb):
    def table(dist):
        return jnp.moveaxis(jnp.take(rel_bias, jnp.asarray(_bucket_np(dist)), axis=0), -1, 0)

    n = t // NSA_BLK
    ends = np.arange(n) * NSA_BLK + (NSA_BLK - 1)
    cmp_p = table(np.arange(t)[:, None] - ends[None, :])
    ar = np.arange(TQ)
    tiles = table(np.stack([d * TQ + ar[:, None] - ar[None, :] for d in range(3)]))
    sel_s = table(past - np.arange(past))
    win_s = table(wb - np.arange(wb))
    n_s = past // NSA_BLK
    cmp_s = table(np.pad(past - (np.arange(n_s) * NSA_BLK + NSA_BLK - 1), (0, LANE - n_s)))
    self_s = table(np.zeros((LANE,), np.int64))
    return cmp_p, tiles, sel_s, win_s, cmp_s, self_s


def _regroup_w_in(w):
    o = np.cumsum((0, CONV_DIM, DN_HEADS * DN_DV, DN_HEADS, DN_HEADS, NSA_HEADS * NSA_DH,
                   6 * NSA_KV * NSA_DH, 3 * NSA_HEADS, 2 * D_MODEL))
    qkv_z, b, a, qn, kvn, gn, gm = (w[:, o[0]:o[2]], w[:, o[2]:o[3]], w[:, o[3]:o[4]], w[:, o[4]:o[5]],
                                    w[:, o[5]:o[6]], w[:, o[6]:o[7]], w[:, o[7]:o[8]])
    pad = jnp.zeros((w.shape[0], N_PROJ - C_SM - 5 * NSA_HEADS), w.dtype)
    return jnp.concatenate([qkv_z, qn, gm, kvn, b, a, gn, pad], axis=1).astype(BF16)


def _layer_params(l, w_in, conv_w, a_log, dt_bias, onorm_g, w_oa, w_ob, w_out, cmp_pe, cmp_w1, cmp_w2,
                  w_gu, w_down):
    hp = jnp.zeros((SUBLANE, LANE), F32)
    hp = hp.at[0, SM_A:SM_A + DN_HEADS].set(a_log[l]).at[1, SM_A:SM_A + DN_HEADS].set(dt_bias[l])
    return dict(
        w_in=_regroup_w_in(w_in[l]), cw=conv_w[l], hp=hp, on=onorm_g[l].reshape(1, DN_DV),
        w_oa=w_oa[l].astype(BF16), w_ob=w_ob[l].astype(BF16), w_out=w_out[l].astype(BF16),
        pe=jnp.reshape(cmp_pe[l], (1, NSA_BLK * 2 * NSA_DH)),
        w1=jnp.transpose(cmp_w1[l], (1, 0, 2, 3)).reshape(NSA_BLK * 2 * NSA_DH, NSA_CMP_HID).astype(BF16),
        w2=cmp_w2[l].reshape(2 * NSA_CMP_HID, NSA_DH).astype(BF16),
        w_gu=w_gu[l].astype(BF16), w_down=w_down[l].astype(BF16))


def _mods(mod, per_token):
    parts = [mod[:, i * D_MODEL:(i + 1) * D_MODEL] for i in range(6)]
    return [p[None, :, :] if per_token else p[:, None, :] for p in parts]


def _tail(x3, o_a, o_b, proj, mods, ng, P):
    sh1, sc1, gt1, sh2, sc2, gt2 = mods
    x1 = _mix(o_a, o_b, proj, x3, gt1, ng[1], P['w_oa'], P['w_ob'], P['w_out'])
    return _ffn(x1, ng[2], sc2, sh2, gt2, ng[3], P['w_gu'], P['w_down'])


def _kv_rows(proj, branch):
    b, t, _ = proj.shape
    c0 = C_KVN + branch * KV_ROW
    return proj[:, :, c0:c0 + KV_ROW].reshape(b, t, 2, NSA_KV, NSA_DH)


def _prompt_layer(x3, mods, ng, P, tabs):
    b, t, _ = x3.shape
    proj = _proj(x3, ng[0], mods[1], mods[0], P['w_in'])
    o_a, s_new = _dn_prompt(proj, P['cw'], P['hp'], P['on'])
    cmp_rows, sel_rows, win_rows = _kv_rows(proj, 0), _kv_rows(proj, 1), _kv_rows(proj, 2)
    n = t // NSA_BLK
    kvc = _compress(cmp_rows.reshape(b * n, NSA_BLK * KV_ROW), P['pe'], P['w1'], P['w2']).reshape(b, n, KV_ROW)
    o_cmp, selm = _cmp_select(proj, kvc, tabs[0])
    o_b = _selwin(proj, selm, tabs[1], o_cmp)
    y = _tail(x3, o_a, o_b, proj, mods, ng, P)
    conv_new = proj[:, t - (CONV_W - 1):, C_QKV:C_QKV + CONV_DIM]
    return y, (cmp_rows, sel_rows, win_rows[:, t - min(NSA_WINDOW, t):], s_new, conv_new)


def _sample_layer(x3, mods, ng, P, tabs, page_table, cmp_pool, sel_pool, win_cache, state, conv_buf):
    db = x3.shape[1]
    proj = _proj(x3, ng[0], mods[1], mods[0], P['w_in'])
    rows = proj.reshape(db, 1, N_PROJ)
    o_a, s_new = _dn_sample(rows, conv_buf, state, P['cw'], P['hp'], P['on'])
    n_pool = cmp_pool.shape[0]
    halves = PAGE_SIZE // NSA_BLK
    kvc_pool = _compress(cmp_pool.reshape(n_pool * halves, NSA_BLK * KV_ROW), P['pe'], P['w1'], P['w2'])
    q8 = proj[0, :, C_QN:C_QN + NSA_HEADS * NSA_DH].reshape(db, NSA_HEADS, NSA_DH)
    kvn = proj[0, :, C_KVN:C_KVN + 6 * NSA_KV * NSA_DH].reshape(db, 6 * NSA_KV, NSA_DH)
    gates = jnp.transpose(proj[0, :, C_SM + SM_GATE:C_SM + SM_GATE + 3 * NSA_HEADS].reshape(db, 3, NSA_HEADS),
                          (0, 2, 1))
    wb = win_cache.shape[1]
    o_b = _nsa_sample(page_table, q8, kvn, gates, win_cache.reshape(db, wb, KV_ROW),
                      sel_pool.reshape(n_pool, PAGE_SIZE, KV_ROW), kvc_pool.reshape(n_pool, halves, KV_ROW),
                      tabs[2], tabs[3], tabs[4], tabs[5])
    y = _tail(x3, o_a.reshape(1, db, D_MODEL), o_b.reshape(1, db, D_MODEL), proj, mods, ng, P)
    qkv_new = proj[0, :, C_QKV:C_QKV + CONV_DIM]
    conv_new = jnp.concatenate([conv_buf[:, 1:], qkv_new[:, None, :]], axis=1)
    kv_new = kvn.reshape(db, 1, 6, NSA_KV, NSA_DH)
    win_new = jnp.concatenate([win_cache[:, 1:], kv_new[:, :, 4:6]], axis=1)
    return y, (kv_new[:, :, 0:2], kv_new[:, :, 2:4], win_new, s_new, conv_new)


def kernel(x_prompt, x_sample, cache_cmp_kv, cache_sel_kv, cache_win_kv, state_delta, state_conv, page_table, c_prompt, c_sample, w_ada, b_ada, norm_g, w_in, conv_w, a_log, dt_bias, onorm_g, w_oa, w_ob, w_out, cmp_pe, cmp_w1, cmp_w2, rel_bias, w_gu, w_down):
    depth = w_in.shape[0]
    b, t, d = x_prompt.shape
    db, ts, _ = x_sample.shape
    assert ts == 1 and d == D_MODEL and t % TQ == 0
    n_pages = page_table.shape[1]
    assert cache_cmp_kv.shape[2] == PAGE_SIZE
    past = n_pages * PAGE_SIZE
    wb = cache_win_kv.shape[2]
    tabs = _bias_tables(rel_bias, t, past, wb)
    c_all = jnp.concatenate([c_prompt, c_sample], axis=0)
    pad_rows = (-c_all.shape[0]) % SUBLANE
    c_all = jnp.pad(c_all, ((0, pad_rows), (0, 0)))
    yp = x_prompt
    ys = x_sample.reshape(1, db, d)
    p_st, s_st = [], []
    for l in range(depth):
        P = _layer_params(l, w_in, conv_w, a_log, dt_bias, onorm_g, w_oa, w_ob, w_out, cmp_pe, cmp_w1,
                          cmp_w2, w_gu, w_down)
        mod = _ada(c_all, w_ada[l], b_ada[l])
        yp, st = _prompt_layer(yp, _mods(mod[:b], False), norm_g[l], P, tabs)
        p_st.append(st)
        ys, st = _sample_layer(ys, _mods(mod[b:b + db], True), norm_g[l], P, tabs, page_table,
                               cache_cmp_kv[l], cache_sel_kv[l], cache_win_kv[l], state_delta[l], state_conv[l])
        s_st.append(st)
    stk = lambda states, i: jnp.stack([s[i] for s in states])
    return (yp, ys.reshape(db, 1, d),
            stk(p_st, 0), stk(p_st, 1), stk(p_st, 2), stk(p_st, 3), stk(p_st, 4),
            stk(s_st, 0), stk(s_st, 1), stk(s_st, 2), stk(s_st, 3), stk(s_st, 4))
```

```python
import functools
import math

import numpy as np
import jax
import jax.numpy as jnp
from jax import lax
from jax.experimental import pallas as pl
from jax.experimental.pallas import tpu as pltpu

F32 = jnp.float32
BF16 = jnp.bfloat16

D_MODEL = 1024
EPS = 1e-6
DN_HEADS = 8
DN_DK = 128
DN_DV = 128
CONV_W = 4
CONV_DIM = 2 * DN_HEADS * DN_DK + DN_HEADS * DN_DV
DN_CHUNK = 64
NSA_HEADS = 8
NSA_KV = 2
NSA_G = NSA_HEADS // NSA_KV
NSA_DH = 128
NSA_SCALE = NSA_DH ** -0.5
NSA_BLK = 64
NSA_TOPN = 16
NSA_WINDOW = 512
NSA_CMP_HID = 128
FORCE_SCORE = 1e4
NEG_INF = -1e30
NUM_BUCKETS = 32
MAX_DISTANCE = 128
D_FF = 256 * (-(-8 * D_MODEL // (3 * 256)))
PAGE_SIZE = 128

C_QKV = 0
C_Z = 3072
C_QN = 4096
C_GM = 5120
C_KVN = 7168
C_SM = 8704
N_PROJ = 8832
SM_BETA = 0
SM_A = 8
SM_GATE = 16
KV_ROW = 2 * NSA_KV * NSA_DH
ROWS_PER_TOK = 2 * NSA_KV

LANE = 128
SUBLANE = 8
VMEM_LIMIT = 48 * 1024 * 1024
TQ = 128
WIN_TILES = NSA_WINDOW // TQ + 1


def _cparams(sem):
    return pltpu.CompilerParams(dimension_semantics=sem, vmem_limit_bytes=VMEM_LIMIT)


def _pick(n, pref, mult=SUBLANE):
    for t in range(min(pref, n), 0, -1):
        if n % t == 0 and t % mult == 0:
            return t
    return n


def _mm(a, b, dims=((1,), (0,))):
    return lax.dot_general(a.astype(BF16), b.astype(BF16), (dims, ((), ())),
                           preferred_element_type=F32)


NN = ((1,), (0,))
NT = ((1,), (1,))
TN = ((0,), (0,))


def _hilo(a):
    hi = a.astype(BF16)
    lo = (a - hi.astype(F32)).astype(BF16)
    return hi, lo


def _mm3(a, b, dims=NN):
    ah, al = _hilo(a)
    bh, bl = _hilo(b)
    return _mm(ah, bh, dims) + (_mm(ah, bl, dims) + _mm(al, bh, dims))


def _mm_exact_lhs(a01, b):
    b0 = b.astype(BF16)
    r1 = b - b0.astype(F32)
    b1 = r1.astype(BF16)
    b2 = (r1 - b1.astype(F32)).astype(BF16)
    return _mm(a01, b0) + (_mm(a01, b1) + _mm(a01, b2))


def _silu(x):
    return x * jax.nn.sigmoid(x)


def _softplus(x):
    return jnp.maximum(x, 0.0) + jnp.log1p(jnp.exp(-jnp.abs(x)))


def _rms_rows(x):
    return x * lax.rsqrt(jnp.mean(x * x, axis=-1, keepdims=True) + EPS)


def _ada_kernel(c_ref, w_ref, b_ref, o_ref):
    o_ref[...] = _mm(_silu(c_ref[...]), w_ref[...]) + b_ref[...]


def _ada(c_all, w, b):
    m, d = c_all.shape
    n = w.shape[1]
    tn = _pick(n, 1536, LANE)
    return pl.pallas_call(
        _ada_kernel,
        grid=(n // tn,),
        in_specs=[pl.BlockSpec((m, d), lambda j: (0, 0)),
                  pl.BlockSpec((d, tn), lambda j: (0, j)),
                  pl.BlockSpec((1, tn), lambda j: (0, j))],
        out_specs=pl.BlockSpec((m, tn), lambda j: (0, j)),
        out_shape=jax.ShapeDtypeStruct((m, n), F32),
        compiler_params=_cparams(("parallel",)),
        name="ada",
    )(c_all, w, b.reshape(1, n))


def _proj_kernel(x_ref, g_ref, sc_ref, sh_ref, w_ref, o_ref, h_ref):
    @pl.when(pl.program_id(2) == 0)
    def _():
        y = _rms_rows(x_ref[0]) * g_ref[...]
        h_ref[...] = (y * (1.0 + sc_ref[0]) + sh_ref[0]).astype(BF16)

    o_ref[0] = jnp.dot(h_ref[...], w_ref[...], preferred_element_type=F32)


def _mod_spec(mod, tm):
    if mod.shape[1] == 1:
        return pl.BlockSpec((1, 1, D_MODEL), lambda b, i, j: (b, 0, 0))
    return pl.BlockSpec((1, tm, D_MODEL), lambda b, i, j: (b, i, 0))


def _proj(x3, g, sc, sh, w):
    bb, t, d = x3.shape
    n = w.shape[1]
    tm = _pick(t, 512)
    tn = _pick(n, 2944, LANE)
    return pl.pallas_call(
        _proj_kernel,
        grid=(bb, t // tm, n // tn),
        in_specs=[pl.BlockSpec((1, tm, d), lambda b, i, j: (b, i, 0)),
                  pl.BlockSpec((1, d), lambda b, i, j: (0, 0)),
                  _mod_spec(sc, tm), _mod_spec(sh, tm),
                  pl.BlockSpec((d, tn), lambda b, i, j: (0, j))],
        out_specs=pl.BlockSpec((1, tm, tn), lambda b, i, j: (b, i, j)),
        out_shape=jax.ShapeDtypeStruct((bb, t, n), F32),
        scratch_shapes=[pltpu.VMEM((tm, d), BF16)],
        compiler_params=_cparams(("parallel", "parallel", "arbitrary")),
        name="proj",
    )(x3, g.reshape(1, d), sc, sh, w)


def _dn_prompt_kernel(qkv_ref, z_ref, sm_ref, cw_ref, hp_ref, on_ref, o_ref, s_ref, buf_ref, st_ref):
    c = pl.program_id(1)
    C = DN_CHUNK
    hist = SUBLANE

    @pl.when(c == 0)
    def _():
        buf_ref[0:hist, :] = jnp.zeros((hist, CONV_DIM), F32)
        st_ref[...] = jnp.zeros(st_ref.shape, F32)

    buf_ref[hist:hist + C, :] = qkv_ref[0]

    def conv_act(c0):
        acc = None
        for i in range(CONV_W):
            r0 = hist - (CONV_W - 1) + i
            term = buf_ref[r0:r0 + C, c0:c0 + LANE] * cw_ref[i:i + 1, c0:c0 + LANE]
            acc = term if acc is None else acc + term
        return _silu(acc)

    sm = sm_ref[0]
    beta_all = jax.nn.sigmoid(sm)
    g_all = -jnp.exp(hp_ref[0:1, :]) * _softplus(sm + hp_ref[1:2, :])
    ri = lax.broadcasted_iota(jnp.int32, (C, C), 0)
    ci = lax.broadcasted_iota(jnp.int32, (C, C), 1)
    incl = ri >= ci
    strict = ri > ci
    tri = jnp.where(incl, 1.0, 0.0).astype(BF16)
    eye = jnp.where(ri == ci, 1.0, 0.0)
    gc_all = _mm_exact_lhs(tri, g_all)
    gc_t = jnp.transpose(jnp.concatenate([gc_all, jnp.zeros((LANE - C, LANE), F32)], axis=0))
    eg_all = jnp.exp(gc_all)

    H = range(DN_HEADS)
    q = [conv_act(h * DN_DK) for h in H]
    k = [conv_act(DN_HEADS * DN_DK + h * DN_DK) for h in H]
    v = [conv_act(2 * DN_HEADS * DN_DK + h * DN_DV) for h in H]
    q = [x * lax.rsqrt(jnp.sum(x * x, axis=-1, keepdims=True) + EPS) * (DN_DK ** -0.5) for x in q]
    k = [x * lax.rsqrt(jnp.sum(x * x, axis=-1, keepdims=True) + EPS) for x in k]
    beta = [beta_all[:, SM_BETA + h:SM_BETA + h + 1] for h in H]
    gc = [gc_all[:, SM_A + h:SM_A + h + 1] for h in H]
    eg = [eg_all[:, SM_A + h:SM_A + h + 1] for h in H]
    g_last = [gc_all[C - 1:C, SM_A + h:SM_A + h + 1] for h in H]
    decay = [jnp.exp(jnp.where(incl, gc[h] - gc_t[SM_A + h:SM_A + h + 1, 0:C], NEG_INF)) for h in H]
    kb = [k[h] * beta[h] for h in H]
    a = [jnp.where(strict, _mm(kb[h], k[h], NT) * decay[h], 0.0) for h in H]
    qk = [_mm(q[h], k[h], NT) * decay[h] for h in H]
    t = [eye - a[h] for h in H]
    p = [_mm3(a[h], a[h]) for h in H]
    steps = int(math.log2(C)) - 1
    for s in range(steps):
        t = [t[h] + _mm3(t[h], p[h]) for h in H]
        if s + 1 < steps:
            p = [_mm3(p[h], p[h]) for h in H]
    u = [_mm3(t[h], v[h] * beta[h]) for h in H]
    w = [_mm3(t[h], kb[h] * eg[h]) for h in H]
    s_old = [st_ref[h] for h in H]
    v_new = [u[h] - _mm(w[h], s_old[h]) for h in H]
    o = [_mm(q[h] * eg[h], s_old[h]) + _mm(qk[h], v_new[h]) for h in H]
    for h in H:
        st_ref[h] = s_old[h] * jnp.exp(g_last[h]) + _mm(k[h] * jnp.exp(g_last[h] - gc[h]), v_new[h], TN)
    for h in H:
        z = z_ref[0, :, h * DN_DV:(h + 1) * DN_DV]
        o_ref[0, :, h * DN_DV:(h + 1) * DN_DV] = _rms_rows(o[h]) * on_ref[...] * _silu(z)

    buf_ref[0:hist, :] = buf_ref[C:C + hist, :]

    @pl.when(c == pl.num_programs(1) - 1)
    def _():
        s_ref[0] = st_ref[...]


def _dn_prompt(proj, cw, hp, on):
    b, t, _ = proj.shape
    C = DN_CHUNK
    assert t % C == 0
    return pl.pallas_call(
        _dn_prompt_kernel,
        grid=(b, t // C),
        in_specs=[pl.BlockSpec((1, C, CONV_DIM), lambda i, c: (i, c, C_QKV // CONV_DIM)),
                  pl.BlockSpec((1, C, 1024), lambda i, c: (i, c, C_Z // 1024)),
                  pl.BlockSpec((1, C, LANE), lambda i, c: (i, c, C_SM // LANE)),
                  pl.BlockSpec((CONV_W, CONV_DIM), lambda i, c: (0, 0)),
                  pl.BlockSpec((SUBLANE, LANE), lambda i, c: (0, 0)),
                  pl.BlockSpec((1, DN_DV), lambda i, c: (0, 0))],
        out_specs=[pl.BlockSpec((1, C, DN_HEADS * DN_DV), lambda i, c: (i, c, 0)),
                   pl.BlockSpec((1, DN_HEADS, DN_DK, DN_DV), lambda i, c: (i, 0, 0, 0))],
        out_shape=[jax.ShapeDtypeStruct((b, t, DN_HEADS * DN_DV), F32),
                   jax.ShapeDtypeStruct((b, DN_HEADS, DN_DK, DN_DV), F32)],
        scratch_shapes=[pltpu.VMEM((C + SUBLANE, CONV_DIM), F32),
                        pltpu.VMEM((DN_HEADS, DN_DK, DN_DV), F32)],
        compiler_params=_cparams(("parallel", "arbitrary")),
        name="dn_prompt",
    )(proj, proj, proj, cw, hp, on)


def _dn_sample_kernel(qkv_ref, z_ref, sm_ref, cb_ref, cw_ref, hp_ref, on_ref, sin_ref, o_ref, sout_ref):
    xb = cb_ref[0, 0]
    conv = qkv_ref[0] * cw_ref[CONV_W - 1:CONV_W, :]
    for i in range(CONV_W - 1):
        conv = conv + xb[i:i + 1, :] * cw_ref[i:i + 1, :]
    act = _silu(conv)
    sm = sm_ref[0]
    beta_all = jax.nn.sigmoid(sm)
    eg_all = jnp.exp(-jnp.exp(hp_ref[0:1, :]) * _softplus(sm + hp_ref[1:2, :]))
    row = lax.broadcasted_iota(jnp.int32, (SUBLANE, DN_DK), 0)
    rows = lambda x: jnp.broadcast_to(x, row.shape)
    H = range(DN_HEADS)
    q = [act[:, h * DN_DK:(h + 1) * DN_DK] for h in H]
    k = [act[:, (DN_HEADS + h) * DN_DK:(DN_HEADS + h + 1) * DN_DK] for h in H]
    v = [act[:, 2 * DN_HEADS * DN_DK + h * DN_DV:2 * DN_HEADS * DN_DK + (h + 1) * DN_DV] for h in H]
    q = [x * lax.rsqrt(jnp.sum(x * x, axis=-1, keepdims=True) + EPS) * (DN_DK ** -0.5) for x in q]
    k = [x * lax.rsqrt(jnp.sum(x * x, axis=-1, keepdims=True) + EPS) for x in k]
    beta = [beta_all[:, SM_BETA + h:SM_BETA + h + 1] for h in H]
    eg = [eg_all[:, SM_A + h:SM_A + h + 1] for h in H]
    s_old = [sin_ref[0, 0, h] for h in H]
    kq = [jnp.where(row == 0, rows(k[h]), jnp.where(row == 1, rows(q[h]), 0.0)) for h in H]
    r = [_mm3(kq[h], s_old[h]) for h in H]
    v_new = [beta[h] * (v[h] - eg[h] * r[h][0:1, :]) for h in H]
    qk = [jnp.sum(q[h] * k[h], axis=-1, keepdims=True) for h in H]
    o = [eg[h] * r[h][1:2, :] + qk[h] * v_new[h] for h in H]
    outer = [_mm3(jnp.where(row == 0, rows(k[h]), 0.0), rows(v_new[h]), TN) for h in H]
    for h in H:
        sout_ref[0, h] = s_old[h] * eg[h] + outer[h]
    for h in H:
        z = z_ref[0, :, h * DN_DV:(h + 1) * DN_DV]
        o_ref[0, :, h * DN_DV:(h + 1) * DN_DV] = _rms_rows(o[h]) * on_ref[...] * _silu(z)


def _dn_sample(proj_rows, conv_all, state_all, layer, cw, hp, on):
    db = proj_rows.shape[0]
    return pl.pallas_call(
        _dn_sample_kernel,
        grid=(db,),
        in_specs=[pl.BlockSpec((1, 1, CONV_DIM), lambda i: (i, 0, C_QKV // CONV_DIM)),
                  pl.BlockSpec((1, 1, 1024), lambda i: (i, 0, C_Z // 1024)),
                  pl.BlockSpec((1, 1, LANE), lambda i: (i, 0, C_SM // LANE)),
                  pl.BlockSpec((1, 1, CONV_W - 1, CONV_DIM), lambda i: (layer, i, 0, 0)),
                  pl.BlockSpec((CONV_W, CONV_DIM), lambda i: (0, 0)),
                  pl.BlockSpec((SUBLANE, LANE), lambda i: (0, 0)),
                  pl.BlockSpec((1, DN_DV), lambda i: (0, 0)),
                  pl.BlockSpec((1, 1, DN_HEADS, DN_DK, DN_DV), lambda i: (layer, i, 0, 0, 0))],
        out_specs=[pl.BlockSpec((1, 1, DN_HEADS * DN_DV), lambda i: (i, 0, 0)),
                   pl.BlockSpec((1, DN_HEADS, DN_DK, DN_DV), lambda i: (i, 0, 0, 0))],
        out_shape=[jax.ShapeDtypeStruct((db, 1, DN_HEADS * DN_DV), F32),
                   jax.ShapeDtypeStruct((db, DN_HEADS, DN_DK, DN_DV), F32)],
        compiler_params=_cparams(("parallel",)),
        name="dn_sample",
    )(proj_rows, proj_rows, proj_rows, conv_all, cw, hp, on, state_all)


BLK_ROWS = NSA_BLK * ROWS_PER_TOK


def _compress_kernel(x_ref, pe_ref, w1_ref, w2_ref, o_ref):
    tb = x_ref.shape[0] // BLK_ROWS
    for ck in range(ROWS_PER_TOK):
        c = ck // NSA_KV
        acc = jnp.zeros((tb, NSA_CMP_HID), F32)
        for l in range(NSA_BLK):
            x = (x_ref[pl.ds(l * ROWS_PER_TOK + ck, tb, stride=BLK_ROWS), :]
                 + pe_ref[:, (l * 2 + c) * NSA_DH:(l * 2 + c + 1) * NSA_DH])
            acc = acc + _mm(x, w1_ref[(l * 2 + c) * NSA_DH:(l * 2 + c + 1) * NSA_DH, :])
        o_ref[:, ck * NSA_DH:(ck + 1) * NSA_DH] = _mm(
            _silu(acc), w2_ref[c * NSA_CMP_HID:(c + 1) * NSA_CMP_HID, :])


def _compress(x_rows, row0, r, pe_flat, w1_r, w2_r):
    tb = _pick(r, 64)
    assert row0 % tb == 0
    return pl.pallas_call(
        _compress_kernel,
        grid=(r // tb,),
        in_specs=[pl.BlockSpec((tb * BLK_ROWS, NSA_DH), lambda i: (row0 // tb + i, 0)),
                  pl.BlockSpec((1, NSA_BLK * 2 * NSA_DH), lambda i: (0, 0)),
                  pl.BlockSpec((NSA_BLK * 2 * NSA_DH, NSA_CMP_HID), lambda i: (0, 0)),
                  pl.BlockSpec((2 * NSA_CMP_HID, NSA_DH), lambda i: (0, 0))],
        out_specs=pl.BlockSpec((tb, KV_ROW), lambda i: (i, 0)),
        out_shape=jax.ShapeDtypeStruct((r, KV_ROW), F32),
        compiler_params=_cparams(("parallel",)),
        name="compress",
    )(x_rows, pe_flat, w1_r, w2_r)


def _cmp_select_kernel(q_ref, kvc_ref, bias_ref, o_ref, selt_ref, *, n_top):
    qi = pl.program_id(1)
    tq = q_ref.shape[1]
    n = kvc_ref.shape[1]
    qpos = qi * tq + lax.broadcasted_iota(jnp.int32, (n, tq), 1)
    jn = lax.broadcasted_iota(jnp.int32, (n, tq), 0)
    valid = qpos >= jn * NSA_BLK + (NSA_BLK - 1)
    validf = jnp.where(valid, 1.0, 0.0)
    cur = qpos // NSA_BLK
    forced = jnp.where(jn == 0, 1.0, 0.0) + jnp.where(jn == cur, 1.0, 0.0) + jnp.where(jn == cur - 1, 1.0, 0.0)
    kvc = kvc_ref[0]
    kc = [kvc[:, kv * NSA_DH:(kv + 1) * NSA_DH] for kv in range(NSA_KV)]
    vc = [kvc[:, (NSA_KV + kv) * NSA_DH:(NSA_KV + kv + 1) * NSA_DH] for kv in range(NSA_KV)]
    HH = range(NSA_HEADS)
    lg = [_mm3(kc[h // NSA_G], q_ref[0, :, h * NSA_DH:(h + 1) * NSA_DH], NT) * NSA_SCALE + bias_ref[h]
          for h in HH]
    lg = [jnp.where(valid, x, NEG_INF) for x in lg]
    e = [jnp.exp(x - jnp.max(x, axis=0, keepdims=True)) for x in lg]
    p = [x / jnp.sum(x, axis=0, keepdims=True) * validf for x in e]
    for h in HH:
        o_ref[0, :, h * NSA_DH:(h + 1) * NSA_DH] = _mm(p[h], vc[h // NSA_G], TN)
    for kv in range(NSA_KV):
        imp = p[kv * NSA_G]
        for g in range(1, NSA_G):
            imp = imp + p[kv * NSA_G + g]
        score = jnp.where(jn <= cur, jnp.where(forced > 0.5, FORCE_SCORE, imp), -1.0)
        rank = jnp.zeros((n, tq), F32)
        for i in range(n):
            si = score[i:i + 1, :]
            rank = rank + jnp.where(si > score, 1.0, jnp.where(si == score, jnp.where(jn > i, 1.0, 0.0), 0.0))
        selt_ref[0, kv] = jnp.where(rank < n_top, 1.0, 0.0)


def _cmp_select(proj, kvc, bias_cmp):
    b, t, _ = proj.shape
    n = kvc.shape[1]
    tq = TQ
    kern = functools.partial(_cmp_select_kernel, n_top=min(NSA_TOPN, n))
    return pl.pallas_call(
        kern,
        grid=(b, t // tq),
        in_specs=[pl.BlockSpec((1, tq, 1024), lambda i, j: (i, j, C_QN // 1024)),
                  pl.BlockSpec((1, n, KV_ROW), lambda i, j: (i, 0, 0)),
                  pl.BlockSpec((NSA_HEADS, n, tq), lambda i, j: (0, 0, j))],
        out_specs=[pl.BlockSpec((1, tq, NSA_HEADS * NSA_DH), lambda i, j: (i, j, 0)),
                   pl.BlockSpec((1, NSA_KV, n, tq), lambda i, j: (i, 0, 0, j))],
        out_shape=[jax.ShapeDtypeStruct((b, t, NSA_HEADS * NSA_DH), F32),
                   jax.ShapeDtypeStruct((b, NSA_KV, n, t), F32)],
        compiler_params=_cparams(("parallel", "parallel")),
        name="cmp_select",
    )(proj, kvc, bias_cmp)


def _selwin_kernel(q_ref, ks_ref, vs_ref, kw_ref, vw_ref, selt_ref, bt_ref, oc_ref, sm_ref, o_ref, os_ref,
                   *, kbs):
    qi = pl.program_id(1)
    tq = TQ
    t = ks_ref.shape[1]
    nb = selt_ref.shape[2]
    sm = sm_ref[0]
    kvcols = lambda kv: slice(kv * NSA_DH, (kv + 1) * NSA_DH)
    q4 = [jnp.concatenate([q_ref[0, :, (kv * NSA_G + g) * NSA_DH:(kv * NSA_G + g + 1) * NSA_DH]
                           for g in range(NSA_G)], axis=0).astype(BF16) for kv in range(NSA_KV)]

    def attend(kv, k, v, allowed, tile_idx):
        s = lax.dot_general(q4[kv], k, (NT, ((), ())), preferred_element_type=F32) * NSA_SCALE
        outs = []
        for g in range(NSA_G):
            h = kv * NSA_G + g
            bias = jnp.concatenate([bt_ref[h, ti] for ti in tile_idx], axis=1)
            sg = jnp.where(allowed, s[g * tq:(g + 1) * tq] + bias, NEG_INF)
            e = jnp.exp(sg - jnp.max(sg, axis=-1, keepdims=True))
            den = jnp.sum(e, axis=-1, keepdims=True)
            outs.append(jnp.dot(e.astype(BF16), v, preferred_element_type=F32) / den)
        return outs

    def sel_branch(length):
        qpos = qi * tq + lax.broadcasted_iota(jnp.int32, (tq, length), 0)
        kpos = lax.broadcasted_iota(jnp.int32, (tq, length), 1)
        eb = lax.broadcasted_iota(jnp.int32, (nb, length), 0)
        ec = lax.broadcasted_iota(jnp.int32, (nb, length), 1) // NSA_BLK
        onehot = jnp.where(eb == ec, 1.0, 0.0).astype(BF16)
        tile_idx = [jnp.clip(qi - kt, 0, 2) for kt in range(length // tq)]
        for kv in range(NSA_KV):
            k = ks_ref[0, 0:length, kvcols(kv)].astype(BF16)
            v = vs_ref[0, 0:length, kvcols(kv)].astype(BF16)
            picked = _mm(selt_ref[0, kv], onehot, TN)
            allowed = jnp.where(kpos <= qpos, picked, 0.0) > 0.5
            outs = attend(kv, k, v, allowed, tile_idx)
            for g in range(NSA_G):
                os_ref[kv * NSA_G + g] = outs[g]

    for i in range(t // kbs):
        pl.when((qi * tq) // kbs == i)(functools.partial(sel_branch, (i + 1) * kbs))

    lw = WIN_TILES * tq
    st = jnp.clip(qi - (WIN_TILES - 1), 0, t // tq - WIN_TILES)
    off = pl.multiple_of(st * tq, tq)
    dist = (qi - st) * tq + (lax.broadcasted_iota(jnp.int32, (tq, lw), 0)
                             - lax.broadcasted_iota(jnp.int32, (tq, lw), 1))
    win_allowed = jnp.where(dist >= 0, jnp.where(dist < NSA_WINDOW, 1.0, 0.0), 0.0) > 0.5
    win_tiles = [jnp.clip(qi - st - j, 0, 2) for j in range(WIN_TILES)]
    for kv in range(NSA_KV):
        k = kw_ref[0, pl.ds(off, lw), kvcols(kv)].astype(BF16)
        v = vw_ref[0, pl.ds(off, lw), kvcols(kv)].astype(BF16)
        o_win = attend(kv, k, v, win_allowed, win_tiles)
        for g in range(NSA_G):
            h = kv * NSA_G + g
            gate = [jax.nn.sigmoid(sm[:, SM_GATE + br * NSA_HEADS + h:SM_GATE + br * NSA_HEADS + h + 1])
                    for br in range(3)]
            cols = slice(h * NSA_DH, (h + 1) * NSA_DH)
            o_ref[0, :, cols] = gate[0] * oc_ref[0, :, cols] + gate[1] * os_ref[h] + gate[2] * o_win[g]


def _selwin(proj, selt, bias_tiles, o_cmp):
    b, t, _ = proj.shape
    tq = TQ
    nb = selt.shape[2]
    assert t >= WIN_TILES * tq
    kbs = math.gcd(t, 4 * tq)
    pair = NSA_KV * NSA_DH
    kvspec = lambda c6: pl.BlockSpec((1, t, pair), lambda i, j: (i, 0, C_KVN // pair + c6))
    return pl.pallas_call(
        functools.partial(_selwin_kernel, kbs=kbs),
        grid=(b, t // tq),
        in_specs=[pl.BlockSpec((1, tq, 1024), lambda i, j: (i, j, C_QN // 1024)),
                  kvspec(2), kvspec(3), kvspec(4), kvspec(5),
                  pl.BlockSpec((1, NSA_KV, nb, tq), lambda i, j: (i, 0, 0, j)),
                  pl.BlockSpec((NSA_HEADS, 3, tq, tq), lambda i, j: (0, 0, 0, 0)),
                  pl.BlockSpec((1, tq, 1024), lambda i, j: (i, j, 0)),
                  pl.BlockSpec((1, tq, LANE), lambda i, j: (i, j, C_SM // LANE))],
        out_specs=pl.BlockSpec((1, tq, NSA_HEADS * NSA_DH), lambda i, j: (i, j, 0)),
        out_shape=jax.ShapeDtypeStruct((b, t, NSA_HEADS * NSA_DH), F32),
        scratch_shapes=[pltpu.VMEM((NSA_HEADS, tq, NSA_DH), F32)],
        compiler_params=_cparams(("parallel", "arbitrary")),
        name="selwin",
    )(proj, proj, proj, proj, proj, selt, bias_tiles, o_cmp, proj)


def _nsa_sample_kernel(pt_ref, q_ref, kvn_ref, gt_ref, win_ref, bsel_ref, bwin_ref, bcmp_ref, bself_ref,
                       *rest, n_pages, n_top):
    del pt_ref
    sel_pages = rest[:n_pages]
    kvc_pages = rest[n_pages:2 * n_pages]
    o_ref, wout_ref, kvc_s = rest[2 * n_pages:2 * n_pages + 3]
    n = n_pages * (PAGE_SIZE // NSA_BLK)
    past = n_pages * PAGE_SIZE
    wb = win_ref.shape[0] // ROWS_PER_TOK
    R = SUBLANE
    tok_rows = lambda ref, which, kv, ntok: ref[pl.ds(which * NSA_KV + kv, ntok, stride=ROWS_PER_TOK), :]

    wout_ref[0:(wb - 1) * ROWS_PER_TOK, :] = win_ref[ROWS_PER_TOK:wb * ROWS_PER_TOK, :]
    wout_ref[(wb - 1) * ROWS_PER_TOK:wb * ROWS_PER_TOK, :] = kvn_ref[0, 4 * NSA_KV:6 * NSA_KV, :]

    q8 = q_ref[0]
    rowkv = lax.broadcasted_iota(jnp.int32, (R, 1), 0) // NSA_G
    bself = bself_ref[:, 0:1]

    for i in range(n_pages):
        kvc_s[2 * i:2 * i + 2, :] = kvc_pages[i][0]
    kvc_s[n:LANE, :] = jnp.zeros((LANE - n, KV_ROW), F32)
    kvc = kvc_s[...]
    lane = lax.broadcasted_iota(jnp.int32, (R, LANE), 1)
    ii = lax.broadcasted_iota(jnp.int32, (LANE, LANE), 0)
    jj = lax.broadcasted_iota(jnp.int32, (LANE, LANE), 1)
    jrow = lax.broadcasted_iota(jnp.int32, (1, LANE), 1)
    eb = lax.broadcasted_iota(jnp.int32, (LANE, past), 0)
    ec = lax.broadcasted_iota(jnp.int32, (LANE, past), 1) // NSA_BLK
    expand = jnp.where(eb == ec, 1.0, 0.0).astype(BF16)
    wr = lax.broadcasted_iota(jnp.int32, (R, wb), 1)
    win_valid = (wb - wr) < NSA_WINDOW
    KV = range(NSA_KV)

    kc = [kvc[:, kv * NSA_DH:(kv + 1) * NSA_DH] for kv in KV]
    vc = [kvc[:, (NSA_KV + kv) * NSA_DH:(NSA_KV + kv + 1) * NSA_DH] for kv in KV]
    lg = [_mm3(q8, kc[kv], NT) * NSA_SCALE + bcmp_ref[...] for kv in KV]
    lg = [jnp.where(lane < n, x, NEG_INF) for x in lg]
    e = [jnp.exp(x - jnp.max(x, axis=-1, keepdims=True)) for x in lg]
    p = [jnp.where(lane < n, x / jnp.sum(x, axis=-1, keepdims=True), 0.0) for x in e]
    o_cmp = [_mm(p[kv], vc[kv]) for kv in KV]
    imp = [jnp.sum(jnp.where(rowkv == kv, p[kv], 0.0), axis=0, keepdims=True) for kv in KV]
    forced = jnp.where(jrow == 0, 1.0, 0.0) + jnp.where(jrow == n, 1.0, 0.0) + jnp.where(jrow == n - 1, 1.0, 0.0)
    sel8 = []
    for kv in KV:
        score = jnp.where(jrow <= n, jnp.where(forced > 0.5, FORCE_SCORE, imp[kv]), -2.0)
        s_b = jnp.broadcast_to(score, (LANE, LANE))
        s_t = jnp.transpose(s_b)
        ahead = jnp.where(s_t > s_b, 1.0, jnp.where(s_t == s_b, jnp.where(ii < jj, 1.0, 0.0), 0.0))
        rank = jnp.sum(ahead, axis=0, keepdims=True)
        sel_row = jnp.where(rank < n_top, jnp.where(jrow <= n, 1.0, 0.0), 0.0)
        sel8.append(jnp.broadcast_to(sel_row, (R, LANE)))

    lg = [jnp.concatenate([_mm(q8, tok_rows(sel_pages[i], 0, kv, PAGE_SIZE), NT) for i in range(n_pages)],
                          axis=1) * NSA_SCALE + bsel_ref[...] for kv in KV]
    picked = [jnp.dot(sel8[kv].astype(BF16), expand, preferred_element_type=F32) for kv in KV]
    lg = [jnp.where(picked[kv] > 0.5, lg[kv], NEG_INF) for kv in KV]
    k_new = [kvn_ref[0, 2 * NSA_KV + kv:2 * NSA_KV + kv + 1, :] for kv in KV]
    v_new = [kvn_ref[0, 3 * NSA_KV + kv:3 * NSA_KV + kv + 1, :] for kv in KV]
    l_new = [jnp.sum(q8 * k_new[kv], axis=-1, keepdims=True) * NSA_SCALE + bself for kv in KV]
    l_new = [jnp.where(sel8[kv][:, n:n + 1] > 0.5, l_new[kv], NEG_INF) for kv in KV]
    m = [jnp.maximum(jnp.max(lg[kv], axis=-1, keepdims=True), l_new[kv]) for kv in KV]
    e = [jnp.exp(lg[kv] - m[kv]) for kv in KV]
    e_new = [jnp.exp(l_new[kv] - m[kv]) for kv in KV]
    den = [jnp.sum(e[kv], axis=-1, keepdims=True) + e_new[kv] for kv in KV]
    o_sel = []
    for kv in KV:
        pv = e_new[kv] * v_new[kv]
        for i in range(n_pages):
            pv = pv + _mm(e[kv][:, i * PAGE_SIZE:(i + 1) * PAGE_SIZE], tok_rows(sel_pages[i], 1, kv, PAGE_SIZE))
        o_sel.append(pv / den[kv])

    lg = [_mm(q8, tok_rows(win_ref, 0, kv, wb), NT) * NSA_SCALE + bwin_ref[...] for kv in KV]
    lg = [jnp.where(win_valid, x, NEG_INF) for x in lg]
    k_new = [kvn_ref[0, 4 * NSA_KV + kv:4 * NSA_KV + kv + 1, :] for kv in KV]
    v_new = [kvn_ref[0, 5 * NSA_KV + kv:5 * NSA_KV + kv + 1, :] for kv in KV]
    l_new = [jnp.sum(q8 * k_new[kv], axis=-1, keepdims=True) * NSA_SCALE + bself for kv in KV]
    m = [jnp.maximum(jnp.max(lg[kv], axis=-1, keepdims=True), l_new[kv]) for kv in KV]
    e = [jnp.exp(lg[kv] - m[kv]) for kv in KV]
    e_new = [jnp.exp(l_new[kv] - m[kv]) for kv in KV]
    den = [jnp.sum(e[kv], axis=-1, keepdims=True) + e_new[kv] for kv in KV]
    o_win = [(e_new[kv] * v_new[kv] + _mm(e[kv], tok_rows(win_ref, 1, kv, wb))) / den[kv] for kv in KV]

    gate = jax.nn.sigmoid(gt_ref[0])
    pick = lambda pair: jnp.where(rowkv == 0, pair[0], pair[1])
    o_ref[0] = gate[:, 0:1] * pick(o_cmp) + gate[:, 1:2] * pick(o_sel) + gate[:, 2:3] * pick(o_win)


def _nsa_sample(page_table, q8, kvn, gates, win_rows, sel_rows, kvc_pool, layer, n_pool, wb,
                bsel, bwin, bcmp, bself):
    db, n_pages = page_table.shape
    past = n_pages * PAGE_SIZE
    n_blocks = past // NSA_BLK + 1
    kern = functools.partial(_nsa_sample_kernel, n_pages=n_pages, n_top=min(NSA_TOPN, n_blocks))
    fixed = lambda shape: pl.BlockSpec(shape, lambda i, pt: (0,) * len(shape))
    wrows = wb * ROWS_PER_TOK
    prows = PAGE_SIZE * ROWS_PER_TOK
    in_specs = [pl.BlockSpec((1, NSA_HEADS, NSA_DH), lambda i, pt: (i, 0, 0)),
                pl.BlockSpec((1, 6 * NSA_KV, NSA_DH), lambda i, pt: (i, 0, 0)),
                pl.BlockSpec((1, NSA_HEADS, 3), lambda i, pt: (i, 0, 0)),
                pl.BlockSpec((wrows, NSA_DH), lambda i, pt: (layer * db + i, 0)),
                fixed((NSA_HEADS, past)), fixed((NSA_HEADS, wb)), fixed((NSA_HEADS, LANE)),
                fixed((NSA_HEADS, LANE))]
    for p in range(n_pages):
        in_specs.append(pl.BlockSpec((prows, NSA_DH), lambda i, pt, p=p: (layer * n_pool + pt[i, p], 0)))
    for p in range(n_pages):
        in_specs.append(pl.BlockSpec((1, PAGE_SIZE // NSA_BLK, KV_ROW), lambda i, pt, p=p: (pt[i, p], 0, 0)))
    return pl.pallas_call(
        kern,
        grid_spec=pltpu.PrefetchScalarGridSpec(
            num_scalar_prefetch=1, grid=(db,), in_specs=in_specs,
            out_specs=[pl.BlockSpec((1, NSA_HEADS, NSA_DH), lambda i, pt: (i, 0, 0)),
                       pl.BlockSpec((wrows, NSA_DH), lambda i, pt: (i, 0))],
            scratch_shapes=[pltpu.VMEM((LANE, KV_ROW), F32)]),
        out_shape=[jax.ShapeDtypeStruct((db, NSA_HEADS, NSA_DH), F32),
                   jax.ShapeDtypeStruct((db * wrows, NSA_DH), F32)],
        compiler_params=_cparams(("arbitrary",)),
        name="nsa_sample",
    )(page_table, q8, kvn, gates, win_rows, bsel, bwin, bcmp, bself,
      *([sel_rows] * n_pages), *([kvc_pool] * n_pages))


def _mix_kernel(oa_ref, ob_ref, ga_ref, gb_ref, x_ref, gt_ref, g_ref, woa_ref, wob_ref, wout_ref, o_ref):
    a = _mm(oa_ref[0], woa_ref[...])
    b = _mm(ob_ref[0], wob_ref[...])
    m = jax.nn.sigmoid(ga_ref[0]) * a + jax.nn.sigmoid(gb_ref[0]) * b
    mixed = _mm(m, wout_ref[...])
    o_ref[0] = x_ref[0] + gt_ref[0] * (_rms_rows(mixed) * g_ref[...])


def _mix(o_a, o_b, proj, x3, gt, g, w_oa, w_ob, w_out):
    bb, t, d = x3.shape
    tm = _pick(t, 512)
    row = lambda c: pl.BlockSpec((1, tm, d), lambda b, i: (b, i, c))
    wspec = pl.BlockSpec((d, d), lambda b, i: (0, 0))
    gtspec = (pl.BlockSpec((1, 1, d), lambda b, i: (b, 0, 0)) if gt.shape[1] == 1
              else pl.BlockSpec((1, tm, d), lambda b, i: (b, i, 0)))
    return pl.pallas_call(
        _mix_kernel,
        grid=(bb, t // tm),
        in_specs=[row(0), row(0), row(C_GM // d), row(C_GM // d + 1), row(0), gtspec,
                  pl.BlockSpec((1, d), lambda b, i: (0, 0)), wspec, wspec, wspec],
        out_specs=row(0),
        out_shape=jax.ShapeDtypeStruct((bb, t, d), F32),
        compiler_params=_cparams(("parallel", "parallel")),
        name="mix",
    )(o_a, o_b, proj, proj, x3, gt, g.reshape(1, d), w_oa, w_ob, w_out)


def _ffn_kernel(x_ref, g2_ref, sc_ref, sh_ref, gt_ref, g3_ref, wg_ref, wu_ref, wd_ref, o_ref, h_ref, acc_ref):
    f = pl.program_id(2)

    @pl.when(f == 0)
    def _():
        y = _rms_rows(x_ref[0]) * g2_ref[...]
        h_ref[...] = (y * (1.0 + sc_ref[0]) + sh_ref[0]).astype(BF16)
        acc_ref[...] = jnp.zeros(acc_ref.shape, F32)

    h = h_ref[...]
    gate = jnp.dot(h, wg_ref[...], preferred_element_type=F32)
    up = jnp.dot(h, wu_ref[...], preferred_element_type=F32)
    acc_ref[...] += _mm(_silu(gate) * up, wd_ref[...])

    @pl.when(f == pl.num_programs(2) - 1)
    def _():
        o_ref[0] = x_ref[0] + gt_ref[0] * (_rms_rows(acc_ref[...]) * g3_ref[...])


def _ffn(x3, g2, sc, sh, gt, g3, w_gu, w_down):
    bb, t, d = x3.shape
    tm = _pick(t, 512)
    nf = 2
    tf = D_FF // nf
    assert tf % LANE == 0
    return pl.pallas_call(
        _ffn_kernel,
        grid=(bb, t // tm, nf),
        in_specs=[pl.BlockSpec((1, tm, d), lambda b, i, f: (b, i, 0)),
                  pl.BlockSpec((1, d), lambda b, i, f: (0, 0)),
                  _mod_spec(sc, tm), _mod_spec(sh, tm), _mod_spec(gt, tm),
                  pl.BlockSpec((1, d), lambda b, i, f: (0, 0)),
                  pl.BlockSpec((d, tf), lambda b, i, f: (0, f)),
                  pl.BlockSpec((d, tf), lambda b, i, f: (0, nf + f)),
                  pl.BlockSpec((tf, d), lambda b, i, f: (f, 0))],
        out_specs=pl.BlockSpec((1, tm, d), lambda b, i, f: (b, i, 0)),
        out_shape=jax.ShapeDtypeStruct((bb, t, d), F32),
        scratch_shapes=[pltpu.VMEM((tm, d), BF16), pltpu.VMEM((tm, d), F32)],
        compiler_params=_cparams(("parallel", "parallel", "arbitrary")),
        name="ffn",
    )(x3, g2.reshape(1, d), sc, sh, gt, g3.reshape(1, d), w_gu, w_gu, w_down)


def _bucket_np(n):
    n = np.maximum(n, 0)
    exact = NUM_BUCKETS // 2
    big = exact + (np.log(np.maximum(n, 1).astype(np.float32) / np.float32(exact))
                   / np.float32(math.log(MAX_DISTANCE / exact))
                   * np.float32(NUM_BUCKETS - exact)).astype(np.int32)
    return np.where(n < exact, n, np.minimum(big, NUM_BUCKETS - 1)).astype(np.int32)


def _bias_tables(rel_bias, t, past, wb):
    def table(dist):
        return jnp.moveaxis(jnp.take(rel_bias, jnp.asarray(_bucket_np(dist)), axis=0), -1, 0)

    n = t // NSA_BLK
    ends = np.arange(n) * NSA_BLK + (NSA_BLK - 1)
    cmp_p = table(np.arange(t)[None, :] - ends[:, None])
    ar = np.arange(TQ)
    tiles = table(np.stack([d * TQ + ar[:, None] - ar[None, :] for d in range(3)]))
    sel_s = table(past - np.arange(past))
    win_s = table(wb - np.arange(wb))
    n_s = past // NSA_BLK
    cmp_s = table(np.pad(past - (np.arange(n_s) * NSA_BLK + NSA_BLK - 1), (0, LANE - n_s)))
    self_s = table(np.zeros((LANE,), np.int64))
    return cmp_p, tiles, sel_s, win_s, cmp_s, self_s


def _regroup_w_in(w):
    o = np.cumsum((0, CONV_DIM, DN_HEADS * DN_DV, DN_HEADS, DN_HEADS, NSA_HEADS * NSA_DH,
                   6 * NSA_KV * NSA_DH, 3 * NSA_HEADS, 2 * D_MODEL))
    qkv_z, b, a, qn, kvn, gn, gm = (w[:, o[0]:o[2]], w[:, o[2]:o[3]], w[:, o[3]:o[4]], w[:, o[4]:o[5]],
                                    w[:, o[5]:o[6]], w[:, o[6]:o[7]], w[:, o[7]:o[8]])
    pad = jnp.zeros((w.shape[0], N_PROJ - C_SM - 5 * NSA_HEADS), w.dtype)
    return jnp.concatenate([qkv_z, qn, gm, kvn, b, a, gn, pad], axis=1).astype(BF16)


def _layer_params(l, w_in, conv_w, a_log, dt_bias, onorm_g, w_oa, w_ob, w_out, cmp_pe, cmp_w1, cmp_w2,
                  w_gu, w_down):
    hp = jnp.zeros((SUBLANE, LANE), F32)
    hp = hp.at[0, SM_A:SM_A + DN_HEADS].set(a_log[l]).at[1, SM_A:SM_A + DN_HEADS].set(dt_bias[l])
    return dict(
        w_in=_regroup_w_in(w_in[l]), cw=conv_w[l], hp=hp, on=onorm_g[l].reshape(1, DN_DV),
        w_oa=w_oa[l].astype(BF16), w_ob=w_ob[l].astype(BF16), w_out=w_out[l].astype(BF16),
        pe=jnp.reshape(cmp_pe[l], (1, NSA_BLK * 2 * NSA_DH)),
        w1=jnp.transpose(cmp_w1[l], (1, 0, 2, 3)).reshape(NSA_BLK * 2 * NSA_DH, NSA_CMP_HID).astype(BF16),
        w2=cmp_w2[l].reshape(2 * NSA_CMP_HID, NSA_DH).astype(BF16),
        w_gu=w_gu[l].astype(BF16), w_down=w_down[l].astype(BF16))


def _mods(mod, per_token):
    parts = [mod[:, i * D_MODEL:(i + 1) * D_MODEL] for i in range(6)]
    return [p[None, :, :] if per_token else p[:, None, :] for p in parts]


def _tail(x3, o_a, o_b, proj, mods, ng, P):
    sh1, sc1, gt1, sh2, sc2, gt2 = mods
    x1 = _mix(o_a, o_b, proj, x3, gt1, ng[1], P['w_oa'], P['w_ob'], P['w_out'])
    return _ffn(x1, ng[2], sc2, sh2, gt2, ng[3], P['w_gu'], P['w_down'])


def _kv_rows(proj, branch):
    b, t, _ = proj.shape
    c0 = C_KVN + branch * KV_ROW
    return proj[:, :, c0:c0 + KV_ROW].reshape(b, t, 2, NSA_KV, NSA_DH)


def _prompt_layer(x3, mods, ng, P, tabs):
    b, t, _ = x3.shape
    proj = _proj(x3, ng[0], mods[1], mods[0], P['w_in'])
    o_a, s_new = _dn_prompt(proj, P['cw'], P['hp'], P['on'])
    cmp_rows, sel_rows, win_rows = _kv_rows(proj, 0), _kv_rows(proj, 1), _kv_rows(proj, 2)
    n = t // NSA_BLK
    blocks = cmp_rows.reshape(b * t * ROWS_PER_TOK, NSA_DH)
    kvc = _compress(blocks, 0, b * n, P['pe'], P['w1'], P['w2']).reshape(b, n, KV_ROW)
    o_cmp, selt = _cmp_select(proj, kvc, tabs[0])
    o_b = _selwin(proj, selt, tabs[1], o_cmp)
    y = _tail(x3, o_a, o_b, proj, mods, ng, P)
    conv_new = proj[:, t - (CONV_W - 1):, C_QKV:C_QKV + CONV_DIM]
    return y, (cmp_rows, sel_rows, win_rows[:, t - min(NSA_WINDOW, t):], s_new, conv_new)


def _sample_layer(x3, mods, ng, P, tabs, layer, page_table, cmp_blocks, sel_rows, win_rows, n_pool, wb,
                  state_all, conv_all):
    db = x3.shape[1]
    proj = _proj(x3, ng[0], mods[1], mods[0], P['w_in'])
    rows = proj.reshape(db, 1, N_PROJ)
    o_a, s_new = _dn_sample(rows, conv_all, state_all, layer, P['cw'], P['hp'], P['on'])
    halves = PAGE_SIZE // NSA_BLK
    kvc_pool = _compress(cmp_blocks, layer * n_pool * halves, n_pool * halves, P['pe'], P['w1'], P['w2'])
    q8 = proj[0, :, C_QN:C_QN + NSA_HEADS * NSA_DH].reshape(db, NSA_HEADS, NSA_DH)
    kvn = proj[0, :, C_KVN:C_KVN + 6 * NSA_KV * NSA_DH].reshape(db, 6 * NSA_KV, NSA_DH)
    gates = jnp.transpose(proj[0, :, C_SM + SM_GATE:C_SM + SM_GATE + 3 * NSA_HEADS].reshape(db, 3, NSA_HEADS),
                          (0, 2, 1))
    o_b, win_new = _nsa_sample(page_table, q8, kvn, gates, win_rows, sel_rows,
                               kvc_pool.reshape(n_pool, halves, KV_ROW), layer, n_pool, wb,
                               tabs[2], tabs[3], tabs[4], tabs[5])
    y = _tail(x3, o_a.reshape(1, db, D_MODEL), o_b.reshape(1, db, D_MODEL), proj, mods, ng, P)
    qkv_new = proj[0, :, C_QKV:C_QKV + CONV_DIM]
    conv_new = jnp.concatenate([conv_all[layer][:, 1:], qkv_new[:, None, :]], axis=1)
    kv_new = kvn.reshape(db, 1, 6, NSA_KV, NSA_DH)
    return y, (kv_new[:, :, 0:2], kv_new[:, :, 2:4], win_new.reshape(db, wb, 2, NSA_KV, NSA_DH), s_new, conv_new)


def kernel(x_prompt, x_sample, cache_cmp_kv, cache_sel_kv, cache_win_kv, state_delta, state_conv, page_table, c_prompt, c_sample, w_ada, b_ada, norm_g, w_in, conv_w, a_log, dt_bias, onorm_g, w_oa, w_ob, w_out, cmp_pe, cmp_w1, cmp_w2, rel_bias, w_gu, w_down):
    depth = w_in.shape[0]
    b, t, d = x_prompt.shape
    db, ts, _ = x_sample.shape
    assert ts == 1 and d == D_MODEL and t % TQ == 0
    n_pool, n_pages = cache_cmp_kv.shape[1], page_table.shape[1]
    assert cache_cmp_kv.shape[2] == PAGE_SIZE
    past = n_pages * PAGE_SIZE
    wb = cache_win_kv.shape[2]
    tabs = _bias_tables(rel_bias, t, past, wb)
    cmp_blocks = cache_cmp_kv.reshape(depth * n_pool * PAGE_SIZE * ROWS_PER_TOK, NSA_DH)
    sel_rows = cache_sel_kv.reshape(depth * n_pool * PAGE_SIZE * ROWS_PER_TOK, NSA_DH)
    win_rows = cache_win_kv.reshape(depth * db * wb * ROWS_PER_TOK, NSA_DH)
    c_all = jnp.concatenate([c_prompt, c_sample], axis=0)
    pad_rows = (-c_all.shape[0]) % SUBLANE
    c_all = jnp.pad(c_all, ((0, pad_rows), (0, 0)))
    yp = x_prompt
    ys = x_sample.reshape(1, db, d)
    p_st, s_st = [], []
    for l in range(depth):
        P = _layer_params(l, w_in, conv_w, a_log, dt_bias, onorm_g, w_oa, w_ob, w_out, cmp_pe, cmp_w1,
                          cmp_w2, w_gu, w_down)
        mod = _ada(c_all, w_ada[l], b_ada[l])
        yp, st = _prompt_layer(yp, _mods(mod[:b], False), norm_g[l], P, tabs)
        p_st.append(st)
        ys, st = _sample_layer(ys, _mods(mod[b:b + db], True), norm_g[l], P, tabs, l, page_table,
                               cmp_blocks, sel_rows, win_rows, n_pool, wb, state_delta, state_conv)
        s_st.append(st)
    stk = lambda states, i: jnp.stack([s[i] for s in states])
    return (yp, ys.reshape(db, 1, d),
            stk(p_st, 0), stk(p_st, 1), stk(p_st, 2), stk(p_st, 3), stk(p_st, 4),
            stk(s_st, 0), stk(s_st, 1), stk(s_st, 2), stk(s_st, 3), stk(s_st, 4))
```

```python
import functools
import math

import numpy as np
import jax
import jax.numpy as jnp
from jax import lax
from jax.experimental import pallas as pl
from jax.experimental.pallas import tpu as pltpu

F32 = jnp.float32
BF16 = jnp.bfloat16

D_MODEL = 1024
EPS = 1e-6
DN_HEADS = 8
DN_DK = 128
DN_DV = 128
CONV_W = 4
CONV_DIM = 2 * DN_HEADS * DN_DK + DN_HEADS * DN_DV
DN_CHUNK = 64
NSA_HEADS = 8
NSA_KV = 2
NSA_G = NSA_HEADS // NSA_KV
NSA_DH = 128
NSA_SCALE = NSA_DH ** -0.5
NSA_BLK = 64
NSA_TOPN = 16
NSA_WINDOW = 512
NSA_CMP_HID = 128
FORCE_SCORE = 1e4
NEG_INF = -1e30
NUM_BUCKETS = 32
MAX_DISTANCE = 128
D_FF = 256 * (-(-8 * D_MODEL // (3 * 256)))
PAGE_SIZE = 128

C_QKV = 0
C_Z = 3072
C_QN = 4096
C_GM = 5120
C_KVN = 7168
C_SM = 8704
N_PROJ = 8832
SM_BETA = 0
SM_A = 8
SM_GATE = 16
KV_ROW = 2 * NSA_KV * NSA_DH
ROWS_PER_TOK = 2 * NSA_KV

LANE = 128
SUBLANE = 8
VMEM_LIMIT = 48 * 1024 * 1024
TQ = 128
WIN_TILES = NSA_WINDOW // TQ + 1


def _cparams(sem):
    return pltpu.CompilerParams(dimension_semantics=sem, vmem_limit_bytes=VMEM_LIMIT)


def _pick(n, pref, mult=SUBLANE):
    for t in range(min(pref, n), 0, -1):
        if n % t == 0 and t % mult == 0:
            return t
    return n


def _mm(a, b, dims=((1,), (0,))):
    return lax.dot_general(a.astype(BF16), b.astype(BF16), (dims, ((), ())),
                           preferred_element_type=F32)


NN = ((1,), (0,))
NT = ((1,), (1,))
TN = ((0,), (0,))


def _hilo(a):
    hi = a.astype(BF16)
    lo = (a - hi.astype(F32)).astype(BF16)
    return hi, lo


def _mm3(a, b, dims=NN):
    ah, al = _hilo(a)
    bh, bl = _hilo(b)
    return _mm(ah, bh, dims) + (_mm(ah, bl, dims) + _mm(al, bh, dims))


def _mm_exact_lhs(a01, b):
    b0 = b.astype(BF16)
    r1 = b - b0.astype(F32)
    b1 = r1.astype(BF16)
    b2 = (r1 - b1.astype(F32)).astype(BF16)
    return _mm(a01, b0) + (_mm(a01, b1) + _mm(a01, b2))


def _silu(x):
    return x * jax.nn.sigmoid(x)


def _softplus(x):
    return jnp.maximum(x, 0.0) + jnp.log1p(jnp.exp(-jnp.abs(x)))


def _rms_rows(x):
    return x * lax.rsqrt(jnp.mean(x * x, axis=-1, keepdims=True) + EPS)


def _ada_kernel(c_ref, w_ref, b_ref, o_ref):
    o_ref[...] = _mm(_silu(c_ref[...]), w_ref[...]) + b_ref[...]


def _ada(c_all, w, b):
    m, d = c_all.shape
    n = w.shape[1]
    tn = _pick(n, 1536, LANE)
    return pl.pallas_call(
        _ada_kernel,
        grid=(n // tn,),
        in_specs=[pl.BlockSpec((m, d), lambda j: (0, 0)),
                  pl.BlockSpec((d, tn), lambda j: (0, j)),
                  pl.BlockSpec((1, tn), lambda j: (0, j))],
        out_specs=pl.BlockSpec((m, tn), lambda j: (0, j)),
        out_shape=jax.ShapeDtypeStruct((m, n), F32),
        compiler_params=_cparams(("parallel",)),
        name="ada",
    )(c_all, w, b.reshape(1, n))


def _modulated_norm(x, g, sc, sh):
    return ((_rms_rows(x) * g) * (1.0 + sc) + sh).astype(BF16)


def _proj_kernel(x_ref, g_ref, sc_ref, sh_ref, w_ref, o_ref):
    h = _modulated_norm(x_ref[0], g_ref[...], sc_ref[0], sh_ref[0])
    o_ref[0] = jnp.dot(h, w_ref[...], preferred_element_type=F32)


def _mod_spec(mod, tm):
    if mod.shape[1] == 1:
        return pl.BlockSpec((1, 1, D_MODEL), lambda b, i: (b, 0, 0))
    return pl.BlockSpec((1, tm, D_MODEL), lambda b, i: (b, i, 0))


def _proj(x3, g, sc, sh, w):
    bb, t, d = x3.shape
    n = w.shape[1]
    tm = _pick(t, 512)
    tn = _pick(n, 2944, LANE)
    rows = lambda spec: pl.BlockSpec(spec.block_shape, lambda j, b, i: spec.index_map(b, i))
    return pl.pallas_call(
        _proj_kernel,
        grid=(n // tn, bb, t // tm),
        in_specs=[pl.BlockSpec((1, tm, d), lambda j, b, i: (b, i, 0)),
                  pl.BlockSpec((1, d), lambda j, b, i: (0, 0)),
                  rows(_mod_spec(sc, tm)), rows(_mod_spec(sh, tm)),
                  pl.BlockSpec((d, tn), lambda j, b, i: (0, j))],
        out_specs=pl.BlockSpec((1, tm, tn), lambda j, b, i: (b, i, j)),
        out_shape=jax.ShapeDtypeStruct((bb, t, n), F32),
        compiler_params=_cparams(("parallel", "parallel", "parallel")),
        name="proj",
    )(x3, g.reshape(1, d), sc, sh, w)


def _dn_prompt_kernel(qkv_ref, z_ref, sm_ref, cw_ref, hp_ref, on_ref, o_ref, s_ref, buf_ref, st_ref):
    c = pl.program_id(1)
    C = DN_CHUNK
    hist = SUBLANE

    @pl.when(c == 0)
    def _():
        buf_ref[0:hist, :] = jnp.zeros((hist, CONV_DIM), F32)
        st_ref[...] = jnp.zeros(st_ref.shape, F32)

    buf_ref[hist:hist + C, :] = qkv_ref[0]

    def conv_act(c0):
        acc = None
        for i in range(CONV_W):
            r0 = hist - (CONV_W - 1) + i
            term = buf_ref[r0:r0 + C, c0:c0 + LANE] * cw_ref[i:i + 1, c0:c0 + LANE]
            acc = term if acc is None else acc + term
        return _silu(acc)

    sm = sm_ref[0]
    beta_all = jax.nn.sigmoid(sm)
    g_all = -jnp.exp(hp_ref[0:1, :]) * _softplus(sm + hp_ref[1:2, :])
    ri = lax.broadcasted_iota(jnp.int32, (C, C), 0)
    ci = lax.broadcasted_iota(jnp.int32, (C, C), 1)
    incl = ri >= ci
    strict = ri > ci
    tri = jnp.where(incl, 1.0, 0.0).astype(BF16)
    eye = jnp.where(ri == ci, 1.0, 0.0)
    gc_all = _mm_exact_lhs(tri, g_all)
    gc_t = jnp.transpose(jnp.concatenate([gc_all, jnp.zeros((LANE - C, LANE), F32)], axis=0))
    eg_all = jnp.exp(gc_all)

    H = range(DN_HEADS)
    q = [conv_act(h * DN_DK) for h in H]
    k = [conv_act(DN_HEADS * DN_DK + h * DN_DK) for h in H]
    v = [conv_act(2 * DN_HEADS * DN_DK + h * DN_DV) for h in H]
    q = [x * lax.rsqrt(jnp.sum(x * x, axis=-1, keepdims=True) + EPS) * (DN_DK ** -0.5) for x in q]
    k = [x * lax.rsqrt(jnp.sum(x * x, axis=-1, keepdims=True) + EPS) for x in k]
    beta = [beta_all[:, SM_BETA + h:SM_BETA + h + 1] for h in H]
    gc = [gc_all[:, SM_A + h:SM_A + h + 1] for h in H]
    eg = [eg_all[:, SM_A + h:SM_A + h + 1] for h in H]
    g_last = [gc_all[C - 1:C, SM_A + h:SM_A + h + 1] for h in H]
    decay = [jnp.exp(jnp.where(incl, gc[h] - gc_t[SM_A + h:SM_A + h + 1, 0:C], NEG_INF)) for h in H]
    kb = [k[h] * beta[h] for h in H]
    a = [jnp.where(strict, _mm(kb[h], k[h], NT) * decay[h], 0.0) for h in H]
    qk = [_mm(q[h], k[h], NT) * decay[h] for h in H]
    t = [eye - a[h] for h in H]
    p = [_mm(a[h], a[h]) for h in H]
    steps = int(math.log2(C)) - 1
    for s in range(steps):
        t = [t[h] + _mm(t[h], p[h]) for h in H]
        if s + 1 < steps:
            p = [_mm(p[h], p[h]) for h in H]
    resid = [eye - t[h] - _mm3(a[h], t[h]) for h in H]
    t = [t[h] + _mm(t[h], resid[h]) for h in H]
    u = [_mm(t[h], v[h] * beta[h]) for h in H]
    w = [_mm(t[h], kb[h] * eg[h]) for h in H]
    s_old = [st_ref[h] for h in H]
    v_new = [u[h] - _mm(w[h], s_old[h]) for h in H]
    o = [_mm(q[h] * eg[h], s_old[h]) + _mm(qk[h], v_new[h]) for h in H]
    for h in H:
        st_ref[h] = s_old[h] * jnp.exp(g_last[h]) + _mm(k[h] * jnp.exp(g_last[h] - gc[h]), v_new[h], TN)
    for h in H:
        z = z_ref[0, :, h * DN_DV:(h + 1) * DN_DV]
        o_ref[0, :, h * DN_DV:(h + 1) * DN_DV] = (_rms_rows(o[h]) * on_ref[...] * _silu(z)).astype(o_ref.dtype)

    buf_ref[0:hist, :] = buf_ref[C:C + hist, :]

    @pl.when(c == pl.num_programs(1) - 1)
    def _():
        s_ref[0] = st_ref[...]


def _dn_prompt(proj, cw, hp, on):
    b, t, _ = proj.shape
    C = DN_CHUNK
    assert t % C == 0
    return pl.pallas_call(
        _dn_prompt_kernel,
        grid=(b, t // C),
        in_specs=[pl.BlockSpec((1, C, CONV_DIM), lambda i, c: (i, c, C_QKV // CONV_DIM)),
                  pl.BlockSpec((1, C, 1024), lambda i, c: (i, c, C_Z // 1024)),
                  pl.BlockSpec((1, C, LANE), lambda i, c: (i, c, C_SM // LANE)),
                  pl.BlockSpec((CONV_W, CONV_DIM), lambda i, c: (0, 0)),
                  pl.BlockSpec((SUBLANE, LANE), lambda i, c: (0, 0)),
                  pl.BlockSpec((1, DN_DV), lambda i, c: (0, 0))],
        out_specs=[pl.BlockSpec((1, C, DN_HEADS * DN_DV), lambda i, c: (i, c, 0)),
                   pl.BlockSpec((1, DN_HEADS, DN_DK, DN_DV), lambda i, c: (i, 0, 0, 0))],
        out_shape=[jax.ShapeDtypeStruct((b, t, DN_HEADS * DN_DV), BF16),
                   jax.ShapeDtypeStruct((b, DN_HEADS, DN_DK, DN_DV), F32)],
        scratch_shapes=[pltpu.VMEM((C + SUBLANE, CONV_DIM), F32),
                        pltpu.VMEM((DN_HEADS, DN_DK, DN_DV), F32)],
        compiler_params=_cparams(("parallel", "arbitrary")),
        name="dn_prompt",
    )(proj, proj, proj, cw, hp, on)


def _dn_sample_kernel(qkv_ref, z_ref, sm_ref, cb_ref, cw_ref, hp_ref, on_ref, sin_ref, o_ref, sout_ref):
    xb = cb_ref[0, 0]
    conv = qkv_ref[0] * cw_ref[CONV_W - 1:CONV_W, :]
    for i in range(CONV_W - 1):
        conv = conv + xb[i:i + 1, :] * cw_ref[i:i + 1, :]
    act = _silu(conv)
    sm = sm_ref[0]
    beta_all = jax.nn.sigmoid(sm)
    eg_all = jnp.exp(-jnp.exp(hp_ref[0:1, :]) * _softplus(sm + hp_ref[1:2, :]))
    row = lax.broadcasted_iota(jnp.int32, (SUBLANE, DN_DK), 0)
    rows = lambda x: jnp.broadcast_to(x, row.shape)
    H = range(DN_HEADS)
    q = [act[:, h * DN_DK:(h + 1) * DN_DK] for h in H]
    k = [act[:, (DN_HEADS + h) * DN_DK:(DN_HEADS + h + 1) * DN_DK] for h in H]
    v = [act[:, 2 * DN_HEADS * DN_DK + h * DN_DV:2 * DN_HEADS * DN_DK + (h + 1) * DN_DV] for h in H]
    q = [x * lax.rsqrt(jnp.sum(x * x, axis=-1, keepdims=True) + EPS) * (DN_DK ** -0.5) for x in q]
    k = [x * lax.rsqrt(jnp.sum(x * x, axis=-1, keepdims=True) + EPS) for x in k]
    beta = [beta_all[:, SM_BETA + h:SM_BETA + h + 1] for h in H]
    eg = [eg_all[:, SM_A + h:SM_A + h + 1] for h in H]
    s_old = [sin_ref[0, 0, h] for h in H]
    kq = [jnp.where(row == 0, rows(k[h]), jnp.where(row == 1, rows(q[h]), 0.0)) for h in H]
    r = [_mm3(kq[h], s_old[h]) for h in H]
    v_new = [beta[h] * (v[h] - eg[h] * r[h][0:1, :]) for h in H]
    qk = [jnp.sum(q[h] * k[h], axis=-1, keepdims=True) for h in H]
    o = [eg[h] * r[h][1:2, :] + qk[h] * v_new[h] for h in H]
    outer = [_mm3(jnp.where(row == 0, rows(k[h]), 0.0), rows(v_new[h]), TN) for h in H]
    for h in H:
        sout_ref[0, h] = s_old[h] * eg[h] + outer[h]
    for h in H:
        z = z_ref[0, :, h * DN_DV:(h + 1) * DN_DV]
        o_ref[0, :, h * DN_DV:(h + 1) * DN_DV] = _rms_rows(o[h]) * on_ref[...] * _silu(z)


def _dn_sample(proj_rows, conv_all, state_all, layer, cw, hp, on):
    db = proj_rows.shape[0]
    return pl.pallas_call(
        _dn_sample_kernel,
        grid=(db,),
        in_specs=[pl.BlockSpec((1, 1, CONV_DIM), lambda i: (i, 0, C_QKV // CONV_DIM)),
                  pl.BlockSpec((1, 1, 1024), lambda i: (i, 0, C_Z // 1024)),
                  pl.BlockSpec((1, 1, LANE), lambda i: (i, 0, C_SM // LANE)),
                  pl.BlockSpec((1, 1, CONV_W - 1, CONV_DIM), lambda i: (layer, i, 0, 0)),
                  pl.BlockSpec((CONV_W, CONV_DIM), lambda i: (0, 0)),
                  pl.BlockSpec((SUBLANE, LANE), lambda i: (0, 0)),
                  pl.BlockSpec((1, DN_DV), lambda i: (0, 0)),
                  pl.BlockSpec((1, 1, DN_HEADS, DN_DK, DN_DV), lambda i: (layer, i, 0, 0, 0))],
        out_specs=[pl.BlockSpec((1, 1, DN_HEADS * DN_DV), lambda i: (i, 0, 0)),
                   pl.BlockSpec((1, DN_HEADS, DN_DK, DN_DV), lambda i: (i, 0, 0, 0))],
        out_shape=[jax.ShapeDtypeStruct((db, 1, DN_HEADS * DN_DV), F32),
                   jax.ShapeDtypeStruct((db, DN_HEADS, DN_DK, DN_DV), F32)],
        compiler_params=_cparams(("parallel",)),
        name="dn_sample",
    )(proj_rows, proj_rows, proj_rows, conv_all, cw, hp, on, state_all)


BLK_ROWS = NSA_BLK * ROWS_PER_TOK


def _compress_kernel(x_ref, pe_ref, w1_ref, w2_ref, o_ref):
    tb = x_ref.shape[0] // BLK_ROWS
    for ck in range(ROWS_PER_TOK):
        c = ck // NSA_KV
        acc = jnp.zeros((tb, NSA_CMP_HID), F32)
        for l in range(NSA_BLK):
            x = (x_ref[pl.ds(l * ROWS_PER_TOK + ck, tb, stride=BLK_ROWS), :]
                 + pe_ref[:, (l * 2 + c) * NSA_DH:(l * 2 + c + 1) * NSA_DH])
            acc = acc + _mm(x, w1_ref[(l * 2 + c) * NSA_DH:(l * 2 + c + 1) * NSA_DH, :])
        o_ref[:, ck * NSA_DH:(ck + 1) * NSA_DH] = _mm(
            _silu(acc), w2_ref[c * NSA_CMP_HID:(c + 1) * NSA_CMP_HID, :])


def _compress(x_rows, row0, r, pe_flat, w1_r, w2_r):
    tb = _pick(r, 64)
    assert row0 % tb == 0
    return pl.pallas_call(
        _compress_kernel,
        grid=(r // tb,),
        in_specs=[pl.BlockSpec((tb * BLK_ROWS, NSA_DH), lambda i: (row0 // tb + i, 0)),
                  pl.BlockSpec((1, NSA_BLK * 2 * NSA_DH), lambda i: (0, 0)),
                  pl.BlockSpec((NSA_BLK * 2 * NSA_DH, NSA_CMP_HID), lambda i: (0, 0)),
                  pl.BlockSpec((2 * NSA_CMP_HID, NSA_DH), lambda i: (0, 0))],
        out_specs=pl.BlockSpec((tb, KV_ROW), lambda i: (i, 0)),
        out_shape=jax.ShapeDtypeStruct((r, KV_ROW), F32),
        compiler_params=_cparams(("parallel",)),
        name="compress",
    )(x_rows, pe_flat, w1_r, w2_r)


def _cmp_select_kernel(q_ref, kvc_ref, bias_ref, o_ref, selt_ref, *, n_top):
    qi = pl.program_id(1)
    tq = q_ref.shape[1]
    n = kvc_ref.shape[1]
    qpos = qi * tq + lax.broadcasted_iota(jnp.int32, (n, tq), 1)
    jn = lax.broadcasted_iota(jnp.int32, (n, tq), 0)
    valid = qpos >= jn * NSA_BLK + (NSA_BLK - 1)
    validf = jnp.where(valid, 1.0, 0.0)
    cur = qpos // NSA_BLK
    forced = jnp.where(jn == 0, 1.0, 0.0) + jnp.where(jn == cur, 1.0, 0.0) + jnp.where(jn == cur - 1, 1.0, 0.0)
    kvc = kvc_ref[0]
    kc = [kvc[:, kv * NSA_DH:(kv + 1) * NSA_DH] for kv in range(NSA_KV)]
    vc = [kvc[:, (NSA_KV + kv) * NSA_DH:(NSA_KV + kv + 1) * NSA_DH] for kv in range(NSA_KV)]
    HH = range(NSA_HEADS)
    lg = [_mm3(kc[h // NSA_G], q_ref[0, :, h * NSA_DH:(h + 1) * NSA_DH], NT) * NSA_SCALE + bias_ref[h]
          for h in HH]
    lg = [jnp.where(valid, x, NEG_INF) for x in lg]
    e = [jnp.exp(x - jnp.max(x, axis=0, keepdims=True)) for x in lg]
    p = [x / jnp.sum(x, axis=0, keepdims=True) * validf for x in e]
    for h in HH:
        o_ref[0, :, h * NSA_DH:(h + 1) * NSA_DH] = _mm(p[h], vc[h // NSA_G], TN)
    for kv in range(NSA_KV):
        imp = p[kv * NSA_G]
        for g in range(1, NSA_G):
            imp = imp + p[kv * NSA_G + g]
        score = jnp.where(jn <= cur, jnp.where(forced > 0.5, FORCE_SCORE, imp), -1.0)
        rank = jnp.zeros((n, tq), F32)
        for i in range(n):
            si = score[i:i + 1, :]
            rank = rank + jnp.where(si > score, 1.0, jnp.where(si == score, jnp.where(jn > i, 1.0, 0.0), 0.0))
        selt_ref[0, kv] = jnp.where(rank < n_top, 1.0, 0.0)


def _cmp_select(proj, kvc, bias_cmp):
    b, t, _ = proj.shape
    n = kvc.shape[1]
    tq = TQ
    kern = functools.partial(_cmp_select_kernel, n_top=min(NSA_TOPN, n))
    return pl.pallas_call(
        kern,
        grid=(b, t // tq),
        in_specs=[pl.BlockSpec((1, tq, 1024), lambda i, j: (i, j, C_QN // 1024)),
                  pl.BlockSpec((1, n, KV_ROW), lambda i, j: (i, 0, 0)),
                  pl.BlockSpec((NSA_HEADS, n, tq), lambda i, j: (0, 0, j))],
        out_specs=[pl.BlockSpec((1, tq, NSA_HEADS * NSA_DH), lambda i, j: (i, j, 0)),
                   pl.BlockSpec((1, NSA_KV, n, tq), lambda i, j: (i, 0, 0, j))],
        out_shape=[jax.ShapeDtypeStruct((b, t, NSA_HEADS * NSA_DH), F32),
                   jax.ShapeDtypeStruct((b, NSA_KV, n, t), F32)],
        compiler_params=_cparams(("parallel", "parallel")),
        name="cmp_select",
    )(proj, kvc, bias_cmp)


def _selwin_kernel(q_ref, ks_ref, vs_ref, kw_ref, vw_ref, selt_ref, bt_ref, oc_ref, sm_ref, o_ref, os_ref,
                   *, kbs):
    qi = pl.program_id(1)
    tq = TQ
    t = ks_ref.shape[1]
    nb = selt_ref.shape[2]
    sm = sm_ref[0]
    kvcols = lambda kv: slice(kv * NSA_DH, (kv + 1) * NSA_DH)
    q4 = [jnp.concatenate([q_ref[0, :, (kv * NSA_G + g) * NSA_DH:(kv * NSA_G + g + 1) * NSA_DH]
                           for g in range(NSA_G)], axis=0).astype(BF16) for kv in range(NSA_KV)]

    def attend(kv, k, v, allowed, tile_idx):
        s = lax.dot_general(q4[kv], k, (NT, ((), ())), preferred_element_type=F32) * NSA_SCALE
        outs = []
        for g in range(NSA_G):
            h = kv * NSA_G + g
            bias = jnp.concatenate([bt_ref[h, ti] for ti in tile_idx], axis=1)
            sg = jnp.where(allowed, s[g * tq:(g + 1) * tq] + bias, NEG_INF)
            e = jnp.exp(sg - jnp.max(sg, axis=-1, keepdims=True))
            den = jnp.sum(e, axis=-1, keepdims=True)
            outs.append(jnp.dot(e.astype(BF16), v, preferred_element_type=F32) / den)
        return outs

    def sel_branch(length):
        qpos = qi * tq + lax.broadcasted_iota(jnp.int32, (tq, length), 0)
        kpos = lax.broadcasted_iota(jnp.int32, (tq, length), 1)
        eb = lax.broadcasted_iota(jnp.int32, (nb, length), 0)
        ec = lax.broadcasted_iota(jnp.int32, (nb, length), 1) // NSA_BLK
        onehot = jnp.where(eb == ec, 1.0, 0.0).astype(BF16)
        tile_idx = [jnp.clip(qi - kt, 0, 2) for kt in range(length // tq)]
        for kv in range(NSA_KV):
            k = ks_ref[0, 0:length, kvcols(kv)].astype(BF16)
            v = vs_ref[0, 0:length, kvcols(kv)].astype(BF16)
            picked = _mm(selt_ref[0, kv], onehot, TN)
            allowed = jnp.where(kpos <= qpos, picked, 0.0) > 0.5
            outs = attend(kv, k, v, allowed, tile_idx)
            for g in range(NSA_G):
                os_ref[kv * NSA_G + g] = outs[g]

    for i in range(t // kbs):
        pl.when((qi * tq) // kbs == i)(functools.partial(sel_branch, (i + 1) * kbs))

    lw = WIN_TILES * tq
    st = jnp.clip(qi - (WIN_TILES - 1), 0, t // tq - WIN_TILES)
    off = pl.multiple_of(st * tq, tq)
    dist = (qi - st) * tq + (lax.broadcasted_iota(jnp.int32, (tq, lw), 0)
                             - lax.broadcasted_iota(jnp.int32, (tq, lw), 1))
    win_allowed = jnp.where(dist >= 0, jnp.where(dist < NSA_WINDOW, 1.0, 0.0), 0.0) > 0.5
    win_tiles = [jnp.clip(qi - st - j, 0, 2) for j in range(WIN_TILES)]
    for kv in range(NSA_KV):
        k = kw_ref[0, pl.ds(off, lw), kvcols(kv)].astype(BF16)
        v = vw_ref[0, pl.ds(off, lw), kvcols(kv)].astype(BF16)
        o_win = attend(kv, k, v, win_allowed, win_tiles)
        for g in range(NSA_G):
            h = kv * NSA_G + g
            gate = [jax.nn.sigmoid(sm[:, SM_GATE + br * NSA_HEADS + h:SM_GATE + br * NSA_HEADS + h + 1])
                    for br in range(3)]
            cols = slice(h * NSA_DH, (h + 1) * NSA_DH)
            o_ref[0, :, cols] = (gate[0] * oc_ref[0, :, cols] + gate[1] * os_ref[h]
                                 + gate[2] * o_win[g]).astype(o_ref.dtype)


def _selwin(proj, selt, bias_tiles, o_cmp):
    b, t, _ = proj.shape
    tq = TQ
    nb = selt.shape[2]
    assert t >= WIN_TILES * tq
    kbs = math.gcd(t, 4 * tq)
    pair = NSA_KV * NSA_DH
    kvspec = lambda c6: pl.BlockSpec((1, t, pair), lambda i, j: (i, 0, C_KVN // pair + c6))
    return pl.pallas_call(
        functools.partial(_selwin_kernel, kbs=kbs),
        grid=(b, t // tq),
        in_specs=[pl.BlockSpec((1, tq, 1024), lambda i, j: (i, j, C_QN // 1024)),
                  kvspec(2), kvspec(3), kvspec(4), kvspec(5),
                  pl.BlockSpec((1, NSA_KV, nb, tq), lambda i, j: (i, 0, 0, j)),
                  pl.BlockSpec((NSA_HEADS, 3, tq, tq), lambda i, j: (0, 0, 0, 0)),
                  pl.BlockSpec((1, tq, 1024), lambda i, j: (i, j, 0)),
                  pl.BlockSpec((1, tq, LANE), lambda i, j: (i, j, C_SM // LANE))],
        out_specs=pl.BlockSpec((1, tq, NSA_HEADS * NSA_DH), lambda i, j: (i, j, 0)),
        out_shape=jax.ShapeDtypeStruct((b, t, NSA_HEADS * NSA_DH), BF16),
        scratch_shapes=[pltpu.VMEM((NSA_HEADS, tq, NSA_DH), F32)],
        compiler_params=_cparams(("parallel", "arbitrary")),
        name="selwin",
    )(proj, proj, proj, proj, proj, selt, bias_tiles, o_cmp, proj)


def _nsa_sample_kernel(pt_ref, q_ref, kvn_ref, gt_ref, win_ref, bsel_ref, bwin_ref, bcmp_ref, bself_ref,
                       *rest, n_pages, n_top):
    del pt_ref
    nrow = q_ref.shape[0]
    sel_pages = rest[:nrow * n_pages]
    kvc_pages = rest[nrow * n_pages:2 * nrow * n_pages]
    o_ref, wout_ref, kvc_s = rest[2 * nrow * n_pages:2 * nrow * n_pages + 3]
    n = n_pages * (PAGE_SIZE // NSA_BLK)
    past = n_pages * PAGE_SIZE
    wrows = win_ref.shape[0] // nrow
    wb = wrows // ROWS_PER_TOK
    R = SUBLANE
    U = [(r, kv) for r in range(nrow) for kv in range(NSA_KV)]
    NU = range(len(U))
    tok_rows = lambda ref, base, which, kv, ntok: ref[pl.ds(base + which * NSA_KV + kv, ntok,
                                                            stride=ROWS_PER_TOK), :]
    page = lambda r, i: sel_pages[r * n_pages + i]

    for r in range(nrow):
        wout_ref[r * wrows:(r + 1) * wrows - ROWS_PER_TOK, :] = win_ref[r * wrows + ROWS_PER_TOK:(r + 1) * wrows, :]
        wout_ref[(r + 1) * wrows - ROWS_PER_TOK:(r + 1) * wrows, :] = kvn_ref[r, 4 * NSA_KV:6 * NSA_KV, :]

    q8 = [q_ref[r] for r in range(nrow)]
    rowkv = lax.broadcasted_iota(jnp.int32, (R, 1), 0) // NSA_G
    bself = bself_ref[:, 0:1]

    for r in range(nrow):
        for i in range(n_pages):
            kvc_s[r, 2 * i:2 * i + 2, :] = kvc_pages[r * n_pages + i][0]
        kvc_s[r, n:LANE, :] = jnp.zeros((LANE - n, KV_ROW), F32)
    lane = lax.broadcasted_iota(jnp.int32, (R, LANE), 1)
    ii = lax.broadcasted_iota(jnp.int32, (LANE, LANE), 0)
    jj = lax.broadcasted_iota(jnp.int32, (LANE, LANE), 1)
    jrow = lax.broadcasted_iota(jnp.int32, (1, LANE), 1)
    eb = lax.broadcasted_iota(jnp.int32, (LANE, past), 0)
    ec = lax.broadcasted_iota(jnp.int32, (LANE, past), 1) // NSA_BLK
    expand = jnp.where(eb == ec, 1.0, 0.0).astype(BF16)
    wr = lax.broadcasted_iota(jnp.int32, (R, wb), 1)
    win_valid = (wb - wr) < NSA_WINDOW

    kc = [kvc_s[r, :, kv * NSA_DH:(kv + 1) * NSA_DH] for r, kv in U]
    vc = [kvc_s[r, :, (NSA_KV + kv) * NSA_DH:(NSA_KV + kv + 1) * NSA_DH] for r, kv in U]
    lg = [_mm3(q8[r], kc[u], NT) * NSA_SCALE + bcmp_ref[...] for u, (r, kv) in enumerate(U)]
    lg = [jnp.where(lane < n, x, NEG_INF) for x in lg]
    e = [jnp.exp(x - jnp.max(x, axis=-1, keepdims=True)) for x in lg]
    p = [jnp.where(lane < n, x / jnp.sum(x, axis=-1, keepdims=True), 0.0) for x in e]
    o_cmp = [_mm(p[u], vc[u]) for u in NU]
    imp = [jnp.sum(jnp.where(rowkv == kv, p[u], 0.0), axis=0, keepdims=True) for u, (r, kv) in enumerate(U)]
    forced = jnp.where(jrow == 0, 1.0, 0.0) + jnp.where(jrow == n, 1.0, 0.0) + jnp.where(jrow == n - 1, 1.0, 0.0)
    score = [jnp.where(jrow <= n, jnp.where(forced > 0.5, FORCE_SCORE, x), -2.0) for x in imp]
    s_b = [jnp.broadcast_to(x, (LANE, LANE)) for x in score]
    s_t = [jnp.transpose(x) for x in s_b]
    ahead = [jnp.where(s_t[u] > s_b[u], 1.0, jnp.where(s_t[u] == s_b[u], jnp.where(ii < jj, 1.0, 0.0), 0.0))
             for u in NU]
    rank = [jnp.sum(x, axis=0, keepdims=True) for x in ahead]
    sel8 = [jnp.broadcast_to(jnp.where(x < n_top, jnp.where(jrow <= n, 1.0, 0.0), 0.0), (R, LANE)) for x in rank]

    lg = [jnp.concatenate([_mm(q8[r], tok_rows(page(r, i), 0, 0, kv, PAGE_SIZE), NT) for i in range(n_pages)],
                          axis=1) * NSA_SCALE + bsel_ref[...] for r, kv in U]
    picked = [jnp.dot(x.astype(BF16), expand, preferred_element_type=F32) for x in sel8]
    lg = [jnp.where(picked[u] > 0.5, lg[u], NEG_INF) for u in NU]
    k_new = [kvn_ref[r, 2 * NSA_KV + kv:2 * NSA_KV + kv + 1, :] for r, kv in U]
    v_new = [kvn_ref[r, 3 * NSA_KV + kv:3 * NSA_KV + kv + 1, :] for r, kv in U]
    l_new = [jnp.sum(q8[r] * k_new[u], axis=-1, keepdims=True) * NSA_SCALE + bself for u, (r, kv) in enumerate(U)]
    l_new = [jnp.where(sel8[u][:, n:n + 1] > 0.5, l_new[u], NEG_INF) for u in NU]
    m = [jnp.maximum(jnp.max(lg[u], axis=-1, keepdims=True), l_new[u]) for u in NU]
    e = [jnp.exp(lg[u] - m[u]) for u in NU]
    e_new = [jnp.exp(l_new[u] - m[u]) for u in NU]
    den = [jnp.sum(e[u], axis=-1, keepdims=True) + e_new[u] for u in NU]
    pv = [e_new[u] * v_new[u] for u in NU]
    for i in range(n_pages):
        pv = [pv[u] + _mm(e[u][:, i * PAGE_SIZE:(i + 1) * PAGE_SIZE], tok_rows(page(r, i), 0, 1, kv, PAGE_SIZE))
              for u, (r, kv) in enumerate(U)]
    o_sel = [pv[u] / den[u] for u in NU]

    lg = [_mm(q8[r], tok_rows(win_ref, r * wrows, 0, kv, wb), NT) * NSA_SCALE + bwin_ref[...] for r, kv in U]
    lg = [jnp.where(win_valid, x, NEG_INF) for x in lg]
    k_new = [kvn_ref[r, 4 * NSA_KV + kv:4 * NSA_KV + kv + 1, :] for r, kv in U]
    v_new = [kvn_ref[r, 5 * NSA_KV + kv:5 * NSA_KV + kv + 1, :] for r, kv in U]
    l_new = [jnp.sum(q8[r] * k_new[u], axis=-1, keepdims=True) * NSA_SCALE + bself for u, (r, kv) in enumerate(U)]
    m = [jnp.maximum(jnp.max(lg[u], axis=-1, keepdims=True), l_new[u]) for u in NU]
    e = [jnp.exp(lg[u] - m[u]) for u in NU]
    e_new = [jnp.exp(l_new[u] - m[u]) for u in NU]
    den = [jnp.sum(e[u], axis=-1, keepdims=True) + e_new[u] for u in NU]
    o_win = [(e_new[u] * v_new[u] + _mm(e[u], tok_rows(win_ref, r * wrows, 1, kv, wb))) / den[u]
             for u, (r, kv) in enumerate(U)]

    for r in range(nrow):
        gate = jax.nn.sigmoid(gt_ref[r])
        pick = lambda outs: jnp.where(rowkv == 0, outs[r * NSA_KV], outs[r * NSA_KV + 1])
        o_ref[r] = gate[:, 0:1] * pick(o_cmp) + gate[:, 1:2] * pick(o_sel) + gate[:, 2:3] * pick(o_win)


NSA_SAMPLE_ROWS = 2


def _nsa_sample(page_table, q8, kvn, gates, win_rows, sel_rows, kvc_pool, layer, n_pool, wb,
                bsel, bwin, bcmp, bself):
    db, n_pages = page_table.shape
    nrow = NSA_SAMPLE_ROWS
    assert db % nrow == 0
    past = n_pages * PAGE_SIZE
    n_blocks = past // NSA_BLK + 1
    kern = functools.partial(_nsa_sample_kernel, n_pages=n_pages, n_top=min(NSA_TOPN, n_blocks))
    fixed = lambda shape: pl.BlockSpec(shape, lambda i, pt: (0,) * len(shape))
    wrows = wb * ROWS_PER_TOK
    prows = PAGE_SIZE * ROWS_PER_TOK
    in_specs = [pl.BlockSpec((nrow, NSA_HEADS, NSA_DH), lambda i, pt: (i, 0, 0)),
                pl.BlockSpec((nrow, 6 * NSA_KV, NSA_DH), lambda i, pt: (i, 0, 0)),
                pl.BlockSpec((nrow, NSA_HEADS, 3), lambda i, pt: (i, 0, 0)),
                pl.BlockSpec((nrow * wrows, NSA_DH), lambda i, pt: (layer * (db // nrow) + i, 0)),
                fixed((NSA_HEADS, past)), fixed((NSA_HEADS, wb)), fixed((NSA_HEADS, LANE)),
                fixed((NSA_HEADS, LANE))]
    for r in range(nrow):
        for p in range(n_pages):
            in_specs.append(pl.BlockSpec(
                (prows, NSA_DH), lambda i, pt, r=r, p=p: (layer * n_pool + pt[i * nrow + r, p], 0)))
    for r in range(nrow):
        for p in range(n_pages):
            in_specs.append(pl.BlockSpec(
                (1, PAGE_SIZE // NSA_BLK, KV_ROW), lambda i, pt, r=r, p=p: (pt[i * nrow + r, p], 0, 0)))
    return pl.pallas_call(
        kern,
        grid_spec=pltpu.PrefetchScalarGridSpec(
            num_scalar_prefetch=1, grid=(db // nrow,), in_specs=in_specs,
            out_specs=[pl.BlockSpec((nrow, NSA_HEADS, NSA_DH), lambda i, pt: (i, 0, 0)),
                       pl.BlockSpec((nrow * wrows, NSA_DH), lambda i, pt: (i, 0))],
            scratch_shapes=[pltpu.VMEM((nrow, LANE, KV_ROW), F32)]),
        out_shape=[jax.ShapeDtypeStruct((db, NSA_HEADS, NSA_DH), F32),
                   jax.ShapeDtypeStruct((db * wrows, NSA_DH), F32)],
        compiler_params=_cparams(("arbitrary",)),
        name="nsa_sample",
    )(page_table, q8, kvn, gates, win_rows, bsel, bwin, bcmp, bself,
      *([sel_rows] * (nrow * n_pages)), *([kvc_pool] * (nrow * n_pages)))


def _mix_kernel(oa_ref, ob_ref, ga_ref, gb_ref, x_ref, gt_ref, g_ref, woa_ref, wob_ref, wout_ref, o_ref):
    a = jnp.dot(oa_ref[0], woa_ref[...], preferred_element_type=F32)
    b = jnp.dot(ob_ref[0], wob_ref[...], preferred_element_type=F32)
    m = jax.nn.sigmoid(ga_ref[0]) * a + jax.nn.sigmoid(gb_ref[0]) * b
    mixed = _mm(m, wout_ref[...])
    o_ref[0] = x_ref[0] + gt_ref[0] * (_rms_rows(mixed) * g_ref[...])


def _resident(shape, index=None):
    index = index or (0,) * len(shape)
    return pl.BlockSpec(shape, lambda *_: index, pipeline_mode=pl.Buffered(1))


def _mix(o_a, o_b, proj, x3, gt, g, w_oa, w_ob, w_out):
    bb, t, d = x3.shape
    tm = _pick(t, 512)
    row = lambda c: pl.BlockSpec((1, tm, d), lambda b, i: (b, i, c))
    return pl.pallas_call(
        _mix_kernel,
        grid=(bb, t // tm),
        in_specs=[row(0), row(0), row(C_GM // d), row(C_GM // d + 1), row(0), _mod_spec(gt, tm),
                  _resident((1, d)), _resident((d, d)), _resident((d, d)), _resident((d, d))],
        out_specs=row(0),
        out_shape=jax.ShapeDtypeStruct((bb, t, d), F32),
        compiler_params=_cparams(("parallel", "parallel")),
        name="mix",
    )(o_a, o_b, proj, proj, x3, gt, g.reshape(1, d), w_oa, w_ob, w_out)


FFN_SPLIT = 2


def _ffn_kernel(x_ref, g2_ref, sc_ref, sh_ref, gt_ref, g3_ref, wg_ref, wu_ref, wd_ref, o_ref):
    x = x_ref[0]
    h = _modulated_norm(x, g2_ref[...], sc_ref[0], sh_ref[0])
    tf = D_FF // FFN_SPLIT
    y = None
    for f in range(FFN_SPLIT):
        cols = slice(f * tf, (f + 1) * tf)
        gate = jnp.dot(h, wg_ref[:, cols], preferred_element_type=F32)
        up = jnp.dot(h, wu_ref[:, cols], preferred_element_type=F32)
        part = _mm(_silu(gate) * up, wd_ref[cols, :])
        y = part if y is None else y + part
    o_ref[0] = x + gt_ref[0] * (_rms_rows(y) * g3_ref[...])


def _ffn(x3, g2, sc, sh, gt, g3, w_gu, w_down):
    bb, t, d = x3.shape
    tm = _pick(t, 512)
    assert (D_FF // FFN_SPLIT) % LANE == 0
    row = pl.BlockSpec((1, tm, d), lambda b, i: (b, i, 0))
    return pl.pallas_call(
        _ffn_kernel,
        grid=(bb, t // tm),
        in_specs=[row, _resident((1, d)), _mod_spec(sc, tm), _mod_spec(sh, tm), _mod_spec(gt, tm),
                  _resident((1, d)), _resident((d, D_FF), (0, 0)), _resident((d, D_FF), (0, 1)),
                  _resident((D_FF, d))],
        out_specs=row,
        out_shape=jax.ShapeDtypeStruct((bb, t, d), F32),
        compiler_params=_cparams(("parallel", "parallel")),
        name="ffn",
    )(x3, g2.reshape(1, d), sc, sh, gt, g3.reshape(1, d), w_gu, w_gu, w_down)


def _bucket_np(n):
    n = np.maximum(n, 0)
    exact = NUM_BUCKETS // 2
    big = exact + (np.log(np.maximum(n, 1).astype(np.float32) / np.float32(exact))
                   / np.float32(math.log(MAX_DISTANCE / exact))
                   * np.float32(NUM_BUCKETS - exact)).astype(np.int32)
    return np.where(n < exact, n, np.minimum(big, NUM_BUCKETS - 1)).astype(np.int32)


def _bias_tables(rel_bias, t, past, wb):
    def table(dist):
        idx = jnp.asarray(_bucket_np(dist))[None]
        heads = rel_bias.shape[1]
        out = jnp.zeros((heads,) + idx.shape[1:], rel_bias.dtype)
        for bucket in range(NUM_BUCKETS):
            val = rel_bias[bucket].reshape((heads,) + (1,) * (idx.ndim - 1))
            out = jnp.where(idx == bucket, val, out)
        return out

    n = t // NSA_BLK
    ends = np.arange(n) * NSA_BLK + (NSA_BLK - 1)
    cmp_p = table(np.arange(t)[None, :] - ends[:, None])
    ar = np.arange(TQ)
    tiles = table(np.stack([d * TQ + ar[:, None] - ar[None, :] for d in range(3)]))
    sel_s = table(past - np.arange(past))
    win_s = table(wb - np.arange(wb))
    n_s = past // NSA_BLK
    cmp_s = table(np.pad(past - (np.arange(n_s) * NSA_BLK + NSA_BLK - 1), (0, LANE - n_s)))
    self_s = table(np.zeros((LANE,), np.int64))
    return cmp_p, tiles, sel_s, win_s, cmp_s, self_s


def _regroup_w_in(w):
    o = np.cumsum((0, CONV_DIM, DN_HEADS * DN_DV, DN_HEADS, DN_HEADS, NSA_HEADS * NSA_DH,
                   6 * NSA_KV * NSA_DH, 3 * NSA_HEADS, 2 * D_MODEL))
    qkv_z, b, a, qn, kvn, gn, gm = (w[:, o[0]:o[2]], w[:, o[2]:o[3]], w[:, o[3]:o[4]], w[:, o[4]:o[5]],
                                    w[:, o[5]:o[6]], w[:, o[6]:o[7]], w[:, o[7]:o[8]])
    pad = jnp.zeros((w.shape[0], N_PROJ - C_SM - 5 * NSA_HEADS), w.dtype)
    return jnp.concatenate([qkv_z, qn, gm, kvn, b, a, gn, pad], axis=1).astype(BF16)


def _layer_params(l, w_in, conv_w, a_log, dt_bias, onorm_g, w_oa, w_ob, w_out, cmp_pe, cmp_w1, cmp_w2,
                  w_gu, w_down):
    hp = jnp.zeros((SUBLANE, LANE), F32)
    hp = hp.at[0, SM_A:SM_A + DN_HEADS].set(a_log[l]).at[1, SM_A:SM_A + DN_HEADS].set(dt_bias[l])
    return dict(
        w_in=_regroup_w_in(w_in[l]), cw=conv_w[l], hp=hp, on=onorm_g[l].reshape(1, DN_DV),
        w_oa=w_oa[l].astype(BF16), w_ob=w_ob[l].astype(BF16), w_out=w_out[l].astype(BF16),
        pe=jnp.reshape(cmp_pe[l], (1, NSA_BLK * 2 * NSA_DH)),
        w1=jnp.transpose(cmp_w1[l], (1, 0, 2, 3)).reshape(NSA_BLK * 2 * NSA_DH, NSA_CMP_HID).astype(BF16),
        w2=cmp_w2[l].reshape(2 * NSA_CMP_HID, NSA_DH).astype(BF16),
        w_gu=w_gu[l].astype(BF16), w_down=w_down[l].astype(BF16))


def _mods(mod, per_token):
    parts = [mod[:, i * D_MODEL:(i + 1) * D_MODEL] for i in range(6)]
    return [p[None, :, :] if per_token else p[:, None, :] for p in parts]


def _tail(x3, o_a, o_b, proj, mods, ng, P):
    sh1, sc1, gt1, sh2, sc2, gt2 = mods
    x1 = _mix(o_a, o_b, proj, x3, gt1, ng[1], P['w_oa'], P['w_ob'], P['w_out'])
    return _ffn(x1, ng[2], sc2, sh2, gt2, ng[3], P['w_gu'], P['w_down'])


def _kv_rows(proj, branch):
    b, t, _ = proj.shape
    c0 = C_KVN + branch * KV_ROW
    return proj[:, :, c0:c0 + KV_ROW].reshape(b, t, 2, NSA_KV, NSA_DH)


def _prompt_layer(x3, mods, ng, P, tabs):
    b, t, _ = x3.shape
    proj = _proj(x3, ng[0], mods[1], mods[0], P['w_in'])
    o_a, s_new = _dn_prompt(proj, P['cw'], P['hp'], P['on'])
    cmp_rows, sel_rows, win_rows = _kv_rows(proj, 0), _kv_rows(proj, 1), _kv_rows(proj, 2)
    n = t // NSA_BLK
    blocks = cmp_rows.reshape(b * t * ROWS_PER_TOK, NSA_DH)
    kvc = _compress(blocks, 0, b * n, P['pe'], P['w1'], P['w2']).reshape(b, n, KV_ROW)
    o_cmp, selt = _cmp_select(proj, kvc, tabs[0])
    o_b = _selwin(proj, selt, tabs[1], o_cmp)
    y = _tail(x3, o_a, o_b, proj, mods, ng, P)
    conv_new = proj[:, t - (CONV_W - 1):, C_QKV:C_QKV + CONV_DIM]
    return y, (cmp_rows, sel_rows, win_rows[:, t - min(NSA_WINDOW, t):], s_new, conv_new)


def _sample_layer(x3, mods, ng, P, tabs, layer, page_table, cmp_blocks, sel_rows, win_rows, n_pool, wb,
                  state_all, conv_all):
    db = x3.shape[1]
    proj = _proj(x3, ng[0], mods[1], mods[0], P['w_in'])
    rows = proj.reshape(db, 1, N_PROJ)
    o_a, s_new = _dn_sample(rows, conv_all, state_all, layer, P['cw'], P['hp'], P['on'])
    halves = PAGE_SIZE // NSA_BLK
    kvc_pool = _compress(cmp_blocks, layer * n_pool * halves, n_pool * halves, P['pe'], P['w1'], P['w2'])
    q8 = proj[0, :, C_QN:C_QN + NSA_HEADS * NSA_DH].reshape(db, NSA_HEADS, NSA_DH)
    kvn = proj[0, :, C_KVN:C_KVN + 6 * NSA_KV * NSA_DH].reshape(db, 6 * NSA_KV, NSA_DH)
    gates = jnp.transpose(proj[0, :, C_SM + SM_GATE:C_SM + SM_GATE + 3 * NSA_HEADS].reshape(db, 3, NSA_HEADS),
                          (0, 2, 1))
    o_b, win_new = _nsa_sample(page_table, q8, kvn, gates, win_rows, sel_rows,
                               kvc_pool.reshape(n_pool, halves, KV_ROW), layer, n_pool, wb,
                               tabs[2], tabs[3], tabs[4], tabs[5])
    y = _tail(x3, o_a.reshape(1, db, D_MODEL).astype(BF16), o_b.reshape(1, db, D_MODEL).astype(BF16),
              proj, mods, ng, P)
    qkv_new = proj[0, :, C_QKV:C_QKV + CONV_DIM]
    conv_new = jnp.concatenate([conv_all[layer][:, 1:], qkv_new[:, None, :]], axis=1)
    kv_new = kvn.reshape(db, 1, 6, NSA_KV, NSA_DH)
    return y, (kv_new[:, :, 0:2], kv_new[:, :, 2:4], win_new.reshape(db, wb, 2, NSA_KV, NSA_DH), s_new, conv_new)


def kernel(x_prompt, x_sample, cache_cmp_kv, cache_sel_kv, cache_win_kv, state_delta, state_conv, page_table, c_prompt, c_sample, w_ada, b_ada, norm_g, w_in, conv_w, a_log, dt_bias, onorm_g, w_oa, w_ob, w_out, cmp_pe, cmp_w1, cmp_w2, rel_bias, w_gu, w_down):
    depth = w_in.shape[0]
    b, t, d = x_prompt.shape
    db, ts, _ = x_sample.shape
    assert ts == 1 and d == D_MODEL and t % TQ == 0
    n_pool, n_pages = cache_cmp_kv.shape[1], page_table.shape[1]
    assert cache_cmp_kv.shape[2] == PAGE_SIZE
    past = n_pages * PAGE_SIZE
    wb = cache_win_kv.shape[2]
    tabs = _bias_tables(rel_bias, t, past, wb)
    cmp_blocks = cache_cmp_kv.reshape(depth * n_pool * PAGE_SIZE * ROWS_PER_TOK, NSA_DH)
    sel_rows = cache_sel_kv.reshape(depth * n_pool * PAGE_SIZE * ROWS_PER_TOK, NSA_DH)
    win_rows = cache_win_kv.reshape(depth * db * wb * ROWS_PER_TOK, NSA_DH)
    c_all = jnp.concatenate([c_prompt, c_sample], axis=0)
    pad_rows = (-c_all.shape[0]) % SUBLANE
    c_all = jnp.pad(c_all, ((0, pad_rows), (0, 0)))
    yp = x_prompt
    ys = x_sample.reshape(1, db, d)
    p_st, s_st = [], []
    for l in range(depth):
        P = _layer_params(l, w_in, conv_w, a_log, dt_bias, onorm_g, w_oa, w_ob, w_out, cmp_pe, cmp_w1,
                          cmp_w2, w_gu, w_down)
        mod = _ada(c_all, w_ada[l], b_ada[l])
        yp, st = _prompt_layer(yp, _mods(mod[:b], False), norm_g[l], P, tabs)
        p_st.append(st)
        ys, st = _sample_layer(ys, _mods(mod[b:b + db], True), norm_g[l], P, tabs, l, page_table,
                               cmp_blocks, sel_rows, win_rows, n_pool, wb, state_delta, state_conv)
        s_st.append(st)
    stk = lambda states, i: jnp.stack([s[i] for s in states])
    return (yp, ys.reshape(db, 1, d),
            stk(p_st, 0), stk(p_st, 1), stk(p_st, 2), stk(p_st, 3), stk(p_st, 4),
            stk(s_st, 0), stk(s_st, 1), stk(s_st, 2), stk(s_st, 3), stk(s_st, 4))
```

```python
import functools
import math

import numpy as np
import jax
import jax.numpy as jnp
from jax import lax
from jax.experimental import pallas as pl
from jax.experimental.pallas import tpu as pltpu

F32 = jnp.float32
BF16 = jnp.bfloat16

D_MODEL = 1024
EPS = 1e-6
DN_HEADS = 8
DN_DK = 128
DN_DV = 128
CONV_W = 4
CONV_DIM = 2 * DN_HEADS * DN_DK + DN_HEADS * DN_DV
DN_CHUNK = 64
NSA_HEADS = 8
NSA_KV = 2
NSA_G = NSA_HEADS // NSA_KV
NSA_DH = 128
NSA_SCALE = NSA_DH ** -0.5
NSA_BLK = 64
NSA_TOPN = 16
NSA_WINDOW = 512
NSA_CMP_HID = 128
FORCE_SCORE = 1e4
NEG_INF = -1e30
NUM_BUCKETS = 32
MAX_DISTANCE = 128
D_FF = 256 * (-(-8 * D_MODEL // (3 * 256)))
PAGE_SIZE = 128

C_QKV = 0
C_Z = 3072
C_QN = 4096
C_GM = 5120
C_KVN = 7168
C_SM = 8704
N_PROJ = 8832
N_MAIN = C_KVN
LOG2E = math.log2(math.e)
SM_BETA = 0
SM_A = 8
SM_GATE = 16
KV_ROW = 2 * NSA_KV * NSA_DH
ROWS_PER_TOK = 2 * NSA_KV

LANE = 128
SUBLANE = 8
VMEM_LIMIT = 48 * 1024 * 1024
TQ = 128
WIN_TILES = NSA_WINDOW // TQ + 1


def _cparams(sem):
    return pltpu.CompilerParams(dimension_semantics=sem, vmem_limit_bytes=VMEM_LIMIT)


def _pick(n, pref, mult=SUBLANE):
    for t in range(min(pref, n), 0, -1):
        if n % t == 0 and t % mult == 0:
            return t
    return n


def _mm(a, b, dims=((1,), (0,))):
    return lax.dot_general(a.astype(BF16), b.astype(BF16), (dims, ((), ())),
                           preferred_element_type=F32)


NN = ((1,), (0,))
NT = ((1,), (1,))
TN = ((0,), (0,))


def _hilo(a):
    hi = a.astype(BF16)
    lo = (a - hi.astype(F32)).astype(BF16)
    return hi, lo


def _mm3(a, b, dims=NN):
    ah, al = _hilo(a)
    bh, bl = _hilo(b)
    return _mm(ah, bh, dims) + (_mm(ah, bl, dims) + _mm(al, bh, dims))


def _mm_exact_lhs(a01, b):
    b0 = b.astype(BF16)
    r1 = b - b0.astype(F32)
    b1 = r1.astype(BF16)
    b2 = (r1 - b1.astype(F32)).astype(BF16)
    return _mm(a01, b0) + (_mm(a01, b1) + _mm(a01, b2))


def _silu(x):
    return x * jax.nn.sigmoid(x)


def _softplus(x):
    return jnp.maximum(x, 0.0) + jnp.log1p(jnp.exp(-jnp.abs(x)))


def _rms_rows(x):
    return x * lax.rsqrt(jnp.mean(x * x, axis=-1, keepdims=True) + EPS)


def _ada_kernel(c_ref, w_ref, b_ref, o_ref):
    o_ref[...] = _mm(_silu(c_ref[...]), w_ref[...]) + b_ref[...]


def _ada(c_all, w, b):
    m, d = c_all.shape
    n = w.shape[1]
    tn = _pick(n, 1536, LANE)
    return pl.pallas_call(
        _ada_kernel,
        grid=(n // tn,),
        in_specs=[pl.BlockSpec((m, d), lambda j: (0, 0)),
                  pl.BlockSpec((d, tn), lambda j: (0, j)),
                  pl.BlockSpec((1, tn), lambda j: (0, j))],
        out_specs=pl.BlockSpec((m, tn), lambda j: (0, j)),
        out_shape=jax.ShapeDtypeStruct((m, n), F32),
        compiler_params=_cparams(("parallel",)),
        name="ada",
    )(c_all, w, b.reshape(1, n))


def _modulated_norm(x, g, sc, sh):
    return ((_rms_rows(x) * g) * (1.0 + sc) + sh).astype(BF16)


def _proj_kernel(x_ref, g_ref, sc_ref, sh_ref, w_ref, o_ref):
    h = _modulated_norm(x_ref[0], g_ref[...], sc_ref[0], sh_ref[0])
    o_ref[0] = jnp.dot(h, w_ref[...], preferred_element_type=F32)


def _mod_spec(mod, tm):
    if mod.shape[1] == 1:
        return pl.BlockSpec((1, 1, D_MODEL), lambda b, i: (b, 0, 0))
    return pl.BlockSpec((1, tm, D_MODEL), lambda b, i: (b, i, 0))


def _kvproj_kernel(x_ref, g_ref, sc_ref, sh_ref, w_ref, cmp_ref, sel_ref, win_ref, sm_ref):
    tm = x_ref.shape[1]
    h = _modulated_norm(x_ref[0], g_ref[...], sc_ref[0], sh_ref[0])
    res = jnp.dot(h, w_ref[...], preferred_element_type=F32)
    for br, out in enumerate((cmp_ref, sel_ref, win_ref)):
        for r in range(ROWS_PER_TOK):
            c0 = br * KV_ROW + r * NSA_DH
            out[pl.ds(r, tm, stride=ROWS_PER_TOK), :] = res[:, c0:c0 + NSA_DH]
    sm_ref[0] = res[:, 3 * KV_ROW:3 * KV_ROW + LANE]


def _kvproj(x3, g, sc, sh, w_kv):
    bb, t, d = x3.shape
    n = w_kv.shape[1]
    tm = _pick(t, 512)
    nt = t // tm
    rows = pl.BlockSpec((tm * ROWS_PER_TOK, NSA_DH), lambda b, i: (b * nt + i, 0))
    rows_shape = jax.ShapeDtypeStruct((bb * t * ROWS_PER_TOK, NSA_DH), F32)
    return pl.pallas_call(
        _kvproj_kernel,
        grid=(bb, nt),
        in_specs=[pl.BlockSpec((1, tm, d), lambda b, i: (b, i, 0)), _resident((1, d)),
                  _mod_spec(sc, tm), _mod_spec(sh, tm), _resident((d, n))],
        out_specs=[rows, rows, rows, pl.BlockSpec((1, tm, LANE), lambda b, i: (b, i, 0))],
        out_shape=[rows_shape, rows_shape, rows_shape, jax.ShapeDtypeStruct((bb, t, LANE), F32)],
        compiler_params=_cparams(("parallel", "parallel")),
        name="kvproj",
    )(x3, g.reshape(1, d), sc, sh, w_kv)


def _resident(shape, index=None):
    index = index or (0,) * len(shape)
    return pl.BlockSpec(shape, lambda *_: index, pipeline_mode=pl.Buffered(1))


def _proj(x3, g, sc, sh, w):
    bb, t, d = x3.shape
    n = w.shape[1]
    tm = _pick(t, 512)
    tn = _pick(n, 3584, LANE)
    rows = lambda spec: pl.BlockSpec(spec.block_shape, lambda j, b, i: spec.index_map(b, i))
    return pl.pallas_call(
        _proj_kernel,
        grid=(n // tn, bb, t // tm),
        in_specs=[pl.BlockSpec((1, tm, d), lambda j, b, i: (b, i, 0)),
                  pl.BlockSpec((1, d), lambda j, b, i: (0, 0)),
                  rows(_mod_spec(sc, tm)), rows(_mod_spec(sh, tm)),
                  pl.BlockSpec((d, tn), lambda j, b, i: (0, j))],
        out_specs=pl.BlockSpec((1, tm, tn), lambda j, b, i: (b, i, j)),
        out_shape=jax.ShapeDtypeStruct((bb, t, n), F32),
        compiler_params=_cparams(("parallel", "parallel", "parallel")),
        name="proj",
    )(x3, g.reshape(1, d), sc, sh, w)


def _dn_prompt_kernel(qkv_ref, z_ref, sm_ref, cw_ref, hp_ref, on_ref, o_ref, s_ref, buf_ref, st_ref):
    c = pl.program_id(1)
    C = DN_CHUNK
    hist = SUBLANE

    @pl.when(c == 0)
    def _():
        buf_ref[0:hist, :] = jnp.zeros((hist, CONV_DIM), F32)
        st_ref[...] = jnp.zeros(st_ref.shape, F32)

    buf_ref[hist:hist + C, :] = qkv_ref[0]

    def conv_act(c0):
        acc = None
        for i in range(CONV_W):
            r0 = hist - (CONV_W - 1) + i
            term = buf_ref[r0:r0 + C, c0:c0 + LANE] * cw_ref[i:i + 1, c0:c0 + LANE]
            acc = term if acc is None else acc + term
        return _silu(acc)

    sm = sm_ref[0]
    beta_all = jax.nn.sigmoid(sm)
    g_all = -jnp.exp(hp_ref[0:1, :]) * _softplus(sm + hp_ref[1:2, :])
    ri = lax.broadcasted_iota(jnp.int32, (C, C), 0)
    ci = lax.broadcasted_iota(jnp.int32, (C, C), 1)
    incl = ri >= ci
    strict = ri > ci
    tri = jnp.where(incl, 1.0, 0.0).astype(BF16)
    eye = jnp.where(ri == ci, 1.0, 0.0)
    gc_all = _mm_exact_lhs(tri, g_all)
    gc_t = jnp.transpose(jnp.concatenate([gc_all, jnp.zeros((LANE - C, LANE), F32)], axis=0))
    eg_all = jnp.exp(gc_all)

    H = range(DN_HEADS)
    q = [conv_act(h * DN_DK) for h in H]
    k = [conv_act(DN_HEADS * DN_DK + h * DN_DK) for h in H]
    v = [conv_act(2 * DN_HEADS * DN_DK + h * DN_DV) for h in H]
    q = [x * lax.rsqrt(jnp.sum(x * x, axis=-1, keepdims=True) + EPS) * (DN_DK ** -0.5) for x in q]
    k = [x * lax.rsqrt(jnp.sum(x * x, axis=-1, keepdims=True) + EPS) for x in k]
    beta = [beta_all[:, SM_BETA + h:SM_BETA + h + 1] for h in H]
    gc = [gc_all[:, SM_A + h:SM_A + h + 1] for h in H]
    eg = [eg_all[:, SM_A + h:SM_A + h + 1] for h in H]
    g_last = [gc_all[C - 1:C, SM_A + h:SM_A + h + 1] for h in H]
    decay = [jnp.exp(jnp.where(incl, gc[h] - gc_t[SM_A + h:SM_A + h + 1, 0:C], NEG_INF)) for h in H]
    kb = [k[h] * beta[h] for h in H]
    a = [jnp.where(strict, _mm(kb[h], k[h], NT) * decay[h], 0.0) for h in H]
    qk = [_mm(q[h], k[h], NT) * decay[h] for h in H]
    t = [eye - a[h] for h in H]
    p = [_mm(a[h], a[h]) for h in H]
    steps = int(math.log2(C)) - 1
    for s in range(steps):
        t = [t[h] + _mm(t[h], p[h]) for h in H]
        if s + 1 < steps:
            p = [_mm(p[h], p[h]) for h in H]
    resid = [eye - t[h] - _mm3(a[h], t[h]) for h in H]
    t = [t[h] + _mm(t[h], resid[h]) for h in H]
    u = [_mm(t[h], v[h] * beta[h]) for h in H]
    w = [_mm(t[h], kb[h] * eg[h]) for h in H]
    s_old = [st_ref[h] for h in H]
    v_new = [u[h] - _mm(w[h], s_old[h]) for h in H]
    o = [_mm(q[h] * eg[h], s_old[h]) + _mm(qk[h], v_new[h]) for h in H]
    for h in H:
        st_ref[h] = s_old[h] * jnp.exp(g_last[h]) + _mm(k[h] * jnp.exp(g_last[h] - gc[h]), v_new[h], TN)
    for h in H:
        z = z_ref[0, :, h * DN_DV:(h + 1) * DN_DV]
        o_ref[0, :, h * DN_DV:(h + 1) * DN_DV] = (_rms_rows(o[h]) * on_ref[...] * _silu(z)).astype(o_ref.dtype)

    buf_ref[0:hist, :] = buf_ref[C:C + hist, :]

    @pl.when(c == pl.num_programs(1) - 1)
    def _():
        s_ref[0] = st_ref[...]


def _dn_prompt(proj, small, cw, hp, on):
    b, t, _ = proj.shape
    C = DN_CHUNK
    assert t % C == 0
    return pl.pallas_call(
        _dn_prompt_kernel,
        grid=(b, t // C),
        in_specs=[pl.BlockSpec((1, C, CONV_DIM), lambda i, c: (i, c, C_QKV // CONV_DIM)),
                  pl.BlockSpec((1, C, 1024), lambda i, c: (i, c, C_Z // 1024)),
                  pl.BlockSpec((1, C, LANE), lambda i, c: (i, c, 0)),
                  pl.BlockSpec((CONV_W, CONV_DIM), lambda i, c: (0, 0)),
                  pl.BlockSpec((SUBLANE, LANE), lambda i, c: (0, 0)),
                  pl.BlockSpec((1, DN_DV), lambda i, c: (0, 0))],
        out_specs=[pl.BlockSpec((1, C, DN_HEADS * DN_DV), lambda i, c: (i, c, 0)),
                   pl.BlockSpec((1, DN_HEADS, DN_DK, DN_DV), lambda i, c: (i, 0, 0, 0))],
        out_shape=[jax.ShapeDtypeStruct((b, t, DN_HEADS * DN_DV), BF16),
                   jax.ShapeDtypeStruct((b, DN_HEADS, DN_DK, DN_DV), F32)],
        scratch_shapes=[pltpu.VMEM((C + SUBLANE, CONV_DIM), F32),
                        pltpu.VMEM((DN_HEADS, DN_DK, DN_DV), F32)],
        compiler_params=_cparams(("parallel", "arbitrary")),
        name="dn_prompt",
    )(proj, proj, small, cw, hp, on)


def _dn_sample_kernel(qkv_ref, z_ref, sm_ref, cb_ref, cw_ref, hp_ref, on_ref, sin_ref, o_ref, sout_ref):
    xb = cb_ref[0, 0]
    conv = qkv_ref[0] * cw_ref[CONV_W - 1:CONV_W, :]
    for i in range(CONV_W - 1):
        conv = conv + xb[i:i + 1, :] * cw_ref[i:i + 1, :]
    act = _silu(conv)
    sm = sm_ref[0]
    beta_all = jax.nn.sigmoid(sm)
    eg_all = jnp.exp(-jnp.exp(hp_ref[0:1, :]) * _softplus(sm + hp_ref[1:2, :]))
    row = lax.broadcasted_iota(jnp.int32, (SUBLANE, DN_DK), 0)
    rows = lambda x: jnp.broadcast_to(x, row.shape)
    H = range(DN_HEADS)
    q = [act[:, h * DN_DK:(h + 1) * DN_DK] for h in H]
    k = [act[:, (DN_HEADS + h) * DN_DK:(DN_HEADS + h + 1) * DN_DK] for h in H]
    v = [act[:, 2 * DN_HEADS * DN_DK + h * DN_DV:2 * DN_HEADS * DN_DK + (h + 1) * DN_DV] for h in H]
    q = [x * lax.rsqrt(jnp.sum(x * x, axis=-1, keepdims=True) + EPS) * (DN_DK ** -0.5) for x in q]
    k = [x * lax.rsqrt(jnp.sum(x * x, axis=-1, keepdims=True) + EPS) for x in k]
    beta = [beta_all[:, SM_BETA + h:SM_BETA + h + 1] for h in H]
    eg = [eg_all[:, SM_A + h:SM_A + h + 1] for h in H]
    s_old = [sin_ref[0, 0, h] for h in H]
    kq = [jnp.where(row == 0, rows(k[h]), jnp.where(row == 1, rows(q[h]), 0.0)) for h in H]
    r = [_mm3(kq[h], s_old[h]) for h in H]
    v_new = [beta[h] * (v[h] - eg[h] * r[h][0:1, :]) for h in H]
    qk = [jnp.sum(q[h] * k[h], axis=-1, keepdims=True) for h in H]
    o = [eg[h] * r[h][1:2, :] + qk[h] * v_new[h] for h in H]
    outer = [_mm3(jnp.where(row == 0, rows(k[h]), 0.0), rows(v_new[h]), TN) for h in H]
    for h in H:
        sout_ref[0, h] = s_old[h] * eg[h] + outer[h]
    for h in H:
        z = z_ref[0, :, h * DN_DV:(h + 1) * DN_DV]
        o_ref[0, :, h * DN_DV:(h + 1) * DN_DV] = _rms_rows(o[h]) * on_ref[...] * _silu(z)


def _dn_sample(proj_rows, small_rows, conv_all, state_all, layer, cw, hp, on):
    db = proj_rows.shape[0]
    return pl.pallas_call(
        _dn_sample_kernel,
        grid=(db,),
        in_specs=[pl.BlockSpec((1, 1, CONV_DIM), lambda i: (i, 0, C_QKV // CONV_DIM)),
                  pl.BlockSpec((1, 1, 1024), lambda i: (i, 0, C_Z // 1024)),
                  pl.BlockSpec((1, 1, LANE), lambda i: (i, 0, 0)),
                  pl.BlockSpec((1, 1, CONV_W - 1, CONV_DIM), lambda i: (layer, i, 0, 0)),
                  pl.BlockSpec((CONV_W, CONV_DIM), lambda i: (0, 0)),
                  pl.BlockSpec((SUBLANE, LANE), lambda i: (0, 0)),
                  pl.BlockSpec((1, DN_DV), lambda i: (0, 0)),
                  pl.BlockSpec((1, 1, DN_HEADS, DN_DK, DN_DV), lambda i: (layer, i, 0, 0, 0))],
        out_specs=[pl.BlockSpec((1, 1, DN_HEADS * DN_DV), lambda i: (i, 0, 0)),
                   pl.BlockSpec((1, DN_HEADS, DN_DK, DN_DV), lambda i: (i, 0, 0, 0))],
        out_shape=[jax.ShapeDtypeStruct((db, 1, DN_HEADS * DN_DV), F32),
                   jax.ShapeDtypeStruct((db, DN_HEADS, DN_DK, DN_DV), F32)],
        compiler_params=_cparams(("parallel",)),
        name="dn_sample",
    )(proj_rows, proj_rows, small_rows, conv_all, cw, hp, on, state_all)


BLK_ROWS = NSA_BLK * ROWS_PER_TOK


def _compress_kernel(x_ref, pe_ref, w1_ref, w2_ref, o_ref):
    tb = x_ref.shape[0] // BLK_ROWS
    for ck in range(ROWS_PER_TOK):
        c = ck // NSA_KV
        acc = jnp.zeros((tb, NSA_CMP_HID), F32)
        for l in range(NSA_BLK):
            x = (x_ref[pl.ds(l * ROWS_PER_TOK + ck, tb, stride=BLK_ROWS), :]
                 + pe_ref[:, (l * 2 + c) * NSA_DH:(l * 2 + c + 1) * NSA_DH])
            acc = acc + _mm(x, w1_ref[(l * 2 + c) * NSA_DH:(l * 2 + c + 1) * NSA_DH, :])
        o_ref[:, ck * NSA_DH:(ck + 1) * NSA_DH] = _mm(
            _silu(acc), w2_ref[c * NSA_CMP_HID:(c + 1) * NSA_CMP_HID, :])


def _compress(x_rows, row0, r, pe_flat, w1_r, w2_r):
    tb = _pick(r, 64)
    assert row0 % tb == 0
    return pl.pallas_call(
        _compress_kernel,
        grid=(r // tb,),
        in_specs=[pl.BlockSpec((tb * BLK_ROWS, NSA_DH), lambda i: (row0 // tb + i, 0)),
                  pl.BlockSpec((1, NSA_BLK * 2 * NSA_DH), lambda i: (0, 0)),
                  pl.BlockSpec((NSA_BLK * 2 * NSA_DH, NSA_CMP_HID), lambda i: (0, 0)),
                  pl.BlockSpec((2 * NSA_CMP_HID, NSA_DH), lambda i: (0, 0))],
        out_specs=pl.BlockSpec((tb, KV_ROW), lambda i: (i, 0)),
        out_shape=jax.ShapeDtypeStruct((r, KV_ROW), F32),
        compiler_params=_cparams(("parallel",)),
        name="compress",
    )(x_rows, pe_flat, w1_r, w2_r)


def _cmp_select_kernel(q_ref, kvc_ref, bias_ref, o_ref, selt_ref, *, n_top):
    qi = pl.program_id(1)
    tq = q_ref.shape[1]
    n = kvc_ref.shape[1]
    qpos = qi * tq + lax.broadcasted_iota(jnp.int32, (n, tq), 1)
    jn = lax.broadcasted_iota(jnp.int32, (n, tq), 0)
    valid = qpos >= jn * NSA_BLK + (NSA_BLK - 1)
    validf = jnp.where(valid, 1.0, 0.0)
    cur = qpos // NSA_BLK
    forced = jnp.where(jn == 0, 1.0, 0.0) + jnp.where(jn == cur, 1.0, 0.0) + jnp.where(jn == cur - 1, 1.0, 0.0)
    kvc = kvc_ref[0]
    kc = [kvc[:, kv * NSA_DH:(kv + 1) * NSA_DH] for kv in range(NSA_KV)]
    vc = [kvc[:, (NSA_KV + kv) * NSA_DH:(NSA_KV + kv + 1) * NSA_DH] for kv in range(NSA_KV)]
    HH = range(NSA_HEADS)
    lg = [_mm3(kc[h // NSA_G], q_ref[0, :, h * NSA_DH:(h + 1) * NSA_DH], NT) * NSA_SCALE + bias_ref[h]
          for h in HH]
    lg = [jnp.where(valid, x, NEG_INF) for x in lg]
    e = [jnp.exp(x - jnp.max(x, axis=0, keepdims=True)) for x in lg]
    p = [x / jnp.sum(x, axis=0, keepdims=True) * validf for x in e]
    for h in HH:
        o_ref[0, :, h * NSA_DH:(h + 1) * NSA_DH] = _mm(p[h], vc[h // NSA_G], TN)
    for kv in range(NSA_KV):
        imp = p[kv * NSA_G]
        for g in range(1, NSA_G):
            imp = imp + p[kv * NSA_G + g]
        score = jnp.where(jn <= cur, jnp.where(forced > 0.5, FORCE_SCORE, imp), -1.0)
        rank = jnp.zeros((n, tq), F32)
        for i in range(n):
            si = score[i:i + 1, :]
            rank = rank + jnp.where(si > score, 1.0, jnp.where(si == score, jnp.where(jn > i, 1.0, 0.0), 0.0))
        selt_ref[0, kv] = jnp.where(rank < n_top, 1.0, 0.0)


def _cmp_select(proj, kvc, bias_cmp):
    b, t, _ = proj.shape
    n = kvc.shape[1]
    tq = TQ
    kern = functools.partial(_cmp_select_kernel, n_top=min(NSA_TOPN, n))
    return pl.pallas_call(
        kern,
        grid=(b, t // tq),
        in_specs=[pl.BlockSpec((1, tq, 1024), lambda i, j: (i, j, C_QN // 1024)),
                  pl.BlockSpec((1, n, KV_ROW), lambda i, j: (i, 0, 0)),
                  pl.BlockSpec((NSA_HEADS, n, tq), lambda i, j: (0, 0, j))],
        out_specs=[pl.BlockSpec((1, tq, NSA_HEADS * NSA_DH), lambda i, j: (i, j, 0)),
                   pl.BlockSpec((1, NSA_KV, n, tq), lambda i, j: (i, 0, 0, j))],
        out_shape=[jax.ShapeDtypeStruct((b, t, NSA_HEADS * NSA_DH), F32),
                   jax.ShapeDtypeStruct((b, NSA_KV, n, t), F32)],
        compiler_params=_cparams(("parallel", "parallel")),
        name="cmp_select",
    )(proj, kvc, bias_cmp)


def _selwin_kernel(q_ref, sel_ref, win_ref, selt_ref, bt_ref, oc_ref, sm_ref, o_ref, os_ref, *, kbs):
    qi = pl.program_id(1)
    tq = TQ
    t = sel_ref.shape[0] // ROWS_PER_TOK
    nb = selt_ref.shape[2]
    sm = sm_ref[0]
    tok_rows = lambda ref, start, which, kv, ntok: ref[pl.ds(start + which * NSA_KV + kv, ntok,
                                                             stride=ROWS_PER_TOK), :].astype(BF16)
    q4 = [jnp.concatenate([q_ref[0, :, (kv * NSA_G + g) * NSA_DH:(kv * NSA_G + g + 1) * NSA_DH]
                           for g in range(NSA_G)], axis=0) * (NSA_SCALE * LOG2E) for kv in range(NSA_KV)]
    q4 = [x.astype(BF16) for x in q4]

    def attend(kv, kv_ref, start, ntile, allowed, first_tile):
        length = ntile * tq
        k = tok_rows(kv_ref, start, 0, kv, length)
        v = tok_rows(kv_ref, start, 1, kv, length)
        s = lax.dot_general(q4[kv], k, (NT, ((), ())), preferred_element_type=F32)
        outs = []
        for g in range(NSA_G):
            h = kv * NSA_G + g
            bias = jnp.concatenate([bt_ref[h, jnp.clip(first_tile - j, 0, 2)] for j in range(ntile)], axis=1)
            sg = jnp.where(allowed, s[g * tq:(g + 1) * tq] + bias, NEG_INF)
            e = jnp.exp2(sg - jnp.max(sg, axis=-1, keepdims=True))
            den = jnp.sum(e, axis=-1, keepdims=True)
            outs.append(jnp.dot(e.astype(BF16), v, preferred_element_type=F32) / den)
        return outs

    def sel_branch(length):
        qpos = qi * tq + lax.broadcasted_iota(jnp.int32, (tq, length), 0)
        kpos = lax.broadcasted_iota(jnp.int32, (tq, length), 1)
        eb = lax.broadcasted_iota(jnp.int32, (nb, length), 0)
        ec = lax.broadcasted_iota(jnp.int32, (nb, length), 1) // NSA_BLK
        onehot = jnp.where(eb == ec, 1.0, 0.0).astype(BF16)
        for kv in range(NSA_KV):
            picked = _mm(selt_ref[0, kv], onehot, TN)
            allowed = jnp.where(kpos <= qpos, picked, 0.0) > 0.5
            outs = attend(kv, sel_ref, 0, length // tq, allowed, qi)
            for g in range(NSA_G):
                os_ref[kv * NSA_G + g] = outs[g]

    for i in range(t // kbs):
        pl.when((qi * tq) // kbs == i)(functools.partial(sel_branch, (i + 1) * kbs))

    lw = WIN_TILES * tq
    st = jnp.clip(qi - (WIN_TILES - 1), 0, t // tq - WIN_TILES)
    off = pl.multiple_of(st * tq * ROWS_PER_TOK, tq * ROWS_PER_TOK)
    dist = (qi - st) * tq + (lax.broadcasted_iota(jnp.int32, (tq, lw), 0)
                             - lax.broadcasted_iota(jnp.int32, (tq, lw), 1))
    win_allowed = jnp.where(dist >= 0, jnp.where(dist < NSA_WINDOW, 1.0, 0.0), 0.0) > 0.5
    for kv in range(NSA_KV):
        o_win = attend(kv, win_ref, off, WIN_TILES, win_allowed, qi - st)
        for g in range(NSA_G):
            h = kv * NSA_G + g
            gate = [jax.nn.sigmoid(sm[:, SM_GATE + br * NSA_HEADS + h:SM_GATE + br * NSA_HEADS + h + 1])
                    for br in range(3)]
            cols = slice(h * NSA_DH, (h + 1) * NSA_DH)
            o_ref[0, :, cols] = (gate[0] * oc_ref[0, :, cols] + gate[1] * os_ref[h]
                                 + gate[2] * o_win[g]).astype(o_ref.dtype)


def _selwin(proj, sel_rows, win_rows, small, selt, bias_tiles, o_cmp):
    b, t, _ = proj.shape
    tq = TQ
    nb = selt.shape[2]
    assert t >= WIN_TILES * tq
    kbs = math.gcd(t, 4 * tq)
    kvspec = pl.BlockSpec((t * ROWS_PER_TOK, NSA_DH), lambda i, j: (i, 0))
    return pl.pallas_call(
        functools.partial(_selwin_kernel, kbs=kbs),
        grid=(b, t // tq),
        in_specs=[pl.BlockSpec((1, tq, 1024), lambda i, j: (i, j, C_QN // 1024)),
                  kvspec, kvspec,
                  pl.BlockSpec((1, NSA_KV, nb, tq), lambda i, j: (i, 0, 0, j)),
                  _resident((NSA_HEADS, 3, tq, tq)),
                  pl.BlockSpec((1, tq, 1024), lambda i, j: (i, j, 0)),
                  pl.BlockSpec((1, tq, LANE), lambda i, j: (i, j, 0))],
        out_specs=pl.BlockSpec((1, tq, NSA_HEADS * NSA_DH), lambda i, j: (i, j, 0)),
        out_shape=jax.ShapeDtypeStruct((b, t, NSA_HEADS * NSA_DH), BF16),
        scratch_shapes=[pltpu.VMEM((NSA_HEADS, tq, NSA_DH), F32)],
        compiler_params=_cparams(("parallel", "arbitrary")),
        name="selwin",
    )(proj, sel_rows, win_rows, selt, bias_tiles, o_cmp, small)


def _nsa_sample_kernel(pt_ref, q_ref, seln_ref, winn_ref, gt_ref, win_ref, bsel_ref, bwin_ref, bcmp_ref,
                       bself_ref, *rest, n_pages, n_top):
    del pt_ref
    new_row = lambda ref, r, which, kv: ref[r * ROWS_PER_TOK + which * NSA_KV + kv:
                                            r * ROWS_PER_TOK + which * NSA_KV + kv + 1, :]
    nrow = q_ref.shape[0]
    sel_pages = rest[:nrow * n_pages]
    kvc_pages = rest[nrow * n_pages:2 * nrow * n_pages]
    o_ref, wout_ref, kvc_s = rest[2 * nrow * n_pages:2 * nrow * n_pages + 3]
    n = n_pages * (PAGE_SIZE // NSA_BLK)
    past = n_pages * PAGE_SIZE
    wrows = win_ref.shape[0] // nrow
    wb = wrows // ROWS_PER_TOK
    R = SUBLANE
    U = [(r, kv) for r in range(nrow) for kv in range(NSA_KV)]
    NU = range(len(U))
    tok_rows = lambda ref, base, which, kv, ntok: ref[pl.ds(base + which * NSA_KV + kv, ntok,
                                                            stride=ROWS_PER_TOK), :]
    page = lambda r, i: sel_pages[r * n_pages + i]

    for r in range(nrow):
        wout_ref[r * wrows:(r + 1) * wrows - ROWS_PER_TOK, :] = win_ref[r * wrows + ROWS_PER_TOK:(r + 1) * wrows, :]
        wout_ref[(r + 1) * wrows - ROWS_PER_TOK:(r + 1) * wrows, :] = winn_ref[r * ROWS_PER_TOK:
                                                                                (r + 1) * ROWS_PER_TOK, :]

    q8 = [q_ref[r] for r in range(nrow)]
    rowkv = lax.broadcasted_iota(jnp.int32, (R, 1), 0) // NSA_G
    bself = bself_ref[:, 0:1]

    for r in range(nrow):
        for i in range(n_pages):
            kvc_s[r, 2 * i:2 * i + 2, :] = kvc_pages[r * n_pages + i][0]
        kvc_s[r, n:LANE, :] = jnp.zeros((LANE - n, KV_ROW), F32)
    lane = lax.broadcasted_iota(jnp.int32, (R, LANE), 1)
    ii = lax.broadcasted_iota(jnp.int32, (LANE, LANE), 0)
    jj = lax.broadcasted_iota(jnp.int32, (LANE, LANE), 1)
    jrow = lax.broadcasted_iota(jnp.int32, (1, LANE), 1)
    eb = lax.broadcasted_iota(jnp.int32, (LANE, past), 0)
    ec = lax.broadcasted_iota(jnp.int32, (LANE, past), 1) // NSA_BLK
    expand = jnp.where(eb == ec, 1.0, 0.0).astype(BF16)
    wr = lax.broadcasted_iota(jnp.int32, (R, wb), 1)
    win_valid = (wb - wr) < NSA_WINDOW

    kc = [kvc_s[r, :, kv * NSA_DH:(kv + 1) * NSA_DH] for r, kv in U]
    vc = [kvc_s[r, :, (NSA_KV + kv) * NSA_DH:(NSA_KV + kv + 1) * NSA_DH] for r, kv in U]
    lg = [_mm3(q8[r], kc[u], NT) * NSA_SCALE + bcmp_ref[...] for u, (r, kv) in enumerate(U)]
    lg = [jnp.where(lane < n, x, NEG_INF) for x in lg]
    e = [jnp.exp(x - jnp.max(x, axis=-1, keepdims=True)) for x in lg]
    p = [jnp.where(lane < n, x / jnp.sum(x, axis=-1, keepdims=True), 0.0) for x in e]
    o_cmp = [_mm(p[u], vc[u]) for u in NU]
    imp = [jnp.sum(jnp.where(rowkv == kv, p[u], 0.0), axis=0, keepdims=True) for u, (r, kv) in enumerate(U)]
    forced = jnp.where(jrow == 0, 1.0, 0.0) + jnp.where(jrow == n, 1.0, 0.0) + jnp.where(jrow == n - 1, 1.0, 0.0)
    score = [jnp.where(jrow <= n, jnp.where(forced > 0.5, FORCE_SCORE, x), -2.0) for x in imp]
    s_b = [jnp.broadcast_to(x, (LANE, LANE)) for x in score]
    s_t = [jnp.transpose(x) for x in s_b]
    ahead = [jnp.where(s_t[u] > s_b[u], 1.0, jnp.where(s_t[u] == s_b[u], jnp.where(ii < jj, 1.0, 0.0), 0.0))
             for u in NU]
    rank = [jnp.sum(x, axis=0, keepdims=True) for x in ahead]
    sel8 = [jnp.broadcast_to(jnp.where(x < n_top, jnp.where(jrow <= n, 1.0, 0.0), 0.0), (R, LANE)) for x in rank]

    lg = [jnp.concatenate([_mm(q8[r], tok_rows(page(r, i), 0, 0, kv, PAGE_SIZE), NT) for i in range(n_pages)],
                          axis=1) * NSA_SCALE + bsel_ref[...] for r, kv in U]
    picked = [jnp.dot(x.astype(BF16), expand, preferred_element_type=F32) for x in sel8]
    lg = [jnp.where(picked[u] > 0.5, lg[u], NEG_INF) for u in NU]
    k_new = [new_row(seln_ref, r, 0, kv) for r, kv in U]
    v_new = [new_row(seln_ref, r, 1, kv) for r, kv in U]
    l_new = [jnp.sum(q8[r] * k_new[u], axis=-1, keepdims=True) * NSA_SCALE + bself for u, (r, kv) in enumerate(U)]
    l_new = [jnp.where(sel8[u][:, n:n + 1] > 0.5, l_new[u], NEG_INF) for u in NU]
    m = [jnp.maximum(jnp.max(lg[u], axis=-1, keepdims=True), l_new[u]) for u in NU]
    e = [jnp.exp(lg[u] - m[u]) for u in NU]
    e_new = [jnp.exp(l_new[u] - m[u]) for u in NU]
    den = [jnp.sum(e[u], axis=-1, keepdims=True) + e_new[u] for u in NU]
    pv = [e_new[u] * v_new[u] for u in NU]
    for i in range(n_pages):
        pv = [pv[u] + _mm(e[u][:, i * PAGE_SIZE:(i + 1) * PAGE_SIZE], tok_rows(page(r, i), 0, 1, kv, PAGE_SIZE))
              for u, (r, kv) in enumerate(U)]
    o_sel = [pv[u] / den[u] for u in NU]

    lg = [_mm(q8[r], tok_rows(win_ref, r * wrows, 0, kv, wb), NT) * NSA_SCALE + bwin_ref[...] for r, kv in U]
    lg = [jnp.where(win_valid, x, NEG_INF) for x in lg]
    k_new = [new_row(winn_ref, r, 0, kv) for r, kv in U]
    v_new = [new_row(winn_ref, r, 1, kv) for r, kv in U]
    l_new = [jnp.sum(q8[r] * k_new[u], axis=-1, keepdims=True) * NSA_SCALE + bself for u, (r, kv) in enumerate(U)]
    m = [jnp.maximum(jnp.max(lg[u], axis=-1, keepdims=True), l_new[u]) for u in NU]
    e = [jnp.exp(lg[u] - m[u]) for u in NU]
    e_new = [jnp.exp(l_new[u] - m[u]) for u in NU]
    den = [jnp.sum(e[u], axis=-1, keepdims=True) + e_new[u] for u in NU]
    o_win = [(e_new[u] * v_new[u] + _mm(e[u], tok_rows(win_ref, r * wrows, 1, kv, wb))) / den[u]
             for u, (r, kv) in enumerate(U)]

    for r in range(nrow):
        gate = jax.nn.sigmoid(gt_ref[r])
        pick = lambda outs: jnp.where(rowkv == 0, outs[r * NSA_KV], outs[r * NSA_KV + 1])
        o_ref[r] = gate[:, 0:1] * pick(o_cmp) + gate[:, 1:2] * pick(o_sel) + gate[:, 2:3] * pick(o_win)


NSA_SAMPLE_ROWS = 2


def _nsa_sample(page_table, q8, sel_new, win_new, gates, win_rows, sel_rows, kvc_pool, layer, n_pool, wb,
                bsel, bwin, bcmp, bself):
    db, n_pages = page_table.shape
    nrow = NSA_SAMPLE_ROWS
    assert db % nrow == 0
    past = n_pages * PAGE_SIZE
    n_blocks = past // NSA_BLK + 1
    kern = functools.partial(_nsa_sample_kernel, n_pages=n_pages, n_top=min(NSA_TOPN, n_blocks))
    fixed = lambda shape: pl.BlockSpec(shape, lambda i, pt: (0,) * len(shape))
    wrows = wb * ROWS_PER_TOK
    prows = PAGE_SIZE * ROWS_PER_TOK
    in_specs = [pl.BlockSpec((nrow, NSA_HEADS, NSA_DH), lambda i, pt: (i, 0, 0)),
                pl.BlockSpec((nrow * ROWS_PER_TOK, NSA_DH), lambda i, pt: (i, 0)),
                pl.BlockSpec((nrow * ROWS_PER_TOK, NSA_DH), lambda i, pt: (i, 0)),
                pl.BlockSpec((nrow, NSA_HEADS, 3), lambda i, pt: (i, 0, 0)),
                pl.BlockSpec((nrow * wrows, NSA_DH), lambda i, pt: (layer * (db // nrow) + i, 0)),
                fixed((NSA_HEADS, past)), fixed((NSA_HEADS, wb)), fixed((NSA_HEADS, LANE)),
                fixed((NSA_HEADS, LANE))]
    for r in range(nrow):
        for p in range(n_pages):
            in_specs.append(pl.BlockSpec(
                (prows, NSA_DH), lambda i, pt, r=r, p=p: (layer * n_pool + pt[i * nrow + r, p], 0)))
    for r in range(nrow):
        for p in range(n_pages):
            in_specs.append(pl.BlockSpec(
                (1, PAGE_SIZE // NSA_BLK, KV_ROW), lambda i, pt, r=r, p=p: (pt[i * nrow + r, p], 0, 0)))
    return pl.pallas_call(
        kern,
        grid_spec=pltpu.PrefetchScalarGridSpec(
            num_scalar_prefetch=1, grid=(db // nrow,), in_specs=in_specs,
            out_specs=[pl.BlockSpec((nrow, NSA_HEADS, NSA_DH), lambda i, pt: (i, 0, 0)),
                       pl.BlockSpec((nrow * wrows, NSA_DH), lambda i, pt: (i, 0))],
            scratch_shapes=[pltpu.VMEM((nrow, LANE, KV_ROW), F32)]),
        out_shape=[jax.ShapeDtypeStruct((db, NSA_HEADS, NSA_DH), F32),
                   jax.ShapeDtypeStruct((db * wrows, NSA_DH), F32)],
        compiler_params=_cparams(("arbitrary",)),
        name="nsa_sample",
    )(page_table, q8, sel_new, win_new, gates, win_rows, bsel, bwin, bcmp, bself,
      *([sel_rows] * (nrow * n_pages)), *([kvc_pool] * (nrow * n_pages)))


def _mix_kernel(oa_ref, ob_ref, ga_ref, gb_ref, x_ref, gt_ref, g_ref, woa_ref, wob_ref, wout_ref, o_ref):
    a = jnp.dot(oa_ref[0], woa_ref[...], preferred_element_type=F32)
    b = jnp.dot(ob_ref[0], wob_ref[...], preferred_element_type=F32)
    m = jax.nn.sigmoid(ga_ref[0]) * a + jax.nn.sigmoid(gb_ref[0]) * b
    mixed = _mm(m, wout_ref[...])
    o_ref[0] = x_ref[0] + gt_ref[0] * (_rms_rows(mixed) * g_ref[...])


def _mix(o_a, o_b, proj, x3, gt, g, w_oa, w_ob, w_out):
    bb, t, d = x3.shape
    tm = _pick(t, 512)
    row = lambda c: pl.BlockSpec((1, tm, d), lambda b, i: (b, i, c))
    return pl.pallas_call(
        _mix_kernel,
        grid=(bb, t // tm),
        in_specs=[row(0), row(0), row(C_GM // d), row(C_GM // d + 1), row(0), _mod_spec(gt, tm),
                  _resident((1, d)), _resident((d, d)), _resident((d, d)), _resident((d, d))],
        out_specs=row(0),
        out_shape=jax.ShapeDtypeStruct((bb, t, d), F32),
        compiler_params=_cparams(("parallel", "parallel")),
        name="mix",
    )(o_a, o_b, proj, proj, x3, gt, g.reshape(1, d), w_oa, w_ob, w_out)


FFN_SPLIT = 2


def _ffn_kernel(x_ref, g2_ref, sc_ref, sh_ref, gt_ref, g3_ref, wg_ref, wu_ref, wd_ref, o_ref):
    x = x_ref[0]
    h = _modulated_norm(x, g2_ref[...], sc_ref[0], sh_ref[0])
    tf = D_FF // FFN_SPLIT
    y = None
    for f in range(FFN_SPLIT):
        cols = slice(f * tf, (f + 1) * tf)
        gate = jnp.dot(h, wg_ref[:, cols], preferred_element_type=F32)
        up = jnp.dot(h, wu_ref[:, cols], preferred_element_type=F32)
        part = _mm(_silu(gate) * up, wd_ref[cols, :])
        y = part if y is None else y + part
    o_ref[0] = x + gt_ref[0] * (_rms_rows(y) * g3_ref[...])


def _ffn(x3, g2, sc, sh, gt, g3, w_gu, w_down):
    bb, t, d = x3.shape
    tm = _pick(t, 512)
    assert (D_FF // FFN_SPLIT) % LANE == 0
    row = pl.BlockSpec((1, tm, d), lambda b, i: (b, i, 0))
    return pl.pallas_call(
        _ffn_kernel,
        grid=(bb, t // tm),
        in_specs=[row, _resident((1, d)), _mod_spec(sc, tm), _mod_spec(sh, tm), _mod_spec(gt, tm),
                  _resident((1, d)), _resident((d, D_FF), (0, 0)), _resident((d, D_FF), (0, 1)),
                  _resident((D_FF, d))],
        out_specs=row,
        out_shape=jax.ShapeDtypeStruct((bb, t, d), F32),
        compiler_params=_cparams(("parallel", "parallel")),
        name="ffn",
    )(x3, g2.reshape(1, d), sc, sh, gt, g3.reshape(1, d), w_gu, w_gu, w_down)


def _bucket_np(n):
    n = np.maximum(n, 0)
    exact = NUM_BUCKETS // 2
    big = exact + (np.log(np.maximum(n, 1).astype(np.float32) / np.float32(exact))
                   / np.float32(math.log(MAX_DISTANCE / exact))
                   * np.float32(NUM_BUCKETS - exact)).astype(np.int32)
    return np.where(n < exact, n, np.minimum(big, NUM_BUCKETS - 1)).astype(np.int32)


def _bias_tables(rel_bias, t, past, wb):
    def table(dist):
        idx = jnp.asarray(_bucket_np(dist))[None]
        heads = rel_bias.shape[1]
        out = jnp.zeros((heads,) + idx.shape[1:], rel_bias.dtype)
        for bucket in range(NUM_BUCKETS):
            val = rel_bias[bucket].reshape((heads,) + (1,) * (idx.ndim - 1))
            out = jnp.where(idx == bucket, val, out)
        return out

    n = t // NSA_BLK
    ends = np.arange(n) * NSA_BLK + (NSA_BLK - 1)
    cmp_p = table(np.arange(t)[None, :] - ends[:, None])
    ar = np.arange(TQ)
    tiles = table(np.stack([d * TQ + ar[:, None] - ar[None, :] for d in range(3)]))
    sel_s = table(past - np.arange(past))
    win_s = table(wb - np.arange(wb))
    n_s = past // NSA_BLK
    cmp_s = table(np.pad(past - (np.arange(n_s) * NSA_BLK + NSA_BLK - 1), (0, LANE - n_s)))
    self_s = table(np.zeros((LANE,), np.int64))
    return cmp_p, tiles, sel_s, win_s, cmp_s, self_s


def _regroup_w_in(w):
    o = np.cumsum((0, CONV_DIM, DN_HEADS * DN_DV, DN_HEADS, DN_HEADS, NSA_HEADS * NSA_DH,
                   6 * NSA_KV * NSA_DH, 3 * NSA_HEADS, 2 * D_MODEL))
    qkv_z, b, a, qn, kvn, gn, gm = (w[:, o[0]:o[2]], w[:, o[2]:o[3]], w[:, o[3]:o[4]], w[:, o[4]:o[5]],
                                    w[:, o[5]:o[6]], w[:, o[6]:o[7]], w[:, o[7]:o[8]])
    pad = jnp.zeros((w.shape[0], N_PROJ - C_SM - 5 * NSA_HEADS), w.dtype)
    main = jnp.concatenate([qkv_z, qn, gm], axis=1).astype(BF16)
    kv_small = jnp.concatenate([kvn, b, a, gn, pad], axis=1).astype(BF16)
    return main, kv_small


def _layer_params(l, w_in, conv_w, a_log, dt_bias, onorm_g, w_oa, w_ob, w_out, cmp_pe, cmp_w1, cmp_w2,
                  w_gu, w_down):
    hp = jnp.zeros((SUBLANE, LANE), F32)
    hp = hp.at[0, SM_A:SM_A + DN_HEADS].set(a_log[l]).at[1, SM_A:SM_A + DN_HEADS].set(dt_bias[l])
    w_main, w_kv = _regroup_w_in(w_in[l])
    return dict(
        w_in=w_main, w_kv=w_kv, cw=conv_w[l], hp=hp, on=onorm_g[l].reshape(1, DN_DV),
        w_oa=w_oa[l].astype(BF16), w_ob=w_ob[l].astype(BF16), w_out=w_out[l].astype(BF16),
        pe=jnp.reshape(cmp_pe[l], (1, NSA_BLK * 2 * NSA_DH)),
        w1=jnp.transpose(cmp_w1[l], (1, 0, 2, 3)).reshape(NSA_BLK * 2 * NSA_DH, NSA_CMP_HID).astype(BF16),
        w2=cmp_w2[l].reshape(2 * NSA_CMP_HID, NSA_DH).astype(BF16),
        w_gu=w_gu[l].astype(BF16), w_down=w_down[l].astype(BF16))


def _mods(mod, per_token):
    parts = [mod[:, i * D_MODEL:(i + 1) * D_MODEL] for i in range(6)]
    return [p[None, :, :] if per_token else p[:, None, :] for p in parts]


def _tail(x3, o_a, o_b, proj, mods, ng, P):
    sh1, sc1, gt1, sh2, sc2, gt2 = mods
    x1 = _mix(o_a, o_b, proj, x3, gt1, ng[1], P['w_oa'], P['w_ob'], P['w_out'])
    return _ffn(x1, ng[2], sc2, sh2, gt2, ng[3], P['w_gu'], P['w_down'])


def _prompt_layer(x3, mods, ng, P, tabs):
    b, t, _ = x3.shape
    proj = _proj(x3, ng[0], mods[1], mods[0], P['w_in'])
    cmp_rows, sel_rows, win_rows, small = _kvproj(x3, ng[0], mods[1], mods[0], P['w_kv'])
    o_a, s_new = _dn_prompt(proj, small, P['cw'], P['hp'], P['on'])
    n = t // NSA_BLK
    kvc = _compress(cmp_rows, 0, b * n, P['pe'], P['w1'], P['w2']).reshape(b, n, KV_ROW)
    o_cmp, selt = _cmp_select(proj, kvc, tabs[0])
    o_b = _selwin(proj, sel_rows, win_rows, small, selt, tabs[1] * LOG2E, o_cmp)
    y = _tail(x3, o_a, o_b, proj, mods, ng, P)
    conv_new = proj[:, t - (CONV_W - 1):, C_QKV:C_QKV + CONV_DIM]
    as_cache = lambda rows: rows.reshape(b, t, 2, NSA_KV, NSA_DH)
    return y, (as_cache(cmp_rows), as_cache(sel_rows), as_cache(win_rows)[:, t - min(NSA_WINDOW, t):],
               s_new, conv_new)


def _sample_layer(x3, mods, ng, P, tabs, layer, page_table, cmp_blocks, sel_rows, win_rows, n_pool, wb,
                  state_all, conv_all):
    db = x3.shape[1]
    proj = _proj(x3, ng[0], mods[1], mods[0], P['w_in'])
    cmp_new, sel_new, win_new, small = _kvproj(x3, ng[0], mods[1], mods[0], P['w_kv'])
    o_a, s_new = _dn_sample(proj.reshape(db, 1, N_MAIN), small.reshape(db, 1, LANE), conv_all, state_all,
                            layer, P['cw'], P['hp'], P['on'])
    halves = PAGE_SIZE // NSA_BLK
    kvc_pool = _compress(cmp_blocks, layer * n_pool * halves, n_pool * halves, P['pe'], P['w1'], P['w2'])
    q8 = proj[0, :, C_QN:C_QN + NSA_HEADS * NSA_DH].reshape(db, NSA_HEADS, NSA_DH)
    gates = jnp.transpose(small[0, :, SM_GATE:SM_GATE + 3 * NSA_HEADS].reshape(db, 3, NSA_HEADS), (0, 2, 1))
    o_b, win_cache = _nsa_sample(page_table, q8, sel_new, win_new, gates, win_rows, sel_rows,
                                 kvc_pool.reshape(n_pool, halves, KV_ROW), layer, n_pool, wb,
                                 tabs[2], tabs[3], tabs[4], tabs[5])
    y = _tail(x3, o_a.reshape(1, db, D_MODEL).astype(BF16), o_b.reshape(1, db, D_MODEL).astype(BF16),
              proj, mods, ng, P)
    qkv_new = proj[0, :, C_QKV:C_QKV + CONV_DIM]
    conv_new = jnp.concatenate([conv_all[layer][:, 1:], qkv_new[:, None, :]], axis=1)
    as_cache = lambda rows, ntok: rows.reshape(db, ntok, 2, NSA_KV, NSA_DH)
    return y, (as_cache(cmp_new, 1), as_cache(sel_new, 1), as_cache(win_cache, wb), s_new, conv_new)


def kernel(x_prompt, x_sample, cache_cmp_kv, cache_sel_kv, cache_win_kv, state_delta, state_conv, page_table, c_prompt, c_sample, w_ada, b_ada, norm_g, w_in, conv_w, a_log, dt_bias, onorm_g, w_oa, w_ob, w_out, cmp_pe, cmp_w1, cmp_w2, rel_bias, w_gu, w_down):
    depth = w_in.shape[0]
    b, t, d = x_prompt.shape
    db, ts, _ = x_sample.shape
    assert ts == 1 and d == D_MODEL and t % TQ == 0
    n_pool, n_pages = cache_cmp_kv.shape[1], page_table.shape[1]
    assert cache_cmp_kv.shape[2] == PAGE_SIZE
    past = n_pages * PAGE_SIZE
    wb = cache_win_kv.shape[2]
    tabs = _bias_tables(rel_bias, t, past, wb)
    cmp_blocks = cache_cmp_kv.reshape(depth * n_pool * PAGE_SIZE * ROWS_PER_TOK, NSA_DH)
    sel_rows = cache_sel_kv.reshape(depth * n_pool * PAGE_SIZE * ROWS_PER_TOK, NSA_DH)
    win_rows = cache_win_kv.reshape(depth * db * wb * ROWS_PER_TOK, NSA_DH)
    c_all = jnp.concatenate([c_prompt, c_sample], axis=0)
    pad_rows = (-c_all.shape[0]) % SUBLANE
    c_all = jnp.pad(c_all, ((0, pad_rows), (0, 0)))
    yp = x_prompt
    ys = x_sample.reshape(1, db, d)
    p_st, s_st = [], []
    for l in range(depth):
        P = _layer_params(l, w_in, conv_w, a_log, dt_bias, onorm_g, w_oa, w_ob, w_out, cmp_pe, cmp_w1,
                          cmp_w2, w_gu, w_down)
        mod = _ada(c_all, w_ada[l], b_ada[l])
        yp, st = _prompt_layer(yp, _mods(mod[:b], False), norm_g[l], P, tabs)
        p_st.append(st)
        ys, st = _sample_layer(ys, _mods(mod[b:b + db], True), norm_g[l], P, tabs, l, page_table,
                               cmp_blocks, sel_rows, win_rows, n_pool, wb, state_delta, state_conv)
        s_st.append(st)
    stk = lambda states, i: jnp.stack([s[i] for s in states])
    return (yp, ys.reshape(db, 1, d),
            stk(p_st, 0), stk(p_st, 1), stk(p_st, 2), stk(p_st, 3), stk(p_st, 4),
            stk(s_st, 0), stk(s_st, 1), stk(s_st, 2), stk(s_st, 3), stk(s_st, 4))
```

```python
import functools
import math

import numpy as np
import jax
import jax.numpy as jnp
from jax import lax
from jax.experimental import pallas as pl
from jax.experimental.pallas import tpu as pltpu

F32 = jnp.float32
BF16 = jnp.bfloat16

D_MODEL = 1024
EPS = 1e-6
DN_HEADS = 8
DN_DK = 128
DN_DV = 128
CONV_W = 4
CONV_DIM = 2 * DN_HEADS * DN_DK + DN_HEADS * DN_DV
DN_CHUNK = 64
NSA_HEADS = 8
NSA_KV = 2
NSA_G = NSA_HEADS // NSA_KV
NSA_DH = 128
NSA_SCALE = NSA_DH ** -0.5
NSA_BLK = 64
NSA_TOPN = 16
NSA_WINDOW = 512
NSA_CMP_HID = 128
FORCE_SCORE = 1e4
NEG_INF = -1e30
NUM_BUCKETS = 32
MAX_DISTANCE = 128
D_FF = 256 * (-(-8 * D_MODEL // (3 * 256)))
PAGE_SIZE = 128

C_QKV = 0
C_Z = 3072
C_QN = 4096
C_GM = 5120
C_KVN = 7168
C_SM = 8704
N_PROJ = 8832
N_MAIN = C_KVN
LOG2E = math.log2(math.e)
SM_BETA = 0
SM_A = 8
SM_GATE = 16
KV_ROW = 2 * NSA_KV * NSA_DH
ROWS_PER_TOK = 2 * NSA_KV

LANE = 128
SUBLANE = 8
VMEM_LIMIT = 48 * 1024 * 1024
TQ = 128
WIN_TILES = NSA_WINDOW // TQ + 1


def _cparams(sem):
    return pltpu.CompilerParams(dimension_semantics=sem, vmem_limit_bytes=VMEM_LIMIT)


def _pick(n, pref, mult=SUBLANE):
    for t in range(min(pref, n), 0, -1):
        if n % t == 0 and t % mult == 0:
            return t
    return n


def _mm(a, b, dims=((1,), (0,))):
    return lax.dot_general(a.astype(BF16), b.astype(BF16), (dims, ((), ())),
                           preferred_element_type=F32)


NN = ((1,), (0,))
NT = ((1,), (1,))
TN = ((0,), (0,))


def _hilo(a):
    hi = a.astype(BF16)
    lo = (a - hi.astype(F32)).astype(BF16)
    return hi, lo


def _mm3(a, b, dims=NN):
    ah, al = _hilo(a)
    bh, bl = _hilo(b)
    return _mm(ah, bh, dims) + (_mm(ah, bl, dims) + _mm(al, bh, dims))


def _mm_exact_lhs(a01, b):
    b0 = b.astype(BF16)
    r1 = b - b0.astype(F32)
    b1 = r1.astype(BF16)
    b2 = (r1 - b1.astype(F32)).astype(BF16)
    return _mm(a01, b0) + (_mm(a01, b1) + _mm(a01, b2))


def _silu(x):
    return x * jax.nn.sigmoid(x)


def _softplus(x):
    return jnp.maximum(x, 0.0) + jnp.log1p(jnp.exp(-jnp.abs(x)))


def _rms_rows(x):
    return x * lax.rsqrt(jnp.mean(x * x, axis=-1, keepdims=True) + EPS)


def _ada_kernel(c_ref, w_ref, b_ref, o_ref):
    o_ref[...] = _mm(_silu(c_ref[...]), w_ref[...]) + b_ref[...]


def _ada(c_all, w, b):
    m, d = c_all.shape
    n = w.shape[1]
    tn = _pick(n, 1536, LANE)
    return pl.pallas_call(
        _ada_kernel,
        grid=(n // tn,),
        in_specs=[pl.BlockSpec((m, d), lambda j: (0, 0)),
                  pl.BlockSpec((d, tn), lambda j: (0, j)),
                  pl.BlockSpec((1, tn), lambda j: (0, j))],
        out_specs=pl.BlockSpec((m, tn), lambda j: (0, j)),
        out_shape=jax.ShapeDtypeStruct((m, n), F32),
        compiler_params=_cparams(("parallel",)),
        name="ada",
    )(c_all, w, b.reshape(1, n))


def _modulated_norm(x, g, sc, sh):
    return ((_rms_rows(x) * g) * (1.0 + sc) + sh).astype(BF16)


def _proj_kernel(x_ref, g_ref, sc_ref, sh_ref, w_ref, o_ref):
    h = _modulated_norm(x_ref[0], g_ref[...], sc_ref[0], sh_ref[0])
    o_ref[0] = jnp.dot(h, w_ref[...], preferred_element_type=F32)


def _mod_spec(mod, tm):
    if mod.shape[1] == 1:
        return pl.BlockSpec((1, 1, D_MODEL), lambda b, i: (b, 0, 0))
    return pl.BlockSpec((1, tm, D_MODEL), lambda b, i: (b, i, 0))


def _kvproj_kernel(x_ref, g_ref, sc_ref, sh_ref, w_ref, cmp_ref, sel_ref, win_ref, sm_ref):
    tm = x_ref.shape[1]
    h = _modulated_norm(x_ref[0], g_ref[...], sc_ref[0], sh_ref[0])
    res = jnp.dot(h, w_ref[...], preferred_element_type=F32)
    for br, out in enumerate((cmp_ref, sel_ref, win_ref)):
        for r in range(ROWS_PER_TOK):
            c0 = br * KV_ROW + r * NSA_DH
            out[pl.ds(r, tm, stride=ROWS_PER_TOK), :] = res[:, c0:c0 + NSA_DH]
    sm_ref[0] = res[:, 3 * KV_ROW:3 * KV_ROW + LANE]


def _kvproj(x3, g, sc, sh, w_kv):
    bb, t, d = x3.shape
    n = w_kv.shape[1]
    tm = _pick(t, 512)
    nt = t // tm
    rows = pl.BlockSpec((tm * ROWS_PER_TOK, NSA_DH), lambda b, i: (b * nt + i, 0))
    rows_shape = jax.ShapeDtypeStruct((bb * t * ROWS_PER_TOK, NSA_DH), F32)
    return pl.pallas_call(
        _kvproj_kernel,
        grid=(bb, nt),
        in_specs=[pl.BlockSpec((1, tm, d), lambda b, i: (b, i, 0)), _resident((1, d)),
                  _mod_spec(sc, tm), _mod_spec(sh, tm), _resident((d, n))],
        out_specs=[rows, rows, rows, pl.BlockSpec((1, tm, LANE), lambda b, i: (b, i, 0))],
        out_shape=[rows_shape, rows_shape, rows_shape, jax.ShapeDtypeStruct((bb, t, LANE), F32)],
        compiler_params=_cparams(("parallel", "parallel")),
        name="kvproj",
    )(x3, g.reshape(1, d), sc, sh, w_kv)


def _resident(shape, index=None):
    index = index or (0,) * len(shape)
    return pl.BlockSpec(shape, lambda *_: index, pipeline_mode=pl.Buffered(1))


def _proj(x3, g, sc, sh, w):
    bb, t, d = x3.shape
    n = w.shape[1]
    tm = _pick(t, 512)
    tn = _pick(n, 3584, LANE)
    rows = lambda spec: pl.BlockSpec(spec.block_shape, lambda j, b, i: spec.index_map(b, i))
    return pl.pallas_call(
        _proj_kernel,
        grid=(n // tn, bb, t // tm),
        in_specs=[pl.BlockSpec((1, tm, d), lambda j, b, i: (b, i, 0)),
                  pl.BlockSpec((1, d), lambda j, b, i: (0, 0)),
                  rows(_mod_spec(sc, tm)), rows(_mod_spec(sh, tm)),
                  pl.BlockSpec((d, tn), lambda j, b, i: (0, j))],
        out_specs=pl.BlockSpec((1, tm, tn), lambda j, b, i: (b, i, j)),
        out_shape=jax.ShapeDtypeStruct((bb, t, n), F32),
        compiler_params=_cparams(("parallel", "parallel", "parallel")),
        name="proj",
    )(x3, g.reshape(1, d), sc, sh, w)


def _dn_prompt_kernel(qkv_ref, z_ref, sm_ref, cw_ref, hp_ref, on_ref, o_ref, s_ref, buf_ref, st_ref):
    c = pl.program_id(1)
    C = DN_CHUNK
    hist = SUBLANE

    @pl.when(c == 0)
    def _():
        buf_ref[0:hist, :] = jnp.zeros((hist, CONV_DIM), F32)
        st_ref[...] = jnp.zeros(st_ref.shape, F32)

    buf_ref[hist:hist + C, :] = qkv_ref[0]

    def conv_act(c0):
        acc = None
        for i in range(CONV_W):
            r0 = hist - (CONV_W - 1) + i
            term = buf_ref[r0:r0 + C, c0:c0 + LANE] * cw_ref[i:i + 1, c0:c0 + LANE]
            acc = term if acc is None else acc + term
        return _silu(acc)

    sm = sm_ref[0]
    beta_all = jax.nn.sigmoid(sm)
    g_all = -jnp.exp(hp_ref[0:1, :]) * _softplus(sm + hp_ref[1:2, :])
    ri = lax.broadcasted_iota(jnp.int32, (C, C), 0)
    ci = lax.broadcasted_iota(jnp.int32, (C, C), 1)
    incl = ri >= ci
    strict = ri > ci
    tri = jnp.where(incl, 1.0, 0.0).astype(BF16)
    eye = jnp.where(ri == ci, 1.0, 0.0)
    gc_all = _mm_exact_lhs(tri, g_all)
    gc_t = jnp.transpose(jnp.concatenate([gc_all, jnp.zeros((LANE - C, LANE), F32)], axis=0))
    eg_all = jnp.exp(gc_all)

    H = range(DN_HEADS)
    q = [conv_act(h * DN_DK) for h in H]
    k = [conv_act(DN_HEADS * DN_DK + h * DN_DK) for h in H]
    v = [conv_act(2 * DN_HEADS * DN_DK + h * DN_DV) for h in H]
    q = [x * lax.rsqrt(jnp.sum(x * x, axis=-1, keepdims=True) + EPS) * (DN_DK ** -0.5) for x in q]
    k = [x * lax.rsqrt(jnp.sum(x * x, axis=-1, keepdims=True) + EPS) for x in k]
    beta = [beta_all[:, SM_BETA + h:SM_BETA + h + 1] for h in H]
    gc = [gc_all[:, SM_A + h:SM_A + h + 1] for h in H]
    eg = [eg_all[:, SM_A + h:SM_A + h + 1] for h in H]
    g_last = [gc_all[C - 1:C, SM_A + h:SM_A + h + 1] for h in H]
    decay = [jnp.exp(jnp.where(incl, gc[h] - gc_t[SM_A + h:SM_A + h + 1, 0:C], NEG_INF)) for h in H]
    kb = [k[h] * beta[h] for h in H]
    a = [jnp.where(strict, _mm(kb[h], k[h], NT) * decay[h], 0.0) for h in H]
    qk = [_mm(q[h], k[h], NT) * decay[h] for h in H]
    t = [eye - a[h] for h in H]
    p = [_mm(a[h], a[h]) for h in H]
    steps = int(math.log2(C)) - 1
    for s in range(steps):
        t = [t[h] + _mm(t[h], p[h]) for h in H]
        if s + 1 < steps:
            p = [_mm(p[h], p[h]) for h in H]
    resid = [eye - t[h] - _mm3(a[h], t[h]) for h in H]
    t = [t[h] + _mm(t[h], resid[h]) for h in H]
    u = [_mm(t[h], v[h] * beta[h]) for h in H]
    w = [_mm(t[h], kb[h] * eg[h]) for h in H]
    s_old = [st_ref[h] for h in H]
    v_new = [u[h] - _mm(w[h], s_old[h]) for h in H]
    o = [_mm(q[h] * eg[h], s_old[h]) + _mm(qk[h], v_new[h]) for h in H]
    for h in H:
        st_ref[h] = s_old[h] * jnp.exp(g_last[h]) + _mm(k[h] * jnp.exp(g_last[h] - gc[h]), v_new[h], TN)
    for h in H:
        z = z_ref[0, :, h * DN_DV:(h + 1) * DN_DV]
        o_ref[0, :, h * DN_DV:(h + 1) * DN_DV] = (_rms_rows(o[h]) * on_ref[...] * _silu(z)).astype(o_ref.dtype)

    buf_ref[0:hist, :] = buf_ref[C:C + hist, :]

    @pl.when(c == pl.num_programs(1) - 1)
    def _():
        s_ref[0] = st_ref[...]


def _dn_prompt(proj, small, cw, hp, on):
    b, t, _ = proj.shape
    C = DN_CHUNK
    assert t % C == 0
    return pl.pallas_call(
        _dn_prompt_kernel,
        grid=(b, t // C),
        in_specs=[pl.BlockSpec((1, C, CONV_DIM), lambda i, c: (i, c, C_QKV // CONV_DIM)),
                  pl.BlockSpec((1, C, 1024), lambda i, c: (i, c, C_Z // 1024)),
                  pl.BlockSpec((1, C, LANE), lambda i, c: (i, c, 0)),
                  pl.BlockSpec((CONV_W, CONV_DIM), lambda i, c: (0, 0)),
                  pl.BlockSpec((SUBLANE, LANE), lambda i, c: (0, 0)),
                  pl.BlockSpec((1, DN_DV), lambda i, c: (0, 0))],
        out_specs=[pl.BlockSpec((1, C, DN_HEADS * DN_DV), lambda i, c: (i, c, 0)),
                   pl.BlockSpec((1, DN_HEADS, DN_DK, DN_DV), lambda i, c: (i, 0, 0, 0))],
        out_shape=[jax.ShapeDtypeStruct((b, t, DN_HEADS * DN_DV), BF16),
                   jax.ShapeDtypeStruct((b, DN_HEADS, DN_DK, DN_DV), F32)],
        scratch_shapes=[pltpu.VMEM((C + SUBLANE, CONV_DIM), F32),
                        pltpu.VMEM((DN_HEADS, DN_DK, DN_DV), F32)],
        compiler_params=_cparams(("parallel", "arbitrary")),
        name="dn_prompt",
    )(proj, proj, small, cw, hp, on)


def _dn_sample_kernel(qkv_ref, z_ref, sm_ref, cb_ref, cw_ref, hp_ref, on_ref, sin_ref, o_ref, sout_ref):
    xb = cb_ref[0, 0]
    conv = qkv_ref[0] * cw_ref[CONV_W - 1:CONV_W, :]
    for i in range(CONV_W - 1):
        conv = conv + xb[i:i + 1, :] * cw_ref[i:i + 1, :]
    act = _silu(conv)
    sm = sm_ref[0]
    beta_all = jax.nn.sigmoid(sm)
    eg_all = jnp.exp(-jnp.exp(hp_ref[0:1, :]) * _softplus(sm + hp_ref[1:2, :]))
    row = lax.broadcasted_iota(jnp.int32, (SUBLANE, DN_DK), 0)
    rows = lambda x: jnp.broadcast_to(x, row.shape)
    H = range(DN_HEADS)
    q = [act[:, h * DN_DK:(h + 1) * DN_DK] for h in H]
    k = [act[:, (DN_HEADS + h) * DN_DK:(DN_HEADS + h + 1) * DN_DK] for h in H]
    v = [act[:, 2 * DN_HEADS * DN_DK + h * DN_DV:2 * DN_HEADS * DN_DK + (h + 1) * DN_DV] for h in H]
    q = [x * lax.rsqrt(jnp.sum(x * x, axis=-1, keepdims=True) + EPS) * (DN_DK ** -0.5) for x in q]
    k = [x * lax.rsqrt(jnp.sum(x * x, axis=-1, keepdims=True) + EPS) for x in k]
    beta = [beta_all[:, SM_BETA + h:SM_BETA + h + 1] for h in H]
    eg = [eg_all[:, SM_A + h:SM_A + h + 1] for h in H]
    s_old = [sin_ref[0, 0, h] for h in H]
    kq = [jnp.where(row == 0, rows(k[h]), jnp.where(row == 1, rows(q[h]), 0.0)) for h in H]
    r = [_mm3(kq[h], s_old[h]) for h in H]
    v_new = [beta[h] * (v[h] - eg[h] * r[h][0:1, :]) for h in H]
    qk = [jnp.sum(q[h] * k[h], axis=-1, keepdims=True) for h in H]
    o = [eg[h] * r[h][1:2, :] + qk[h] * v_new[h] for h in H]
    outer = [_mm3(jnp.where(row == 0, rows(k[h]), 0.0), rows(v_new[h]), TN) for h in H]
    for h in H:
        sout_ref[0, h] = s_old[h] * eg[h] + outer[h]
    for h in H:
        z = z_ref[0, :, h * DN_DV:(h + 1) * DN_DV]
        o_ref[0, :, h * DN_DV:(h + 1) * DN_DV] = _rms_rows(o[h]) * on_ref[...] * _silu(z)


def _dn_sample(proj_rows, small_rows, conv_all, state_all, layer, cw, hp, on):
    db = proj_rows.shape[0]
    return pl.pallas_call(
        _dn_sample_kernel,
        grid=(db,),
        in_specs=[pl.BlockSpec((1, 1, CONV_DIM), lambda i: (i, 0, C_QKV // CONV_DIM)),
                  pl.BlockSpec((1, 1, 1024), lambda i: (i, 0, C_Z // 1024)),
                  pl.BlockSpec((1, 1, LANE), lambda i: (i, 0, 0)),
                  pl.BlockSpec((1, 1, CONV_W - 1, CONV_DIM), lambda i: (layer, i, 0, 0)),
                  pl.BlockSpec((CONV_W, CONV_DIM), lambda i: (0, 0)),
                  pl.BlockSpec((SUBLANE, LANE), lambda i: (0, 0)),
                  pl.BlockSpec((1, DN_DV), lambda i: (0, 0)),
                  pl.BlockSpec((1, 1, DN_HEADS, DN_DK, DN_DV), lambda i: (layer, i, 0, 0, 0))],
        out_specs=[pl.BlockSpec((1, 1, DN_HEADS * DN_DV), lambda i: (i, 0, 0)),
                   pl.BlockSpec((1, DN_HEADS, DN_DK, DN_DV), lambda i: (i, 0, 0, 0))],
        out_shape=[jax.ShapeDtypeStruct((db, 1, DN_HEADS * DN_DV), F32),
                   jax.ShapeDtypeStruct((db, DN_HEADS, DN_DK, DN_DV), F32)],
        compiler_params=_cparams(("parallel",)),
        name="dn_sample",
    )(proj_rows, proj_rows, small_rows, conv_all, cw, hp, on, state_all)


BLK_ROWS = NSA_BLK * ROWS_PER_TOK


def _compress_kernel(x_ref, pe_ref, w1_ref, w2_ref, o_ref):
    tb = x_ref.shape[0] // BLK_ROWS
    G = SUBLANE
    for ck in range(ROWS_PER_TOK):
        c = ck // NSA_KV
        acc = jnp.zeros((tb, NSA_CMP_HID), F32)
        for l0 in range(0, NSA_BLK, G):
            groups = []
            for bg in range(tb // G):
                parts = [x_ref[pl.ds((bg * G + b) * BLK_ROWS + l0 * ROWS_PER_TOK + ck, G, stride=ROWS_PER_TOK), :]
                         for b in range(G)]
                groups.append(jnp.swapaxes(jnp.stack(parts, axis=0), 0, 1))
            for p in range(G):
                l = l0 + p
                x = (jnp.concatenate([grp[p] for grp in groups], axis=0)
                     + pe_ref[:, (l * 2 + c) * NSA_DH:(l * 2 + c + 1) * NSA_DH])
                acc = acc + _mm(x, w1_ref[(l * 2 + c) * NSA_DH:(l * 2 + c + 1) * NSA_DH, :])
        o_ref[:, ck * NSA_DH:(ck + 1) * NSA_DH] = _mm(
            _silu(acc), w2_ref[c * NSA_CMP_HID:(c + 1) * NSA_CMP_HID, :])


def _compress(x_rows, row0, r, pe_flat, w1_r, w2_r):
    tb = _pick(r, 128)
    assert row0 % tb == 0 and tb % SUBLANE == 0
    return pl.pallas_call(
        _compress_kernel,
        grid=(r // tb,),
        in_specs=[pl.BlockSpec((tb * BLK_ROWS, NSA_DH), lambda i: (row0 // tb + i, 0)),
                  _resident((1, NSA_BLK * 2 * NSA_DH)), _resident((NSA_BLK * 2 * NSA_DH, NSA_CMP_HID)),
                  _resident((2 * NSA_CMP_HID, NSA_DH))],
        out_specs=pl.BlockSpec((tb, KV_ROW), lambda i: (i, 0)),
        out_shape=jax.ShapeDtypeStruct((r, KV_ROW), F32),
        compiler_params=_cparams(("parallel",)),
        name="compress",
    )(x_rows, pe_flat, w1_r, w2_r)


def _cmp_select_kernel(q_ref, kvc_ref, bias_ref, o_ref, selt_ref, *, n_top):
    qi = pl.program_id(1)
    tq = q_ref.shape[1]
    n = kvc_ref.shape[1]
    qpos = qi * tq + lax.broadcasted_iota(jnp.int32, (n, tq), 1)
    jn = lax.broadcasted_iota(jnp.int32, (n, tq), 0)
    valid = qpos >= jn * NSA_BLK + (NSA_BLK - 1)
    validf = jnp.where(valid, 1.0, 0.0)
    cur = qpos // NSA_BLK
    forced = jnp.where(jn == 0, 1.0, 0.0) + jnp.where(jn == cur, 1.0, 0.0) + jnp.where(jn == cur - 1, 1.0, 0.0)
    kvc = kvc_ref[0]
    kc = [kvc[:, kv * NSA_DH:(kv + 1) * NSA_DH] for kv in range(NSA_KV)]
    vc = [kvc[:, (NSA_KV + kv) * NSA_DH:(NSA_KV + kv + 1) * NSA_DH] for kv in range(NSA_KV)]
    HH = range(NSA_HEADS)
    lg = [_mm3(kc[h // NSA_G], q_ref[0, :, h * NSA_DH:(h + 1) * NSA_DH], NT) * NSA_SCALE + bias_ref[h]
          for h in HH]
    lg = [jnp.where(valid, x, NEG_INF) for x in lg]
    e = [jnp.exp(x - jnp.max(x, axis=0, keepdims=True)) for x in lg]
    p = [x / jnp.sum(x, axis=0, keepdims=True) * validf for x in e]
    for h in HH:
        o_ref[0, :, h * NSA_DH:(h + 1) * NSA_DH] = _mm(p[h], vc[h // NSA_G], TN)
    for kv in range(NSA_KV):
        imp = p[kv * NSA_G]
        for g in range(1, NSA_G):
            imp = imp + p[kv * NSA_G + g]
        score = jnp.where(jn <= cur, jnp.where(forced > 0.5, FORCE_SCORE, imp), -1.0)
        rank = jnp.zeros((n, tq), F32)
        for i in range(n):
            si = score[i:i + 1, :]
            rank = rank + jnp.where(si > score, 1.0, jnp.where(si == score, jnp.where(jn > i, 1.0, 0.0), 0.0))
        selt_ref[0, kv] = jnp.where(rank < n_top, 1.0, 0.0)


def _cmp_select(proj, kvc, bias_cmp):
    b, t, _ = proj.shape
    n = kvc.shape[1]
    tq = TQ
    kern = functools.partial(_cmp_select_kernel, n_top=min(NSA_TOPN, n))
    return pl.pallas_call(
        kern,
        grid=(b, t // tq),
        in_specs=[pl.BlockSpec((1, tq, 1024), lambda i, j: (i, j, C_QN // 1024)),
                  pl.BlockSpec((1, n, KV_ROW), lambda i, j: (i, 0, 0)),
                  pl.BlockSpec((NSA_HEADS, n, tq), lambda i, j: (0, 0, j))],
        out_specs=[pl.BlockSpec((1, tq, NSA_HEADS * NSA_DH), lambda i, j: (i, j, 0)),
                   pl.BlockSpec((1, NSA_KV, n, tq), lambda i, j: (i, 0, 0, j))],
        out_shape=[jax.ShapeDtypeStruct((b, t, NSA_HEADS * NSA_DH), F32),
                   jax.ShapeDtypeStruct((b, NSA_KV, n, t), F32)],
        compiler_params=_cparams(("parallel", "parallel")),
        name="cmp_select",
    )(proj, kvc, bias_cmp)


def _selwin_kernel(q_ref, sel_ref, win_ref, selt_ref, bt_ref, oc_ref, sm_ref, o_ref, os_ref, *, kbs):
    qi = pl.program_id(1)
    tq = TQ
    t = sel_ref.shape[0] // ROWS_PER_TOK
    nb = selt_ref.shape[2]
    sm = sm_ref[0]
    tok_rows = lambda ref, start, which, kv, ntok: ref[pl.ds(start + which * NSA_KV + kv, ntok,
                                                             stride=ROWS_PER_TOK), :].astype(BF16)
    q4 = [jnp.concatenate([q_ref[0, :, (kv * NSA_G + g) * NSA_DH:(kv * NSA_G + g + 1) * NSA_DH]
                           for g in range(NSA_G)], axis=0) * (NSA_SCALE * LOG2E) for kv in range(NSA_KV)]
    q4 = [x.astype(BF16) for x in q4]

    def attend(kv, kv_ref, start, ntile, allowed, first_tile):
        length = ntile * tq
        k = tok_rows(kv_ref, start, 0, kv, length)
        v = tok_rows(kv_ref, start, 1, kv, length)
        s = lax.dot_general(q4[kv], k, (NT, ((), ())), preferred_element_type=F32)
        outs = []
        for g in range(NSA_G):
            h = kv * NSA_G + g
            bias = jnp.concatenate([bt_ref[h, jnp.clip(first_tile - j, 0, 2)] for j in range(ntile)], axis=1)
            sg = jnp.where(allowed, s[g * tq:(g + 1) * tq] + bias, NEG_INF)
            e = jnp.exp2(sg - jnp.max(sg, axis=-1, keepdims=True))
            den = jnp.sum(e, axis=-1, keepdims=True)
            outs.append(jnp.dot(e.astype(BF16), v, preferred_element_type=F32) / den)
        return outs

    def sel_branch(length):
        qpos = qi * tq + lax.broadcasted_iota(jnp.int32, (tq, length), 0)
        kpos = lax.broadcasted_iota(jnp.int32, (tq, length), 1)
        eb = lax.broadcasted_iota(jnp.int32, (nb, length), 0)
        ec = lax.broadcasted_iota(jnp.int32, (nb, length), 1) // NSA_BLK
        onehot = jnp.where(eb == ec, 1.0, 0.0).astype(BF16)
        for kv in range(NSA_KV):
            picked = _mm(selt_ref[0, kv], onehot, TN)
            allowed = jnp.where(kpos <= qpos, picked, 0.0) > 0.5
            outs = attend(kv, sel_ref, 0, length // tq, allowed, qi)
            for g in range(NSA_G):
                os_ref[kv * NSA_G + g] = outs[g]

    for i in range(t // kbs):
        pl.when((qi * tq) // kbs == i)(functools.partial(sel_branch, (i + 1) * kbs))

    lw = WIN_TILES * tq
    st = jnp.clip(qi - (WIN_TILES - 1), 0, t // tq - WIN_TILES)
    off = pl.multiple_of(st * tq * ROWS_PER_TOK, tq * ROWS_PER_TOK)
    dist = (qi - st) * tq + (lax.broadcasted_iota(jnp.int32, (tq, lw), 0)
                             - lax.broadcasted_iota(jnp.int32, (tq, lw), 1))
    win_allowed = jnp.where(dist >= 0, jnp.where(dist < NSA_WINDOW, 1.0, 0.0), 0.0) > 0.5
    for kv in range(NSA_KV):
        o_win = attend(kv, win_ref, off, WIN_TILES, win_allowed, qi - st)
        for g in range(NSA_G):
            h = kv * NSA_G + g
            gate = [jax.nn.sigmoid(sm[:, SM_GATE + br * NSA_HEADS + h:SM_GATE + br * NSA_HEADS + h + 1])
                    for br in range(3)]
            cols = slice(h * NSA_DH, (h + 1) * NSA_DH)
            o_ref[0, :, cols] = (gate[0] * oc_ref[0, :, cols] + gate[1] * os_ref[h]
                                 + gate[2] * o_win[g]).astype(o_ref.dtype)


def _selwin(proj, sel_rows, win_rows, small, selt, bias_tiles, o_cmp):
    b, t, _ = proj.shape
    tq = TQ
    nb = selt.shape[2]
    assert t >= WIN_TILES * tq
    kbs = math.gcd(t, 4 * tq)
    kvspec = pl.BlockSpec((t * ROWS_PER_TOK, NSA_DH), lambda i, j: (i, 0))
    return pl.pallas_call(
        functools.partial(_selwin_kernel, kbs=kbs),
        grid=(b, t // tq),
        in_specs=[pl.BlockSpec((1, tq, 1024), lambda i, j: (i, j, C_QN // 1024)),
                  kvspec, kvspec,
                  pl.BlockSpec((1, NSA_KV, nb, tq), lambda i, j: (i, 0, 0, j)),
                  _resident((NSA_HEADS, 3, tq, tq)),
                  pl.BlockSpec((1, tq, 1024), lambda i, j: (i, j, 0)),
                  pl.BlockSpec((1, tq, LANE), lambda i, j: (i, j, 0))],
        out_specs=pl.BlockSpec((1, tq, NSA_HEADS * NSA_DH), lambda i, j: (i, j, 0)),
        out_shape=jax.ShapeDtypeStruct((b, t, NSA_HEADS * NSA_DH), BF16),
        scratch_shapes=[pltpu.VMEM((NSA_HEADS, tq, NSA_DH), F32)],
        compiler_params=_cparams(("parallel", "arbitrary")),
        name="selwin",
    )(proj, sel_rows, win_rows, selt, bias_tiles, o_cmp, small)


def _nsa_sample_kernel(pt_ref, q_ref, seln_ref, winn_ref, gt_ref, win_ref, bsel_ref, bwin_ref, bcmp_ref,
                       bself_ref, *rest, n_pages, n_top):
    del pt_ref
    new_row = lambda ref, r, which, kv: ref[r * ROWS_PER_TOK + which * NSA_KV + kv:
                                            r * ROWS_PER_TOK + which * NSA_KV + kv + 1, :]
    nrow = q_ref.shape[0]
    sel_pages = rest[:nrow * n_pages]
    kvc_pages = rest[nrow * n_pages:2 * nrow * n_pages]
    o_ref, wout_ref, kvc_s = rest[2 * nrow * n_pages:2 * nrow * n_pages + 3]
    n = n_pages * (PAGE_SIZE // NSA_BLK)
    past = n_pages * PAGE_SIZE
    wrows = win_ref.shape[0] // nrow
    wb = wrows // ROWS_PER_TOK
    R = SUBLANE
    U = [(r, kv) for r in range(nrow) for kv in range(NSA_KV)]
    NU = range(len(U))
    tok_rows = lambda ref, base, which, kv, ntok: ref[pl.ds(base + which * NSA_KV + kv, ntok,
                                                            stride=ROWS_PER_TOK), :]
    page = lambda r, i: sel_pages[r * n_pages + i]

    for r in range(nrow):
        wout_ref[r * wrows:(r + 1) * wrows - ROWS_PER_TOK, :] = win_ref[r * wrows + ROWS_PER_TOK:(r + 1) * wrows, :]
        wout_ref[(r + 1) * wrows - ROWS_PER_TOK:(r + 1) * wrows, :] = winn_ref[r * ROWS_PER_TOK:
                                                                                (r + 1) * ROWS_PER_TOK, :]

    q8 = [q_ref[r] for r in range(nrow)]
    rowkv = lax.broadcasted_iota(jnp.int32, (R, 1), 0) // NSA_G
    bself = bself_ref[:, 0:1]

    for r in range(nrow):
        for i in range(n_pages):
            kvc_s[r, 2 * i:2 * i + 2, :] = kvc_pages[r * n_pages + i][0]
        kvc_s[r, n:LANE, :] = jnp.zeros((LANE - n, KV_ROW), F32)
    lane = lax.broadcasted_iota(jnp.int32, (R, LANE), 1)
    ii = lax.broadcasted_iota(jnp.int32, (LANE, LANE), 0)
    jj = lax.broadcasted_iota(jnp.int32, (LANE, LANE), 1)
    jrow = lax.broadcasted_iota(jnp.int32, (1, LANE), 1)
    eb = lax.broadcasted_iota(jnp.int32, (LANE, past), 0)
    ec = lax.broadcasted_iota(jnp.int32, (LANE, past), 1) // NSA_BLK
    expand = jnp.where(eb == ec, 1.0, 0.0).astype(BF16)
    wr = lax.broadcasted_iota(jnp.int32, (R, wb), 1)
    win_valid = (wb - wr) < NSA_WINDOW

    kc = [kvc_s[r, :, kv * NSA_DH:(kv + 1) * NSA_DH] for r, kv in U]
    vc = [kvc_s[r, :, (NSA_KV + kv) * NSA_DH:(NSA_KV + kv + 1) * NSA_DH] for r, kv in U]
    lg = [_mm3(q8[r], kc[u], NT) * NSA_SCALE + bcmp_ref[...] for u, (r, kv) in enumerate(U)]
    lg = [jnp.where(lane < n, x, NEG_INF) for x in lg]
    e = [jnp.exp(x - jnp.max(x, axis=-1, keepdims=True)) for x in lg]
    p = [jnp.where(lane < n, x / jnp.sum(x, axis=-1, keepdims=True), 0.0) for x in e]
    o_cmp = [_mm(p[u], vc[u]) for u in NU]
    imp = [jnp.sum(jnp.where(rowkv == kv, p[u], 0.0), axis=0, keepdims=True) for u, (r, kv) in enumerate(U)]
    forced = jnp.where(jrow == 0, 1.0, 0.0) + jnp.where(jrow == n, 1.0, 0.0) + jnp.where(jrow == n - 1, 1.0, 0.0)
    score = [jnp.where(jrow <= n, jnp.where(forced > 0.5, FORCE_SCORE, x), -2.0) for x in imp]
    s_b = [jnp.broadcast_to(x, (LANE, LANE)) for x in score]
    s_t = [jnp.transpose(x) for x in s_b]
    ahead = [jnp.where(s_t[u] > s_b[u], 1.0, jnp.where(s_t[u] == s_b[u], jnp.where(ii < jj, 1.0, 0.0), 0.0))
             for u in NU]
    rank = [jnp.sum(x, axis=0, keepdims=True) for x in ahead]
    sel8 = [jnp.broadcast_to(jnp.where(x < n_top, jnp.where(jrow <= n, 1.0, 0.0), 0.0), (R, LANE)) for x in rank]

    lg = [jnp.concatenate([_mm(q8[r], tok_rows(page(r, i), 0, 0, kv, PAGE_SIZE), NT) for i in range(n_pages)],
                          axis=1) * NSA_SCALE + bsel_ref[...] for r, kv in U]
    picked = [jnp.dot(x.astype(BF16), expand, preferred_element_type=F32) for x in sel8]
    lg = [jnp.where(picked[u] > 0.5, lg[u], NEG_INF) for u in NU]
    k_new = [new_row(seln_ref, r, 0, kv) for r, kv in U]
    v_new = [new_row(seln_ref, r, 1, kv) for r, kv in U]
    l_new = [jnp.sum(q8[r] * k_new[u], axis=-1, keepdims=True) * NSA_SCALE + bself for u, (r, kv) in enumerate(U)]
    l_new = [jnp.where(sel8[u][:, n:n + 1] > 0.5, l_new[u], NEG_INF) for u in NU]
    m = [jnp.maximum(jnp.max(lg[u], axis=-1, keepdims=True), l_new[u]) for u in NU]
    e = [jnp.exp(lg[u] - m[u]) for u in NU]
    e_new = [jnp.exp(l_new[u] - m[u]) for u in NU]
    den = [jnp.sum(e[u], axis=-1, keepdims=True) + e_new[u] for u in NU]
    pv = [e_new[u] * v_new[u] for u in NU]
    for i in range(n_pages):
        pv = [pv[u] + _mm(e[u][:, i * PAGE_SIZE:(i + 1) * PAGE_SIZE], tok_rows(page(r, i), 0, 1, kv, PAGE_SIZE))
              for u, (r, kv) in enumerate(U)]
    o_sel = [pv[u] / den[u] for u in NU]

    lg = [_mm(q8[r], tok_rows(win_ref, r * wrows, 0, kv, wb), NT) * NSA_SCALE + bwin_ref[...] for r, kv in U]
    lg = [jnp.where(win_valid, x, NEG_INF) for x in lg]
    k_new = [new_row(winn_ref, r, 0, kv) for r, kv in U]
    v_new = [new_row(winn_ref, r, 1, kv) for r, kv in U]
    l_new = [jnp.sum(q8[r] * k_new[u], axis=-1, keepdims=True) * NSA_SCALE + bself for u, (r, kv) in enumerate(U)]
    m = [jnp.maximum(jnp.max(lg[u], axis=-1, keepdims=True), l_new[u]) for u in NU]
    e = [jnp.exp(lg[u] - m[u]) for u in NU]
    e_new = [jnp.exp(l_new[u] - m[u]) for u in NU]
    den = [jnp.sum(e[u], axis=-1, keepdims=True) + e_new[u] for u in NU]
    o_win = [(e_new[u] * v_new[u] + _mm(e[u], tok_rows(win_ref, r * wrows, 1, kv, wb))) / den[u]
             for u, (r, kv) in enumerate(U)]

    for r in range(nrow):
        gate = jax.nn.sigmoid(gt_ref[r])
        pick = lambda outs: jnp.where(rowkv == 0, outs[r * NSA_KV], outs[r * NSA_KV + 1])
        o_ref[r] = gate[:, 0:1] * pick(o_cmp) + gate[:, 1:2] * pick(o_sel) + gate[:, 2:3] * pick(o_win)


NSA_SAMPLE_ROWS = 2


def _nsa_sample(page_table, q8, sel_new, win_new, gates, win_rows, sel_rows, kvc_pool, layer, n_pool, wb,
                bsel, bwin, bcmp, bself):
    db, n_pages = page_table.shape
    nrow = NSA_SAMPLE_ROWS
    assert db % nrow == 0
    past = n_pages * PAGE_SIZE
    n_blocks = past // NSA_BLK + 1
    kern = functools.partial(_nsa_sample_kernel, n_pages=n_pages, n_top=min(NSA_TOPN, n_blocks))
    fixed = lambda shape: pl.BlockSpec(shape, lambda i, pt: (0,) * len(shape))
    wrows = wb * ROWS_PER_TOK
    prows = PAGE_SIZE * ROWS_PER_TOK
    in_specs = [pl.BlockSpec((nrow, NSA_HEADS, NSA_DH), lambda i, pt: (i, 0, 0)),
                pl.BlockSpec((nrow * ROWS_PER_TOK, NSA_DH), lambda i, pt: (i, 0)),
                pl.BlockSpec((nrow * ROWS_PER_TOK, NSA_DH), lambda i, pt: (i, 0)),
                pl.BlockSpec((nrow, NSA_HEADS, 3), lambda i, pt: (i, 0, 0)),
                pl.BlockSpec((nrow * wrows, NSA_DH), lambda i, pt: (layer * (db // nrow) + i, 0)),
                fixed((NSA_HEADS, past)), fixed((NSA_HEADS, wb)), fixed((NSA_HEADS, LANE)),
                fixed((NSA_HEADS, LANE))]
    for r in range(nrow):
        for p in range(n_pages):
            in_specs.append(pl.BlockSpec(
                (prows, NSA_DH), lambda i, pt, r=r, p=p: (layer * n_pool + pt[i * nrow + r, p], 0)))
    for r in range(nrow):
        for p in range(n_pages):
            in_specs.append(pl.BlockSpec(
                (1, PAGE_SIZE // NSA_BLK, KV_ROW), lambda i, pt, r=r, p=p: (pt[i * nrow + r, p], 0, 0)))
    return pl.pallas_call(
        kern,
        grid_spec=pltpu.PrefetchScalarGridSpec(
            num_scalar_prefetch=1, grid=(db // nrow,), in_specs=in_specs,
            out_specs=[pl.BlockSpec((nrow, NSA_HEADS, NSA_DH), lambda i, pt: (i, 0, 0)),
                       pl.BlockSpec((nrow * wrows, NSA_DH), lambda i, pt: (i, 0))],
            scratch_shapes=[pltpu.VMEM((nrow, LANE, KV_ROW), F32)]),
        out_shape=[jax.ShapeDtypeStruct((db, NSA_HEADS, NSA_DH), F32),
                   jax.ShapeDtypeStruct((db * wrows, NSA_DH), F32)],
        compiler_params=_cparams(("arbitrary",)),
        name="nsa_sample",
    )(page_table, q8, sel_new, win_new, gates, win_rows, bsel, bwin, bcmp, bself,
      *([sel_rows] * (nrow * n_pages)), *([kvc_pool] * (nrow * n_pages)))


def _mix_kernel(oa_ref, ob_ref, ga_ref, gb_ref, x_ref, gt_ref, g_ref, woa_ref, wob_ref, wout_ref, o_ref):
    a = jnp.dot(oa_ref[0], woa_ref[...], preferred_element_type=F32)
    b = jnp.dot(ob_ref[0], wob_ref[...], preferred_element_type=F32)
    m = jax.nn.sigmoid(ga_ref[0]) * a + jax.nn.sigmoid(gb_ref[0]) * b
    mixed = _mm(m, wout_ref[...])
    o_ref[0] = x_ref[0] + gt_ref[0] * (_rms_rows(mixed) * g_ref[...])


def _mix(o_a, o_b, proj, x3, gt, g, w_oa, w_ob, w_out):
    bb, t, d = x3.shape
    tm = _pick(t, 512)
    row = lambda c: pl.BlockSpec((1, tm, d), lambda b, i: (b, i, c))
    return pl.pallas_call(
        _mix_kernel,
        grid=(bb, t // tm),
        in_specs=[row(0), row(0), row(C_GM // d), row(C_GM // d + 1), row(0), _mod_spec(gt, tm),
                  _resident((1, d)), _resident((d, d)), _resident((d, d)), _resident((d, d))],
        out_specs=row(0),
        out_shape=jax.ShapeDtypeStruct((bb, t, d), F32),
        compiler_params=_cparams(("parallel", "parallel")),
        name="mix",
    )(o_a, o_b, proj, proj, x3, gt, g.reshape(1, d), w_oa, w_ob, w_out)


FFN_SPLIT = 2


def _ffn_kernel(x_ref, g2_ref, sc_ref, sh_ref, gt_ref, g3_ref, wg_ref, wu_ref, wd_ref, o_ref):
    x = x_ref[0]
    h = _modulated_norm(x, g2_ref[...], sc_ref[0], sh_ref[0])
    tf = D_FF // FFN_SPLIT
    y = None
    for f in range(FFN_SPLIT):
        cols = slice(f * tf, (f + 1) * tf)
        gate = jnp.dot(h, wg_ref[:, cols], preferred_element_type=F32)
        up = jnp.dot(h, wu_ref[:, cols], preferred_element_type=F32)
        part = _mm(_silu(gate) * up, wd_ref[cols, :])
        y = part if y is None else y + part
    o_ref[0] = x + gt_ref[0] * (_rms_rows(y) * g3_ref[...])


def _ffn(x3, g2, sc, sh, gt, g3, w_gu, w_down):
    bb, t, d = x3.shape
    tm = _pick(t, 512)
    assert (D_FF // FFN_SPLIT) % LANE == 0
    row = pl.BlockSpec((1, tm, d), lambda b, i: (b, i, 0))
    return pl.pallas_call(
        _ffn_kernel,
        grid=(bb, t // tm),
        in_specs=[row, _resident((1, d)), _mod_spec(sc, tm), _mod_spec(sh, tm), _mod_spec(gt, tm),
                  _resident((1, d)), _resident((d, D_FF), (0, 0)), _resident((d, D_FF), (0, 1)),
                  _resident((D_FF, d))],
        out_specs=row,
        out_shape=jax.ShapeDtypeStruct((bb, t, d), F32),
        compiler_params=_cparams(("parallel", "parallel")),
        name="ffn",
    )(x3, g2.reshape(1, d), sc, sh, gt, g3.reshape(1, d), w_gu, w_gu, w_down)


def _bucket_np(n):
    n = np.maximum(n, 0)
    exact = NUM_BUCKETS // 2
    big = exact + (np.log(np.maximum(n, 1).astype(np.float32) / np.float32(exact))
                   / np.float32(math.log(MAX_DISTANCE / exact))
                   * np.float32(NUM_BUCKETS - exact)).astype(np.int32)
    return np.where(n < exact, n, np.minimum(big, NUM_BUCKETS - 1)).astype(np.int32)


def _bias_tables(rel_bias, t, past, wb):
    def table(dist):
        idx = jnp.asarray(_bucket_np(dist))[None]
        heads = rel_bias.shape[1]
        out = jnp.zeros((heads,) + idx.shape[1:], rel_bias.dtype)
        for bucket in range(NUM_BUCKETS):
            val = rel_bias[bucket].reshape((heads,) + (1,) * (idx.ndim - 1))
            out = jnp.where(idx == bucket, val, out)
        return out

    n = t // NSA_BLK
    ends = np.arange(n) * NSA_BLK + (NSA_BLK - 1)
    cmp_p = table(np.arange(t)[None, :] - ends[:, None])
    ar = np.arange(TQ)
    tiles = table(np.stack([d * TQ + ar[:, None] - ar[None, :] for d in range(3)]))
    sel_s = table(past - np.arange(past))
    win_s = table(wb - np.arange(wb))
    n_s = past // NSA_BLK
    cmp_s = table(np.pad(past - (np.arange(n_s) * NSA_BLK + NSA_BLK - 1), (0, LANE - n_s)))
    self_s = table(np.zeros((LANE,), np.int64))
    return cmp_p, tiles, sel_s, win_s, cmp_s, self_s


def _regroup_w_in(w):
    o = np.cumsum((0, CONV_DIM, DN_HEADS * DN_DV, DN_HEADS, DN_HEADS, NSA_HEADS * NSA_DH,
                   6 * NSA_KV * NSA_DH, 3 * NSA_HEADS, 2 * D_MODEL))
    qkv_z, b, a, qn, kvn, gn, gm = (w[:, o[0]:o[2]], w[:, o[2]:o[3]], w[:, o[3]:o[4]], w[:, o[4]:o[5]],
                                    w[:, o[5]:o[6]], w[:, o[6]:o[7]], w[:, o[7]:o[8]])
    pad = jnp.zeros((w.shape[0], N_PROJ - C_SM - 5 * NSA_HEADS), w.dtype)
    main = jnp.concatenate([qkv_z, qn, gm], axis=1).astype(BF16)
    kv_small = jnp.concatenate([kvn, b, a, gn, pad], axis=1).astype(BF16)
    return main, kv_small


def _layer_params(l, w_in, conv_w, a_log, dt_bias, onorm_g, w_oa, w_ob, w_out, cmp_pe, cmp_w1, cmp_w2,
                  w_gu, w_down):
    hp = jnp.zeros((SUBLANE, LANE), F32)
    hp = hp.at[0, SM_A:SM_A + DN_HEADS].set(a_log[l]).at[1, SM_A:SM_A + DN_HEADS].set(dt_bias[l])
    w_main, w_kv = _regroup_w_in(w_in[l])
    return dict(
        w_in=w_main, w_kv=w_kv, cw=conv_w[l], hp=hp, on=onorm_g[l].reshape(1, DN_DV),
        w_oa=w_oa[l].astype(BF16), w_ob=w_ob[l].astype(BF16), w_out=w_out[l].astype(BF16),
        pe=jnp.reshape(cmp_pe[l], (1, NSA_BLK * 2 * NSA_DH)),
        w1=jnp.transpose(cmp_w1[l], (1, 0, 2, 3)).reshape(NSA_BLK * 2 * NSA_DH, NSA_CMP_HID).astype(BF16),
        w2=cmp_w2[l].reshape(2 * NSA_CMP_HID, NSA_DH).astype(BF16),
        w_gu=w_gu[l].astype(BF16), w_down=w_down[l].astype(BF16))


def _mods(mod, per_token):
    parts = [mod[:, i * D_MODEL:(i + 1) * D_MODEL] for i in range(6)]
    return [p[None, :, :] if per_token else p[:, None, :] for p in parts]


def _tail(x3, o_a, o_b, proj, mods, ng, P):
    sh1, sc1, gt1, sh2, sc2, gt2 = mods
    x1 = _mix(o_a, o_b, proj, x3, gt1, ng[1], P['w_oa'], P['w_ob'], P['w_out'])
    return _ffn(x1, ng[2], sc2, sh2, gt2, ng[3], P['w_gu'], P['w_down'])


def _prompt_layer(x3, mods, ng, P, tabs):
    b, t, _ = x3.shape
    proj = _proj(x3, ng[0], mods[1], mods[0], P['w_in'])
    cmp_rows, sel_rows, win_rows, small = _kvproj(x3, ng[0], mods[1], mods[0], P['w_kv'])
    o_a, s_new = _dn_prompt(proj, small, P['cw'], P['hp'], P['on'])
    n = t // NSA_BLK
    kvc = _compress(cmp_rows, 0, b * n, P['pe'], P['w1'], P['w2']).reshape(b, n, KV_ROW)
    o_cmp, selt = _cmp_select(proj, kvc, tabs[0])
    o_b = _selwin(proj, sel_rows, win_rows, small, selt, tabs[1] * LOG2E, o_cmp)
    y = _tail(x3, o_a, o_b, proj, mods, ng, P)
    conv_new = proj[:, t - (CONV_W - 1):, C_QKV:C_QKV + CONV_DIM]
    as_cache = lambda rows: rows.reshape(b, t, 2, NSA_KV, NSA_DH)
    return y, (as_cache(cmp_rows), as_cache(sel_rows), as_cache(win_rows)[:, t - min(NSA_WINDOW, t):],
               s_new, conv_new)


def _sample_layer(x3, mods, ng, P, tabs, layer, page_table, cmp_blocks, sel_rows, win_rows, n_pool, wb,
                  state_all, conv_all):
    db = x3.shape[1]
    proj = _proj(x3, ng[0], mods[1], mods[0], P['w_in'])
    cmp_new, sel_new, win_new, small = _kvproj(x3, ng[0], mods[1], mods[0], P['w_kv'])
    o_a, s_new = _dn_sample(proj.reshape(db, 1, N_MAIN), small.reshape(db, 1, LANE), conv_all, state_all,
                            layer, P['cw'], P['hp'], P['on'])
    halves = PAGE_SIZE // NSA_BLK
    kvc_pool = _compress(cmp_blocks, layer * n_pool * halves, n_pool * halves, P['pe'], P['w1'], P['w2'])
    q8 = proj[0, :, C_QN:C_QN + NSA_HEADS * NSA_DH].reshape(db, NSA_HEADS, NSA_DH)
    gates = jnp.transpose(small[0, :, SM_GATE:SM_GATE + 3 * NSA_HEADS].reshape(db, 3, NSA_HEADS), (0, 2, 1))
    o_b, win_cache = _nsa_sample(page_table, q8, sel_new, win_new, gates, win_rows, sel_rows,
                                 kvc_pool.reshape(n_pool, halves, KV_ROW), layer, n_pool, wb,
                                 tabs[2], tabs[3], tabs[4], tabs[5])
    y = _tail(x3, o_a.reshape(1, db, D_MODEL).astype(BF16), o_b.reshape(1, db, D_MODEL).astype(BF16),
              proj, mods, ng, P)
    qkv_new = proj[0, :, C_QKV:C_QKV + CONV_DIM]
    conv_new = jnp.concatenate([conv_all[layer][:, 1:], qkv_new[:, None, :]], axis=1)
    as_cache = lambda rows, ntok: rows.reshape(db, ntok, 2, NSA_KV, NSA_DH)
    return y, (as_cache(cmp_new, 1), as_cache(sel_new, 1), as_cache(win_cache, wb), s_new, conv_new)


def kernel(x_prompt, x_sample, cache_cmp_kv, cache_sel_kv, cache_win_kv, state_delta, state_conv, page_table, c_prompt, c_sample, w_ada, b_ada, norm_g, w_in, conv_w, a_log, dt_bias, onorm_g, w_oa, w_ob, w_out, cmp_pe, cmp_w1, cmp_w2, rel_bias, w_gu, w_down):
    depth = w_in.shape[0]
    b, t, d = x_prompt.shape
    db, ts, _ = x_sample.shape
    assert ts == 1 and d == D_MODEL and t % TQ == 0
    n_pool, n_pages = cache_cmp_kv.shape[1], page_table.shape[1]
    assert cache_cmp_kv.shape[2] == PAGE_SIZE
    past = n_pages * PAGE_SIZE
    wb = cache_win_kv.shape[2]
    tabs = _bias_tables(rel_bias, t, past, wb)
    cmp_blocks = cache_cmp_kv.reshape(depth * n_pool * PAGE_SIZE * ROWS_PER_TOK, NSA_DH)
    sel_rows = cache_sel_kv.reshape(depth * n_pool * PAGE_SIZE * ROWS_PER_TOK, NSA_DH)
    win_rows = cache_win_kv.reshape(depth * db * wb * ROWS_PER_TOK, NSA_DH)
    c_all = jnp.concatenate([c_prompt, c_sample], axis=0)
    pad_rows = (-c_all.shape[0]) % SUBLANE
    c_all = jnp.pad(c_all, ((0, pad_rows), (0, 0)))
    yp = x_prompt
    ys = x_sample.reshape(1, db, d)
    p_st, s_st = [], []
    for l in range(depth):
        P = _layer_params(l, w_in, conv_w, a_log, dt_bias, onorm_g, w_oa, w_ob, w_out, cmp_pe, cmp_w1,
                          cmp_w2, w_gu, w_down)
        mod = _ada(c_all, w_ada[l], b_ada[l])
        yp, st = _prompt_layer(yp, _mods(mod[:b], False), norm_g[l], P, tabs)
        p_st.append(st)
        ys, st = _sample_layer(ys, _mods(mod[b:b + db], True), norm_g[l], P, tabs, l, page_table,
                               cmp_blocks, sel_rows, win_rows, n_pool, wb, state_delta, state_conv)
        s_st.append(st)
    stk = lambda states, i: jnp.stack([s[i] for s in states])
    return (yp, ys.reshape(db, 1, d),
            stk(p_st, 0), stk(p_st, 1), stk(p_st, 2), stk(p_st, 3), stk(p_st, 4),
            stk(s_st, 0), stk(s_st, 1), stk(s_st, 2), stk(s_st, 3), stk(s_st, 4))
```

```python
import functools
import math

import numpy as np
import jax
import jax.numpy as jnp
from jax import lax
from jax.experimental import pallas as pl
from jax.experimental.pallas import tpu as pltpu

F32 = jnp.float32
BF16 = jnp.bfloat16

D_MODEL = 1024
EPS = 1e-6
DN_HEADS = 8
DN_DK = 128
DN_DV = 128
CONV_W = 4
CONV_DIM = 2 * DN_HEADS * DN_DK + DN_HEADS * DN_DV
DN_CHUNK = 64
NSA_HEADS = 8
NSA_KV = 2
NSA_G = NSA_HEADS // NSA_KV
NSA_DH = 128
NSA_SCALE = NSA_DH ** -0.5
NSA_BLK = 64
NSA_TOPN = 16
NSA_WINDOW = 512
NSA_CMP_HID = 128
FORCE_SCORE = 1e4
NEG_INF = -1e30
NUM_BUCKETS = 32
MAX_DISTANCE = 128
D_FF = 256 * (-(-8 * D_MODEL // (3 * 256)))
PAGE_SIZE = 128

C_QKV = 0
C_Z = 3072
C_QN = 4096
C_GM = 5120
C_KVN = 7168
C_SM = 8704
N_PROJ = 8832
N_MAIN = C_KVN
LOG2E = math.log2(math.e)
SM_BETA = 0
SM_A = 8
SM_GATE = 16
KV_ROW = 2 * NSA_KV * NSA_DH
ROWS_PER_TOK = 2 * NSA_KV

LANE = 128
SUBLANE = 8
VMEM_LIMIT = 48 * 1024 * 1024
TQ = 128
WIN_TILES = NSA_WINDOW // TQ + 1


def _cparams(sem):
    return pltpu.CompilerParams(dimension_semantics=sem, vmem_limit_bytes=VMEM_LIMIT)


def _pick(n, pref, mult=SUBLANE):
    for t in range(min(pref, n), 0, -1):
        if n % t == 0 and t % mult == 0:
            return t
    return n


def _mm(a, b, dims=((1,), (0,))):
    return lax.dot_general(a.astype(BF16), b.astype(BF16), (dims, ((), ())),
                           preferred_element_type=F32)


NN = ((1,), (0,))
NT = ((1,), (1,))
TN = ((0,), (0,))


def _hilo(a):
    hi = a.astype(BF16)
    lo = (a - hi.astype(F32)).astype(BF16)
    return hi, lo


def _mm3(a, b, dims=NN):
    ah, al = _hilo(a)
    bh, bl = _hilo(b)
    return _mm(ah, bh, dims) + (_mm(ah, bl, dims) + _mm(al, bh, dims))


def _mm_exact_lhs(a01, b):
    b0 = b.astype(BF16)
    r1 = b - b0.astype(F32)
    b1 = r1.astype(BF16)
    b2 = (r1 - b1.astype(F32)).astype(BF16)
    return _mm(a01, b0) + (_mm(a01, b1) + _mm(a01, b2))


def _silu(x):
    return x * jax.nn.sigmoid(x)


def _softplus(x):
    return jnp.maximum(x, 0.0) + jnp.log1p(jnp.exp(-jnp.abs(x)))


def _rms_rows(x):
    return x * lax.rsqrt(jnp.mean(x * x, axis=-1, keepdims=True) + EPS)


def _ada_kernel(c_ref, w_ref, b_ref, o_ref):
    o_ref[...] = _mm(_silu(c_ref[...]), w_ref[...]) + b_ref[...]


def _ada(c_all, w, b):
    m, d = c_all.shape
    n = w.shape[1]
    tn = _pick(n, 1536, LANE)
    return pl.pallas_call(
        _ada_kernel,
        grid=(n // tn,),
        in_specs=[pl.BlockSpec((m, d), lambda j: (0, 0)),
                  pl.BlockSpec((d, tn), lambda j: (0, j)),
                  pl.BlockSpec((1, tn), lambda j: (0, j))],
        out_specs=pl.BlockSpec((m, tn), lambda j: (0, j)),
        out_shape=jax.ShapeDtypeStruct((m, n), F32),
        compiler_params=_cparams(("parallel",)),
        name="ada",
    )(c_all, w, b.reshape(1, n))


def _modulated_norm(x, g, sc, sh):
    return ((_rms_rows(x) * g) * (1.0 + sc) + sh).astype(BF16)


def _proj_kernel(x_ref, g_ref, sc_ref, sh_ref, w_ref, o_ref):
    h = _modulated_norm(x_ref[0], g_ref[...], sc_ref[0], sh_ref[0])
    o_ref[0] = jnp.dot(h, w_ref[...], preferred_element_type=F32)


def _mod_spec(mod, tm):
    if mod.shape[1] == 1:
        return pl.BlockSpec((1, 1, D_MODEL), lambda b, i: (b, 0, 0))
    return pl.BlockSpec((1, tm, D_MODEL), lambda b, i: (b, i, 0))


def _kvproj_kernel(x_ref, g_ref, sc_ref, sh_ref, w_ref, cmp_ref, sel_ref, win_ref, sm_ref):
    tm = x_ref.shape[1]
    h = _modulated_norm(x_ref[0], g_ref[...], sc_ref[0], sh_ref[0])
    res = jnp.dot(h, w_ref[...], preferred_element_type=F32)
    for br, out in enumerate((cmp_ref, sel_ref, win_ref)):
        for r in range(ROWS_PER_TOK):
            c0 = br * KV_ROW + r * NSA_DH
            out[pl.ds(r, tm, stride=ROWS_PER_TOK), :] = res[:, c0:c0 + NSA_DH]
    sm_ref[0] = res[:, 3 * KV_ROW:3 * KV_ROW + LANE]


def _kvproj(x3, g, sc, sh, w_kv):
    bb, t, d = x3.shape
    n = w_kv.shape[1]
    tm = _pick(t, 512)
    nt = t // tm
    rows = pl.BlockSpec((tm * ROWS_PER_TOK, NSA_DH), lambda b, i: (b * nt + i, 0))
    rows_shape = jax.ShapeDtypeStruct((bb * t * ROWS_PER_TOK, NSA_DH), F32)
    return pl.pallas_call(
        _kvproj_kernel,
        grid=(bb, nt),
        in_specs=[pl.BlockSpec((1, tm, d), lambda b, i: (b, i, 0)), _resident((1, d)),
                  _mod_spec(sc, tm), _mod_spec(sh, tm), _resident((d, n))],
        out_specs=[rows, rows, rows, pl.BlockSpec((1, tm, LANE), lambda b, i: (b, i, 0))],
        out_shape=[rows_shape, rows_shape, rows_shape, jax.ShapeDtypeStruct((bb, t, LANE), F32)],
        compiler_params=_cparams(("parallel", "parallel")),
        name="kvproj",
    )(x3, g.reshape(1, d), sc, sh, w_kv)


def _resident(shape, index=None):
    index = index or (0,) * len(shape)
    return pl.BlockSpec(shape, lambda *_: index, pipeline_mode=pl.Buffered(1))


def _proj(x3, g, sc, sh, w):
    bb, t, d = x3.shape
    n = w.shape[1]
    tm = _pick(t, 512)
    tn = _pick(n, 3584, LANE)
    rows = lambda spec: pl.BlockSpec(spec.block_shape, lambda j, b, i: spec.index_map(b, i))
    return pl.pallas_call(
        _proj_kernel,
        grid=(n // tn, bb, t // tm),
        in_specs=[pl.BlockSpec((1, tm, d), lambda j, b, i: (b, i, 0)),
                  pl.BlockSpec((1, d), lambda j, b, i: (0, 0)),
                  rows(_mod_spec(sc, tm)), rows(_mod_spec(sh, tm)),
                  pl.BlockSpec((d, tn), lambda j, b, i: (0, j))],
        out_specs=pl.BlockSpec((1, tm, tn), lambda j, b, i: (b, i, j)),
        out_shape=jax.ShapeDtypeStruct((bb, t, n), F32),
        compiler_params=_cparams(("parallel", "parallel", "parallel")),
        name="proj",
    )(x3, g.reshape(1, d), sc, sh, w)


def _dn_prompt_kernel(qkv_ref, z_ref, sm_ref, cw_ref, hp_ref, on_ref, o_ref, s_ref, buf_ref, st_ref):
    c = pl.program_id(1)
    C = DN_CHUNK
    nrow = qkv_ref.shape[0]
    hist = SUBLANE

    ntile = CONV_DIM // LANE

    @pl.when(c == 0)
    def _():
        buf_ref[:, :, 0:hist, :] = jnp.zeros((nrow, ntile, hist, LANE), F32)
        st_ref[...] = jnp.zeros(st_ref.shape, F32)

    for r in range(nrow):
        for ct in range(ntile):
            buf_ref[r, ct, hist:hist + C, :] = qkv_ref[r, :, ct * LANE:(ct + 1) * LANE]

    def conv_act(r, c0):
        acc = None
        for i in range(CONV_W):
            r0 = hist - (CONV_W - 1) + i
            term = buf_ref[r, c0 // LANE, r0:r0 + C, :] * cw_ref[i:i + 1, c0:c0 + LANE]
            acc = term if acc is None else acc + term
        return _silu(acc)

    ri = lax.broadcasted_iota(jnp.int32, (C, C), 0)
    ci = lax.broadcasted_iota(jnp.int32, (C, C), 1)
    incl = ri >= ci
    strict = ri > ci
    tri = jnp.where(incl, 1.0, 0.0).astype(BF16)
    eye = jnp.where(ri == ci, 1.0, 0.0)
    sm = [sm_ref[r] for r in range(nrow)]
    beta_all = [jax.nn.sigmoid(x) for x in sm]
    g_all = [-jnp.exp(hp_ref[0:1, :]) * _softplus(x + hp_ref[1:2, :]) for x in sm]
    gc_all = [_mm_exact_lhs(tri, x) for x in g_all]
    gc_t = [jnp.transpose(jnp.concatenate([x, jnp.zeros((LANE - C, LANE), F32)], axis=0)) for x in gc_all]
    eg_all = [jnp.exp(x) for x in gc_all]

    U = [(r, h) for r in range(nrow) for h in range(DN_HEADS)]
    NU = range(len(U))
    q = [conv_act(r, h * DN_DK) for r, h in U]
    k = [conv_act(r, DN_HEADS * DN_DK + h * DN_DK) for r, h in U]
    v = [conv_act(r, 2 * DN_HEADS * DN_DK + h * DN_DV) for r, h in U]
    q = [x * lax.rsqrt(jnp.sum(x * x, axis=-1, keepdims=True) + EPS) * (DN_DK ** -0.5) for x in q]
    k = [x * lax.rsqrt(jnp.sum(x * x, axis=-1, keepdims=True) + EPS) for x in k]
    beta = [beta_all[r][:, SM_BETA + h:SM_BETA + h + 1] for r, h in U]
    gc = [gc_all[r][:, SM_A + h:SM_A + h + 1] for r, h in U]
    eg = [eg_all[r][:, SM_A + h:SM_A + h + 1] for r, h in U]
    g_last = [gc_all[r][C - 1:C, SM_A + h:SM_A + h + 1] for r, h in U]
    decay = [jnp.exp(jnp.where(incl, gc[u] - gc_t[r][SM_A + h:SM_A + h + 1, 0:C], NEG_INF))
             for u, (r, h) in enumerate(U)]
    kb = [k[u] * beta[u] for u in NU]
    a = [jnp.where(strict, _mm(kb[u], k[u], NT) * decay[u], 0.0) for u in NU]
    qk = [_mm(q[u], k[u], NT) * decay[u] for u in NU]
    t = [eye - a[u] for u in NU]
    p = [_mm(a[u], a[u]) for u in NU]
    steps = int(math.log2(C)) - 1
    for s in range(steps):
        t = [t[u] + _mm(t[u], p[u]) for u in NU]
        if s + 1 < steps:
            p = [_mm(p[u], p[u]) for u in NU]
    resid = [eye - t[u] - _mm3(a[u], t[u]) for u in NU]
    t = [t[u] + _mm(t[u], resid[u]) for u in NU]
    vb = [_mm(t[u], v[u] * beta[u]) for u in NU]
    w = [_mm(t[u], kb[u] * eg[u]) for u in NU]
    s_old = [st_ref[r, h] for r, h in U]
    v_new = [vb[u] - _mm(w[u], s_old[u]) for u in NU]
    o = [_mm(q[u] * eg[u], s_old[u]) + _mm(qk[u], v_new[u]) for u in NU]
    for u, (r, h) in enumerate(U):
        st_ref[r, h] = s_old[u] * jnp.exp(g_last[u]) + _mm(k[u] * jnp.exp(g_last[u] - gc[u]), v_new[u], TN)
    for u, (r, h) in enumerate(U):
        z = z_ref[r, :, h * DN_DV:(h + 1) * DN_DV]
        o_ref[r, :, h * DN_DV:(h + 1) * DN_DV] = (_rms_rows(o[u]) * on_ref[...] * _silu(z)).astype(o_ref.dtype)

    buf_ref[:, :, 0:hist, :] = buf_ref[:, :, C:C + hist, :]

    @pl.when(c == pl.num_programs(1) - 1)
    def _():
        s_ref[...] = st_ref[...]


DN_PROMPT_ROWS = 2


def _dn_prompt(proj, small, cw, hp, on):
    b, t, _ = proj.shape
    C = DN_CHUNK
    nrow = DN_PROMPT_ROWS if b % DN_PROMPT_ROWS == 0 else 1
    assert t % C == 0
    return pl.pallas_call(
        _dn_prompt_kernel,
        grid=(b // nrow, t // C),
        in_specs=[pl.BlockSpec((nrow, C, CONV_DIM), lambda i, c: (i, c, C_QKV // CONV_DIM)),
                  pl.BlockSpec((nrow, C, 1024), lambda i, c: (i, c, C_Z // 1024)),
                  pl.BlockSpec((nrow, C, LANE), lambda i, c: (i, c, 0)),
                  _resident((CONV_W, CONV_DIM)), _resident((SUBLANE, LANE)), _resident((1, DN_DV))],
        out_specs=[pl.BlockSpec((nrow, C, DN_HEADS * DN_DV), lambda i, c: (i, c, 0)),
                   pl.BlockSpec((nrow, DN_HEADS, DN_DK, DN_DV), lambda i, c: (i, 0, 0, 0))],
        out_shape=[jax.ShapeDtypeStruct((b, t, DN_HEADS * DN_DV), BF16),
                   jax.ShapeDtypeStruct((b, DN_HEADS, DN_DK, DN_DV), F32)],
        scratch_shapes=[pltpu.VMEM((nrow, CONV_DIM // LANE, C + SUBLANE, LANE), F32),
                        pltpu.VMEM((nrow, DN_HEADS, DN_DK, DN_DV), F32)],
        compiler_params=_cparams(("parallel", "arbitrary")),
        name="dn_prompt",
    )(proj, proj, small, cw, hp, on)


def _dn_sample_kernel(qkv_ref, z_ref, sm_ref, cb_ref, cw_ref, hp_ref, on_ref, sin_ref, o_ref, sout_ref):
    xb = cb_ref[0, 0]
    conv = qkv_ref[0] * cw_ref[CONV_W - 1:CONV_W, :]
    for i in range(CONV_W - 1):
        conv = conv + xb[i:i + 1, :] * cw_ref[i:i + 1, :]
    act = _silu(conv)
    sm = sm_ref[0]
    beta_all = jax.nn.sigmoid(sm)
    eg_all = jnp.exp(-jnp.exp(hp_ref[0:1, :]) * _softplus(sm + hp_ref[1:2, :]))
    row = lax.broadcasted_iota(jnp.int32, (SUBLANE, DN_DK), 0)
    rows = lambda x: jnp.broadcast_to(x, row.shape)
    H = range(DN_HEADS)
    q = [act[:, h * DN_DK:(h + 1) * DN_DK] for h in H]
    k = [act[:, (DN_HEADS + h) * DN_DK:(DN_HEADS + h + 1) * DN_DK] for h in H]
    v = [act[:, 2 * DN_HEADS * DN_DK + h * DN_DV:2 * DN_HEADS * DN_DK + (h + 1) * DN_DV] for h in H]
    q = [x * lax.rsqrt(jnp.sum(x * x, axis=-1, keepdims=True) + EPS) * (DN_DK ** -0.5) for x in q]
    k = [x * lax.rsqrt(jnp.sum(x * x, axis=-1, keepdims=True) + EPS) for x in k]
    beta = [beta_all[:, SM_BETA + h:SM_BETA + h + 1] for h in H]
    eg = [eg_all[:, SM_A + h:SM_A + h + 1] for h in H]
    s_old = [sin_ref[0, 0, h] for h in H]
    kq = [jnp.where(row == 0, rows(k[h]), jnp.where(row == 1, rows(q[h]), 0.0)) for h in H]
    r = [_mm3(kq[h], s_old[h]) for h in H]
    v_new = [beta[h] * (v[h] - eg[h] * r[h][0:1, :]) for h in H]
    qk = [jnp.sum(q[h] * k[h], axis=-1, keepdims=True) for h in H]
    o = [eg[h] * r[h][1:2, :] + qk[h] * v_new[h] for h in H]
    outer = [_mm3(jnp.where(row == 0, rows(k[h]), 0.0), rows(v_new[h]), TN) for h in H]
    for h in H:
        sout_ref[0, h] = s_old[h] * eg[h] + outer[h]
    for h in H:
        z = z_ref[0, :, h * DN_DV:(h + 1) * DN_DV]
        o_ref[0, :, h * DN_DV:(h + 1) * DN_DV] = _rms_rows(o[h]) * on_ref[...] * _silu(z)


def _dn_sample(proj_rows, small_rows, conv_all, state_all, layer, cw, hp, on):
    db = proj_rows.shape[0]
    return pl.pallas_call(
        _dn_sample_kernel,
        grid=(db,),
        in_specs=[pl.BlockSpec((1, 1, CONV_DIM), lambda i: (i, 0, C_QKV // CONV_DIM)),
                  pl.BlockSpec((1, 1, 1024), lambda i: (i, 0, C_Z // 1024)),
                  pl.BlockSpec((1, 1, LANE), lambda i: (i, 0, 0)),
                  pl.BlockSpec((1, 1, CONV_W - 1, CONV_DIM), lambda i: (layer, i, 0, 0)),
                  pl.BlockSpec((CONV_W, CONV_DIM), lambda i: (0, 0)),
                  pl.BlockSpec((SUBLANE, LANE), lambda i: (0, 0)),
                  pl.BlockSpec((1, DN_DV), lambda i: (0, 0)),
                  pl.BlockSpec((1, 1, DN_HEADS, DN_DK, DN_DV), lambda i: (layer, i, 0, 0, 0))],
        out_specs=[pl.BlockSpec((1, 1, DN_HEADS * DN_DV), lambda i: (i, 0, 0)),
                   pl.BlockSpec((1, DN_HEADS, DN_DK, DN_DV), lambda i: (i, 0, 0, 0))],
        out_shape=[jax.ShapeDtypeStruct((db, 1, DN_HEADS * DN_DV), F32),
                   jax.ShapeDtypeStruct((db, DN_HEADS, DN_DK, DN_DV), F32)],
        compiler_params=_cparams(("parallel",)),
        name="dn_sample",
    )(proj_rows, proj_rows, small_rows, conv_all, cw, hp, on, state_all)


BLK_ROWS = NSA_BLK * ROWS_PER_TOK


def _compress_kernel(x_ref, pe_ref, w1_ref, w2_ref, o_ref):
    tb = x_ref.shape[0] // BLK_ROWS
    G = SUBLANE
    for ck in range(ROWS_PER_TOK):
        c = ck // NSA_KV
        acc = jnp.zeros((tb, NSA_CMP_HID), F32)
        for l0 in range(0, NSA_BLK, G):
            groups = []
            for bg in range(tb // G):
                parts = [x_ref[pl.ds((bg * G + b) * BLK_ROWS + l0 * ROWS_PER_TOK + ck, G, stride=ROWS_PER_TOK), :]
                         for b in range(G)]
                groups.append(jnp.swapaxes(jnp.stack(parts, axis=0), 0, 1))
            for p in range(G):
                l = l0 + p
                x = (jnp.concatenate([grp[p] for grp in groups], axis=0)
                     + pe_ref[:, (l * 2 + c) * NSA_DH:(l * 2 + c + 1) * NSA_DH])
                acc = acc + _mm(x, w1_ref[(l * 2 + c) * NSA_DH:(l * 2 + c + 1) * NSA_DH, :])
        o_ref[:, ck * NSA_DH:(ck + 1) * NSA_DH] = _mm(
            _silu(acc), w2_ref[c * NSA_CMP_HID:(c + 1) * NSA_CMP_HID, :])


def _compress(x_rows, row0, r, pe_flat, w1_r, w2_r):
    tb = _pick(r, 128)
    assert row0 % tb == 0 and tb % SUBLANE == 0
    return pl.pallas_call(
        _compress_kernel,
        grid=(r // tb,),
        in_specs=[pl.BlockSpec((tb * BLK_ROWS, NSA_DH), lambda i: (row0 // tb + i, 0)),
                  _resident((1, NSA_BLK * 2 * NSA_DH)), _resident((NSA_BLK * 2 * NSA_DH, NSA_CMP_HID)),
                  _resident((2 * NSA_CMP_HID, NSA_DH))],
        out_specs=pl.BlockSpec((tb, KV_ROW), lambda i: (i, 0)),
        out_shape=jax.ShapeDtypeStruct((r, KV_ROW), F32),
        compiler_params=_cparams(("parallel",)),
        name="compress",
    )(x_rows, pe_flat, w1_r, w2_r)


def _cmp_select_kernel(q_ref, kvc_ref, bias_ref, o_ref, selt_ref, *, n_top):
    qi = pl.program_id(1)
    tq = q_ref.shape[1]
    n = kvc_ref.shape[1]
    qpos = qi * tq + lax.broadcasted_iota(jnp.int32, (n, tq), 1)
    jn = lax.broadcasted_iota(jnp.int32, (n, tq), 0)
    valid = qpos >= jn * NSA_BLK + (NSA_BLK - 1)
    validf = jnp.where(valid, 1.0, 0.0)
    cur = qpos // NSA_BLK
    forced = jnp.where(jn == 0, 1.0, 0.0) + jnp.where(jn == cur, 1.0, 0.0) + jnp.where(jn == cur - 1, 1.0, 0.0)
    kvc = kvc_ref[0]
    kc = [kvc[:, kv * NSA_DH:(kv + 1) * NSA_DH] for kv in range(NSA_KV)]
    vc = [kvc[:, (NSA_KV + kv) * NSA_DH:(NSA_KV + kv + 1) * NSA_DH] for kv in range(NSA_KV)]
    HH = range(NSA_HEADS)
    lg = [_mm3(kc[h // NSA_G], q_ref[0, :, h * NSA_DH:(h + 1) * NSA_DH], NT) * NSA_SCALE + bias_ref[h]
          for h in HH]
    lg = [jnp.where(valid, x, NEG_INF) for x in lg]
    e = [jnp.exp(x - jnp.max(x, axis=0, keepdims=True)) for x in lg]
    p = [x / jnp.sum(x, axis=0, keepdims=True) * validf for x in e]
    for h in HH:
        o_ref[0, :, h * NSA_DH:(h + 1) * NSA_DH] = _mm(p[h], vc[h // NSA_G], TN)
    for kv in range(NSA_KV):
        imp = p[kv * NSA_G]
        for g in range(1, NSA_G):
            imp = imp + p[kv * NSA_G + g]
        score = jnp.where(jn <= cur, jnp.where(forced > 0.5, FORCE_SCORE, imp), -1.0)
        rank = jnp.zeros((n, tq), F32)
        for i in range(n):
            si = score[i:i + 1, :]
            rank = rank + jnp.where(si > score, 1.0, jnp.where(si == score, jnp.where(jn > i, 1.0, 0.0), 0.0))
        selt_ref[0, kv] = jnp.where(rank < n_top, 1.0, 0.0)


def _cmp_select(proj, kvc, bias_cmp):
    b, t, _ = proj.shape
    n = kvc.shape[1]
    tq = TQ
    kern = functools.partial(_cmp_select_kernel, n_top=min(NSA_TOPN, n))
    return pl.pallas_call(
        kern,
        grid=(b, t // tq),
        in_specs=[pl.BlockSpec((1, tq, 1024), lambda i, j: (i, j, C_QN // 1024)),
                  pl.BlockSpec((1, n, KV_ROW), lambda i, j: (i, 0, 0)),
                  pl.BlockSpec((NSA_HEADS, n, tq), lambda i, j: (0, 0, j))],
        out_specs=[pl.BlockSpec((1, tq, NSA_HEADS * NSA_DH), lambda i, j: (i, j, 0)),
                   pl.BlockSpec((1, NSA_KV, n, tq), lambda i, j: (i, 0, 0, j))],
        out_shape=[jax.ShapeDtypeStruct((b, t, NSA_HEADS * NSA_DH), F32),
                   jax.ShapeDtypeStruct((b, NSA_KV, n, t), F32)],
        compiler_params=_cparams(("parallel", "parallel")),
        name="cmp_select",
    )(proj, kvc, bias_cmp)


def _selwin_kernel(q_ref, sel_ref, win_ref, selt_ref, bt_ref, oc_ref, sm_ref, o_ref, os_ref, *, kbs):
    qi = pl.program_id(1)
    tq = TQ
    t = sel_ref.shape[0] // ROWS_PER_TOK
    nb = selt_ref.shape[2]
    sm = sm_ref[0]
    tok_rows = lambda ref, start, which, kv, ntok: ref[pl.ds(start + which * NSA_KV + kv, ntok,
                                                             stride=ROWS_PER_TOK), :].astype(BF16)
    q4 = [jnp.concatenate([q_ref[0, :, (kv * NSA_G + g) * NSA_DH:(kv * NSA_G + g + 1) * NSA_DH]
                           for g in range(NSA_G)], axis=0) * (NSA_SCALE * LOG2E) for kv in range(NSA_KV)]
    q4 = [x.astype(BF16) for x in q4]

    def attend(kv, kv_ref, start, ntile, allowed, first_tile):
        length = ntile * tq
        k = tok_rows(kv_ref, start, 0, kv, length)
        v = tok_rows(kv_ref, start, 1, kv, length)
        s = lax.dot_general(q4[kv], k, (NT, ((), ())), preferred_element_type=F32)
        outs = []
        for g in range(NSA_G):
            h = kv * NSA_G + g
            bias = jnp.concatenate([bt_ref[h, jnp.clip(first_tile - j, 0, 2)] for j in range(ntile)], axis=1)
            sg = jnp.where(allowed, s[g * tq:(g + 1) * tq] + bias, NEG_INF)
            e = jnp.exp2(sg - jnp.max(sg, axis=-1, keepdims=True))
            den = jnp.sum(e, axis=-1, keepdims=True)
            outs.append(jnp.dot(e.astype(BF16), v, preferred_element_type=F32) / den)
        return outs

    def sel_branch(length):
        qpos = qi * tq + lax.broadcasted_iota(jnp.int32, (tq, length), 0)
        kpos = lax.broadcasted_iota(jnp.int32, (tq, length), 1)
        eb = lax.broadcasted_iota(jnp.int32, (nb, length), 0)
        ec = lax.broadcasted_iota(jnp.int32, (nb, length), 1) // NSA_BLK
        onehot = jnp.where(eb == ec, 1.0, 0.0).astype(BF16)
        for kv in range(NSA_KV):
            picked = _mm(selt_ref[0, kv], onehot, TN)
            allowed = jnp.where(kpos <= qpos, picked, 0.0) > 0.5
            outs = attend(kv, sel_ref, 0, length // tq, allowed, qi)
            for g in range(NSA_G):
                os_ref[kv * NSA_G + g] = outs[g]

    for i in range(t // kbs):
        pl.when((qi * tq) // kbs == i)(functools.partial(sel_branch, (i + 1) * kbs))

    lw = WIN_TILES * tq
    st = jnp.clip(qi - (WIN_TILES - 1), 0, t // tq - WIN_TILES)
    off = pl.multiple_of(st * tq * ROWS_PER_TOK, tq * ROWS_PER_TOK)
    dist = (qi - st) * tq + (lax.broadcasted_iota(jnp.int32, (tq, lw), 0)
                             - lax.broadcasted_iota(jnp.int32, (tq, lw), 1))
    win_allowed = jnp.where(dist >= 0, jnp.where(dist < NSA_WINDOW, 1.0, 0.0), 0.0) > 0.5
    for kv in range(NSA_KV):
        o_win = attend(kv, win_ref, off, WIN_TILES, win_allowed, qi - st)
        for g in range(NSA_G):
            h = kv * NSA_G + g
            gate = [jax.nn.sigmoid(sm[:, SM_GATE + br * NSA_HEADS + h:SM_GATE + br * NSA_HEADS + h + 1])
                    for br in range(3)]
            cols = slice(h * NSA_DH, (h + 1) * NSA_DH)
            o_ref[0, :, cols] = (gate[0] * oc_ref[0, :, cols] + gate[1] * os_ref[h]
                                 + gate[2] * o_win[g]).astype(o_ref.dtype)


def _selwin(proj, sel_rows, win_rows, small, selt, bias_tiles, o_cmp):
    b, t, _ = proj.shape
    tq = TQ
    nb = selt.shape[2]
    assert t >= WIN_TILES * tq
    kbs = math.gcd(t, 4 * tq)
    kvspec = pl.BlockSpec((t * ROWS_PER_TOK, NSA_DH), lambda i, j: (i, 0))
    return pl.pallas_call(
        functools.partial(_selwin_kernel, kbs=kbs),
        grid=(b, t // tq),
        in_specs=[pl.BlockSpec((1, tq, 1024), lambda i, j: (i, j, C_QN // 1024)),
                  kvspec, kvspec,
                  pl.BlockSpec((1, NSA_KV, nb, tq), lambda i, j: (i, 0, 0, j)),
                  _resident((NSA_HEADS, 3, tq, tq)),
                  pl.BlockSpec((1, tq, 1024), lambda i, j: (i, j, 0)),
                  pl.BlockSpec((1, tq, LANE), lambda i, j: (i, j, 0))],
        out_specs=pl.BlockSpec((1, tq, NSA_HEADS * NSA_DH), lambda i, j: (i, j, 0)),
        out_shape=jax.ShapeDtypeStruct((b, t, NSA_HEADS * NSA_DH), BF16),
        scratch_shapes=[pltpu.VMEM((NSA_HEADS, tq, NSA_DH), F32)],
        compiler_params=_cparams(("parallel", "arbitrary")),
        name="selwin",
    )(proj, sel_rows, win_rows, selt, bias_tiles, o_cmp, small)


def _nsa_sample_kernel(pt_ref, q_ref, seln_ref, winn_ref, gt_ref, win_ref, bsel_ref, bwin_ref, bcmp_ref,
                       bself_ref, *rest, n_pages, n_top):
    del pt_ref
    new_row = lambda ref, r, which, kv: ref[r * ROWS_PER_TOK + which * NSA_KV + kv:
                                            r * ROWS_PER_TOK + which * NSA_KV + kv + 1, :]
    nrow = q_ref.shape[0]
    sel_pages = rest[:nrow * n_pages]
    kvc_pages = rest[nrow * n_pages:2 * nrow * n_pages]
    o_ref, wout_ref, kvc_s = rest[2 * nrow * n_pages:2 * nrow * n_pages + 3]
    n = n_pages * (PAGE_SIZE // NSA_BLK)
    past = n_pages * PAGE_SIZE
    wrows = win_ref.shape[0] // nrow
    wb = wrows // ROWS_PER_TOK
    R = SUBLANE
    U = [(r, kv) for r in range(nrow) for kv in range(NSA_KV)]
    NU = range(len(U))
    tok_rows = lambda ref, base, which, kv, ntok: ref[pl.ds(base + which * NSA_KV + kv, ntok,
                                                            stride=ROWS_PER_TOK), :]
    page = lambda r, i: sel_pages[r * n_pages + i]

    for r in range(nrow):
        wout_ref[r * wrows:(r + 1) * wrows - ROWS_PER_TOK, :] = win_ref[r * wrows + ROWS_PER_TOK:(r + 1) * wrows, :]
        wout_ref[(r + 1) * wrows - ROWS_PER_TOK:(r + 1) * wrows, :] = winn_ref[r * ROWS_PER_TOK:
                                                                                (r + 1) * ROWS_PER_TOK, :]

    q8 = [q_ref[r] for r in range(nrow)]
    rowkv = lax.broadcasted_iota(jnp.int32, (R, 1), 0) // NSA_G
    bself = bself_ref[:, 0:1]

    for r in range(nrow):
        for i in range(n_pages):
            kvc_s[r, 2 * i:2 * i + 2, :] = kvc_pages[r * n_pages + i][0]
        kvc_s[r, n:LANE, :] = jnp.zeros((LANE - n, KV_ROW), F32)
    lane = lax.broadcasted_iota(jnp.int32, (R, LANE), 1)
    ii = lax.broadcasted_iota(jnp.int32, (LANE, LANE), 0)
    jj = lax.broadcasted_iota(jnp.int32, (LANE, LANE), 1)
    jrow = lax.broadcasted_iota(jnp.int32, (1, LANE), 1)
    eb = lax.broadcasted_iota(jnp.int32, (LANE, past), 0)
    ec = lax.broadcasted_iota(jnp.int32, (LANE, past), 1) // NSA_BLK
    expand = jnp.where(eb == ec, 1.0, 0.0).astype(BF16)
    wr = lax.broadcasted_iota(jnp.int32, (R, wb), 1)
    win_valid = (wb - wr) < NSA_WINDOW

    kc = [kvc_s[r, :, kv * NSA_DH:(kv + 1) * NSA_DH] for r, kv in U]
    vc = [kvc_s[r, :, (NSA_KV + kv) * NSA_DH:(NSA_KV + kv + 1) * NSA_DH] for r, kv in U]
    lg = [_mm3(q8[r], kc[u], NT) * NSA_SCALE + bcmp_ref[...] for u, (r, kv) in enumerate(U)]
    lg = [jnp.where(lane < n, x, NEG_INF) for x in lg]
    e = [jnp.exp(x - jnp.max(x, axis=-1, keepdims=True)) for x in lg]
    p = [jnp.where(lane < n, x / jnp.sum(x, axis=-1, keepdims=True), 0.0) for x in e]
    o_cmp = [_mm(p[u], vc[u]) for u in NU]
    imp = [jnp.sum(jnp.where(rowkv == kv, p[u], 0.0), axis=0, keepdims=True) for u, (r, kv) in enumerate(U)]
    forced = jnp.where(jrow == 0, 1.0, 0.0) + jnp.where(jrow == n, 1.0, 0.0) + jnp.where(jrow == n - 1, 1.0, 0.0)
    score = [jnp.where(jrow <= n, jnp.where(forced > 0.5, FORCE_SCORE, x), -2.0) for x in imp]
    s_b = [jnp.broadcast_to(x, (LANE, LANE)) for x in score]
    s_t = [jnp.transpose(x) for x in s_b]
    ahead = [jnp.where(s_t[u] > s_b[u], 1.0, jnp.where(s_t[u] == s_b[u], jnp.where(ii < jj, 1.0, 0.0), 0.0))
             for u in NU]
    rank = [jnp.sum(x, axis=0, keepdims=True) for x in ahead]
    sel8 = [jnp.broadcast_to(jnp.where(x < n_top, jnp.where(jrow <= n, 1.0, 0.0), 0.0), (R, LANE)) for x in rank]

    lg = [jnp.concatenate([_mm(q8[r], tok_rows(page(r, i), 0, 0, kv, PAGE_SIZE), NT) for i in range(n_pages)],
                          axis=1) * NSA_SCALE + bsel_ref[...] for r, kv in U]
    picked = [jnp.dot(x.astype(BF16), expand, preferred_element_type=F32) for x in sel8]
    lg = [jnp.where(picked[u] > 0.5, lg[u], NEG_INF) for u in NU]
    k_new = [new_row(seln_ref, r, 0, kv) for r, kv in U]
    v_new = [new_row(seln_ref, r, 1, kv) for r, kv in U]
    l_new = [jnp.sum(q8[r] * k_new[u], axis=-1, keepdims=True) * NSA_SCALE + bself for u, (r, kv) in enumerate(U)]
    l_new = [jnp.where(sel8[u][:, n:n + 1] > 0.5, l_new[u], NEG_INF) for u in NU]
    m = [jnp.maximum(jnp.max(lg[u], axis=-1, keepdims=True), l_new[u]) for u in NU]
    e = [jnp.exp(lg[u] - m[u]) for u in NU]
    e_new = [jnp.exp(l_new[u] - m[u]) for u in NU]
    den = [jnp.sum(e[u], axis=-1, keepdims=True) + e_new[u] for u in NU]
    pv = [e_new[u] * v_new[u] for u in NU]
    for i in range(n_pages):
        pv = [pv[u] + _mm(e[u][:, i * PAGE_SIZE:(i + 1) * PAGE_SIZE], tok_rows(page(r, i), 0, 1, kv, PAGE_SIZE))
              for u, (r, kv) in enumerate(U)]
    o_sel = [pv[u] / den[u] for u in NU]

    lg = [_mm(q8[r], tok_rows(win_ref, r * wrows, 0, kv, wb), NT) * NSA_SCALE + bwin_ref[...] for r, kv in U]
    lg = [jnp.where(win_valid, x, NEG_INF) for x in lg]
    k_new = [new_row(winn_ref, r, 0, kv) for r, kv in U]
    v_new = [new_row(winn_ref, r, 1, kv) for r, kv in U]
    l_new = [jnp.sum(q8[r] * k_new[u], axis=-1, keepdims=True) * NSA_SCALE + bself for u, (r, kv) in enumerate(U)]
    m = [jnp.maximum(jnp.max(lg[u], axis=-1, keepdims=True), l_new[u]) for u in NU]
    e = [jnp.exp(lg[u] - m[u]) for u in NU]
    e_new = [jnp.exp(l_new[u] - m[u]) for u in NU]
    den = [jnp.sum(e[u], axis=-1, keepdims=True) + e_new[u] for u in NU]
    o_win = [(e_new[u] * v_new[u] + _mm(e[u], tok_rows(win_ref, r * wrows, 1, kv, wb))) / den[u]
             for u, (r, kv) in enumerate(U)]

    for r in range(nrow):
        gate = jax.nn.sigmoid(gt_ref[r])
        pick = lambda outs: jnp.where(rowkv == 0, outs[r * NSA_KV], outs[r * NSA_KV + 1])
        o_ref[r] = gate[:, 0:1] * pick(o_cmp) + gate[:, 1:2] * pick(o_sel) + gate[:, 2:3] * pick(o_win)


NSA_SAMPLE_ROWS = 2


def _nsa_sample(page_table, q8, sel_new, win_new, gates, win_rows, sel_rows, kvc_pool, layer, n_pool, wb,
                bsel, bwin, bcmp, bself):
    db, n_pages = page_table.shape
    nrow = NSA_SAMPLE_ROWS
    assert db % nrow == 0
    past = n_pages * PAGE_SIZE
    n_blocks = past // NSA_BLK + 1
    kern = functools.partial(_nsa_sample_kernel, n_pages=n_pages, n_top=min(NSA_TOPN, n_blocks))
    fixed = lambda shape: pl.BlockSpec(shape, lambda i, pt: (0,) * len(shape))
    wrows = wb * ROWS_PER_TOK
    prows = PAGE_SIZE * ROWS_PER_TOK
    in_specs = [pl.BlockSpec((nrow, NSA_HEADS, NSA_DH), lambda i, pt: (i, 0, 0)),
                pl.BlockSpec((nrow * ROWS_PER_TOK, NSA_DH), lambda i, pt: (i, 0)),
                pl.BlockSpec((nrow * ROWS_PER_TOK, NSA_DH), lambda i, pt: (i, 0)),
                pl.BlockSpec((nrow, NSA_HEADS, 3), lambda i, pt: (i, 0, 0)),
                pl.BlockSpec((nrow * wrows, NSA_DH), lambda i, pt: (layer * (db // nrow) + i, 0)),
                fixed((NSA_HEADS, past)), fixed((NSA_HEADS, wb)), fixed((NSA_HEADS, LANE)),
                fixed((NSA_HEADS, LANE))]
    for r in range(nrow):
        for p in range(n_pages):
            in_specs.append(pl.BlockSpec(
                (prows, NSA_DH), lambda i, pt, r=r, p=p: (layer * n_pool + pt[i * nrow + r, p], 0)))
    for r in range(nrow):
        for p in range(n_pages):
            in_specs.append(pl.BlockSpec(
                (1, PAGE_SIZE // NSA_BLK, KV_ROW), lambda i, pt, r=r, p=p: (pt[i * nrow + r, p], 0, 0)))
    return pl.pallas_call(
        kern,
        grid_spec=pltpu.PrefetchScalarGridSpec(
            num_scalar_prefetch=1, grid=(db // nrow,), in_specs=in_specs,
            out_specs=[pl.BlockSpec((nrow, NSA_HEADS, NSA_DH), lambda i, pt: (i, 0, 0)),
                       pl.BlockSpec((nrow * wrows, NSA_DH), lambda i, pt: (i, 0))],
            scratch_shapes=[pltpu.VMEM((nrow, LANE, KV_ROW), F32)]),
        out_shape=[jax.ShapeDtypeStruct((db, NSA_HEADS, NSA_DH), F32),
                   jax.ShapeDtypeStruct((db * wrows, NSA_DH), F32)],
        compiler_params=_cparams(("arbitrary",)),
        name="nsa_sample",
    )(page_table, q8, sel_new, win_new, gates, win_rows, bsel, bwin, bcmp, bself,
      *([sel_rows] * (nrow * n_pages)), *([kvc_pool] * (nrow * n_pages)))


def _mix_kernel(oa_ref, ob_ref, ga_ref, gb_ref, x_ref, gt_ref, g_ref, woa_ref, wob_ref, wout_ref, o_ref):
    a = jnp.dot(oa_ref[0], woa_ref[...], preferred_element_type=F32)
    b = jnp.dot(ob_ref[0], wob_ref[...], preferred_element_type=F32)
    m = jax.nn.sigmoid(ga_ref[0]) * a + jax.nn.sigmoid(gb_ref[0]) * b
    mixed = _mm(m, wout_ref[...])
    o_ref[0] = x_ref[0] + gt_ref[0] * (_rms_rows(mixed) * g_ref[...])


def _mix(o_a, o_b, proj, x3, gt, g, w_oa, w_ob, w_out):
    bb, t, d = x3.shape
    tm = _pick(t, 512)
    row = lambda c: pl.BlockSpec((1, tm, d), lambda b, i: (b, i, c))
    return pl.pallas_call(
        _mix_kernel,
        grid=(bb, t // tm),
        in_specs=[row(0), row(0), row(C_GM // d), row(C_GM // d + 1), row(0), _mod_spec(gt, tm),
                  _resident((1, d)), _resident((d, d)), _resident((d, d)), _resident((d, d))],
        out_specs=row(0),
        out_shape=jax.ShapeDtypeStruct((bb, t, d), F32),
        compiler_params=_cparams(("parallel", "parallel")),
        name="mix",
    )(o_a, o_b, proj, proj, x3, gt, g.reshape(1, d), w_oa, w_ob, w_out)


FFN_SPLIT = 2


def _ffn_kernel(x_ref, g2_ref, sc_ref, sh_ref, gt_ref, g3_ref, wg_ref, wu_ref, wd_ref, o_ref):
    x = x_ref[0]
    h = _modulated_norm(x, g2_ref[...], sc_ref[0], sh_ref[0])
    tf = D_FF // FFN_SPLIT
    y = None
    for f in range(FFN_SPLIT):
        cols = slice(f * tf, (f + 1) * tf)
        gate = jnp.dot(h, wg_ref[:, cols], preferred_element_type=F32)
        up = jnp.dot(h, wu_ref[:, cols], preferred_element_type=F32)
        part = _mm(_silu(gate) * up, wd_ref[cols, :])
        y = part if y is None else y + part
    o_ref[0] = x + gt_ref[0] * (_rms_rows(y) * g3_ref[...])


def _ffn(x3, g2, sc, sh, gt, g3, w_gu, w_down):
    bb, t, d = x3.shape
    tm = _pick(t, 512)
    assert (D_FF // FFN_SPLIT) % LANE == 0
    row = pl.BlockSpec((1, tm, d), lambda b, i: (b, i, 0))
    return pl.pallas_call(
        _ffn_kernel,
        grid=(bb, t // tm),
        in_specs=[row, _resident((1, d)), _mod_spec(sc, tm), _mod_spec(sh, tm), _mod_spec(gt, tm),
                  _resident((1, d)), _resident((d, D_FF), (0, 0)), _resident((d, D_FF), (0, 1)),
                  _resident((D_FF, d))],
        out_specs=row,
        out_shape=jax.ShapeDtypeStruct((bb, t, d), F32),
        compiler_params=_cparams(("parallel", "parallel")),
        name="ffn",
    )(x3, g2.reshape(1, d), sc, sh, gt, g3.reshape(1, d), w_gu, w_gu, w_down)


def _bucket_np(n):
    n = np.maximum(n, 0)
    exact = NUM_BUCKETS // 2
    big = exact + (np.log(np.maximum(n, 1).astype(np.float32) / np.float32(exact))
                   / np.float32(math.log(MAX_DISTANCE / exact))
                   * np.float32(NUM_BUCKETS - exact)).astype(np.int32)
    return np.where(n < exact, n, np.minimum(big, NUM_BUCKETS - 1)).astype(np.int32)


def _bias_tables(rel_bias, t, past, wb):
    def table(dist):
        idx = jnp.asarray(_bucket_np(dist))[None]
        heads = rel_bias.shape[1]
        out = jnp.zeros((heads,) + idx.shape[1:], rel_bias.dtype)
        for bucket in range(NUM_BUCKETS):
            val = rel_bias[bucket].reshape((heads,) + (1,) * (idx.ndim - 1))
            out = jnp.where(idx == bucket, val, out)
        return out

    n = t // NSA_BLK
    ends = np.arange(n) * NSA_BLK + (NSA_BLK - 1)
    cmp_p = table(np.arange(t)[None, :] - ends[:, None])
    ar = np.arange(TQ)
    tiles = table(np.stack([d * TQ + ar[:, None] - ar[None, :] for d in range(3)]))
    sel_s = table(past - np.arange(past))
    win_s = table(wb - np.arange(wb))
    n_s = past // NSA_BLK
    cmp_s = table(np.pad(past - (np.arange(n_s) * NSA_BLK + NSA_BLK - 1), (0, LANE - n_s)))
    self_s = table(np.zeros((LANE,), np.int64))
    return cmp_p, tiles, sel_s, win_s, cmp_s, self_s


def _regroup_w_in(w):
    o = np.cumsum((0, CONV_DIM, DN_HEADS * DN_DV, DN_HEADS, DN_HEADS, NSA_HEADS * NSA_DH,
                   6 * NSA_KV * NSA_DH, 3 * NSA_HEADS, 2 * D_MODEL))
    qkv_z, b, a, qn, kvn, gn, gm = (w[:, o[0]:o[2]], w[:, o[2]:o[3]], w[:, o[3]:o[4]], w[:, o[4]:o[5]],
                                    w[:, o[5]:o[6]], w[:, o[6]:o[7]], w[:, o[7]:o[8]])
    pad = jnp.zeros((w.shape[0], N_PROJ - C_SM - 5 * NSA_HEADS), w.dtype)
    main = jnp.concatenate([qkv_z, qn, gm], axis=1).astype(BF16)
    kv_small = jnp.concatenate([kvn, b, a, gn, pad], axis=1).astype(BF16)
    return main, kv_small


def _layer_params(l, w_in, conv_w, a_log, dt_bias, onorm_g, w_oa, w_ob, w_out, cmp_pe, cmp_w1, cmp_w2,
                  w_gu, w_down):
    hp = jnp.zeros((SUBLANE, LANE), F32)
    hp = hp.at[0, SM_A:SM_A + DN_HEADS].set(a_log[l]).at[1, SM_A:SM_A + DN_HEADS].set(dt_bias[l])
    w_main, w_kv = _regroup_w_in(w_in[l])
    return dict(
        w_in=w_main, w_kv=w_kv, cw=conv_w[l], hp=hp, on=onorm_g[l].reshape(1, DN_DV),
        w_oa=w_oa[l].astype(BF16), w_ob=w_ob[l].astype(BF16), w_out=w_out[l].astype(BF16),
        pe=jnp.reshape(cmp_pe[l], (1, NSA_BLK * 2 * NSA_DH)),
        w1=jnp.transpose(cmp_w1[l], (1, 0, 2, 3)).reshape(NSA_BLK * 2 * NSA_DH, NSA_CMP_HID).astype(BF16),
        w2=cmp_w2[l].reshape(2 * NSA_CMP_HID, NSA_DH).astype(BF16),
        w_gu=w_gu[l].astype(BF16), w_down=w_down[l].astype(BF16))


def _mods(mod, per_token):
    parts = [mod[:, i * D_MODEL:(i + 1) * D_MODEL] for i in range(6)]
    return [p[None, :, :] if per_token else p[:, None, :] for p in parts]


def _tail(x3, o_a, o_b, proj, mods, ng, P):
    sh1, sc1, gt1, sh2, sc2, gt2 = mods
    x1 = _mix(o_a, o_b, proj, x3, gt1, ng[1], P['w_oa'], P['w_ob'], P['w_out'])
    return _ffn(x1, ng[2], sc2, sh2, gt2, ng[3], P['w_gu'], P['w_down'])


def _prompt_layer(x3, mods, ng, P, tabs):
    b, t, _ = x3.shape
    proj = _proj(x3, ng[0], mods[1], mods[0], P['w_in'])
    cmp_rows, sel_rows, win_rows, small = _kvproj(x3, ng[0], mods[1], mods[0], P['w_kv'])
    o_a, s_new = _dn_prompt(proj, small, P['cw'], P['hp'], P['on'])
    n = t // NSA_BLK
    kvc = _compress(cmp_rows, 0, b * n, P['pe'], P['w1'], P['w2']).reshape(b, n, KV_ROW)
    o_cmp, selt = _cmp_select(proj, kvc, tabs[0])
    o_b = _selwin(proj, sel_rows, win_rows, small, selt, tabs[1] * LOG2E, o_cmp)
    y = _tail(x3, o_a, o_b, proj, mods, ng, P)
    conv_new = proj[:, t - (CONV_W - 1):, C_QKV:C_QKV + CONV_DIM]
    as_cache = lambda rows: rows.reshape(b, t, 2, NSA_KV, NSA_DH)
    return y, (as_cache(cmp_rows), as_cache(sel_rows), as_cache(win_rows)[:, t - min(NSA_WINDOW, t):],
               s_new, conv_new)


def _sample_layer(x3, mods, ng, P, tabs, layer, page_table, cmp_blocks, sel_rows, win_rows, n_pool, wb,
                  state_all, conv_all):
    db = x3.shape[1]
    proj = _proj(x3, ng[0], mods[1], mods[0], P['w_in'])
    cmp_new, sel_new, win_new, small = _kvproj(x3, ng[0], mods[1], mods[0], P['w_kv'])
    o_a, s_new = _dn_sample(proj.reshape(db, 1, N_MAIN), small.reshape(db, 1, LANE), conv_all, state_all,
                            layer, P['cw'], P['hp'], P['on'])
    halves = PAGE_SIZE // NSA_BLK
    kvc_pool = _compress(cmp_blocks, layer * n_pool * halves, n_pool * halves, P['pe'], P['w1'], P['w2'])
    q8 = proj[0, :, C_QN:C_QN + NSA_HEADS * NSA_DH].reshape(db, NSA_HEADS, NSA_DH)
    gates = jnp.transpose(small[0, :, SM_GATE:SM_GATE + 3 * NSA_HEADS].reshape(db, 3, NSA_HEADS), (0, 2, 1))
    o_b, win_cache = _nsa_sample(page_table, q8, sel_new, win_new, gates, win_rows, sel_rows,
                                 kvc_pool.reshape(n_pool, halves, KV_ROW), layer, n_pool, wb,
                                 tabs[2], tabs[3], tabs[4], tabs[5])
    y = _tail(x3, o_a.reshape(1, db, D_MODEL).astype(BF16), o_b.reshape(1, db, D_MODEL).astype(BF16),
              proj, mods, ng, P)
    qkv_new = proj[0, :, C_QKV:C_QKV + CONV_DIM]
    conv_new = jnp.concatenate([conv_all[layer][:, 1:], qkv_new[:, None, :]], axis=1)
    as_cache = lambda rows, ntok: rows.reshape(db, ntok, 2, NSA_KV, NSA_DH)
    return y, (as_cache(cmp_new, 1), as_cache(sel_new, 1), as_cache(win_cache, wb), s_new, conv_new)


def kernel(x_prompt, x_sample, cache_cmp_kv, cache_sel_kv, cache_win_kv, state_delta, state_conv, page_table, c_prompt, c_sample, w_ada, b_ada, norm_g, w_in, conv_w, a_log, dt_bias, onorm_g, w_oa, w_ob, w_out, cmp_pe, cmp_w1, cmp_w2, rel_bias, w_gu, w_down):
    depth = w_in.shape[0]
    b, t, d = x_prompt.shape
    db, ts, _ = x_sample.shape
    assert ts == 1 and d == D_MODEL and t % TQ == 0
    n_pool, n_pages = cache_cmp_kv.shape[1], page_table.shape[1]
    assert cache_cmp_kv.shape[2] == PAGE_SIZE
    past = n_pages * PAGE_SIZE
    wb = cache_win_kv.shape[2]
    tabs = _bias_tables(rel_bias, t, past, wb)
    cmp_blocks = cache_cmp_kv.reshape(depth * n_pool * PAGE_SIZE * ROWS_PER_TOK, NSA_DH)
    sel_rows = cache_sel_kv.reshape(depth * n_pool * PAGE_SIZE * ROWS_PER_TOK, NSA_DH)
    win_rows = cache_win_kv.reshape(depth * db * wb * ROWS_PER_TOK, NSA_DH)
    c_all = jnp.concatenate([c_prompt, c_sample], axis=0)
    pad_rows = (-c_all.shape[0]) % SUBLANE
    c_all = jnp.pad(c_all, ((0, pad_rows), (0, 0)))
    yp = x_prompt
    ys = x_sample.reshape(1, db, d)
    p_st, s_st = [], []
    for l in range(depth):
        P = _layer_params(l, w_in, conv_w, a_log, dt_bias, onorm_g, w_oa, w_ob, w_out, cmp_pe, cmp_w1,
                          cmp_w2, w_gu, w_down)
        mod = _ada(c_all, w_ada[l], b_ada[l])
        yp, st = _prompt_layer(yp, _mods(mod[:b], False), norm_g[l], P, tabs)
        p_st.append(st)
        ys, st = _sample_layer(ys, _mods(mod[b:b + db], True), norm_g[l], P, tabs, l, page_table,
                               cmp_blocks, sel_rows, win_rows, n_pool, wb, state_delta, state_conv)
        s_st.append(st)
    stk = lambda states, i: jnp.stack([s[i] for s in states])
    return (yp, ys.reshape(db, 1, d),
            stk(p_st, 0), stk(p_st, 1), stk(p_st, 2), stk(p_st, 3), stk(p_st, 4),
            stk(s_st, 0), stk(s_st, 1), stk(s_st, 2), stk(s_st, 3), stk(s_st, 4))
```

```python
import functools
import math

import numpy as np
import jax
import jax.numpy as jnp
from jax import lax
from jax.experimental import pallas as pl
from jax.experimental.pallas import tpu as pltpu

F32 = jnp.float32
BF16 = jnp.bfloat16

D_MODEL = 1024
EPS = 1e-6
DN_HEADS = 8
DN_DK = 128
DN_DV = 128
CONV_W = 4
CONV_DIM = 2 * DN_HEADS * DN_DK + DN_HEADS * DN_DV
DN_CHUNK = 64
NSA_HEADS = 8
NSA_KV = 2
NSA_G = NSA_HEADS // NSA_KV
NSA_DH = 128
NSA_SCALE = NSA_DH ** -0.5
NSA_BLK = 64
NSA_TOPN = 16
NSA_WINDOW = 512
NSA_CMP_HID = 128
FORCE_SCORE = 1e4
NEG_INF = -1e30
NUM_BUCKETS = 32
MAX_DISTANCE = 128
D_FF = 256 * (-(-8 * D_MODEL // (3 * 256)))
PAGE_SIZE = 128

C_QKV = 0
C_Z = 3072
C_QN = 4096
C_GM = 5120
C_KVN = 7168
C_SM = 8704
N_PROJ = 8832
N_MAIN = C_KVN
LOG2E = math.log2(math.e)
SM_BETA = 0
SM_A = 8
SM_GATE = 16
KV_ROW = 2 * NSA_KV * NSA_DH
ROWS_PER_TOK = 2 * NSA_KV

LANE = 128
SUBLANE = 8
VMEM_LIMIT = 48 * 1024 * 1024
TQ = 128
WIN_TILES = NSA_WINDOW // TQ + 1


def _cparams(sem):
    return pltpu.CompilerParams(dimension_semantics=sem, vmem_limit_bytes=VMEM_LIMIT)


def _pick(n, pref, mult=SUBLANE):
    for t in range(min(pref, n), 0, -1):
        if n % t == 0 and t % mult == 0:
            return t
    return n


def _mm(a, b, dims=((1,), (0,))):
    return lax.dot_general(a.astype(BF16), b.astype(BF16), (dims, ((), ())),
                           preferred_element_type=F32)


NN = ((1,), (0,))
NT = ((1,), (1,))
TN = ((0,), (0,))


def _hilo(a):
    hi = a.astype(BF16)
    lo = (a - hi.astype(F32)).astype(BF16)
    return hi, lo


def _mm3(a, b, dims=NN):
    ah, al = _hilo(a)
    bh, bl = _hilo(b)
    return _mm(ah, bh, dims) + (_mm(ah, bl, dims) + _mm(al, bh, dims))


def _mm_exact_lhs(a01, b):
    b0 = b.astype(BF16)
    r1 = b - b0.astype(F32)
    b1 = r1.astype(BF16)
    b2 = (r1 - b1.astype(F32)).astype(BF16)
    return _mm(a01, b0) + (_mm(a01, b1) + _mm(a01, b2))


def _silu(x):
    return x * jax.nn.sigmoid(x)


def _softplus(x):
    return jnp.maximum(x, 0.0) + jnp.log1p(jnp.exp(-jnp.abs(x)))


def _rms_rows(x):
    return x * lax.rsqrt(jnp.mean(x * x, axis=-1, keepdims=True) + EPS)


def _ada_kernel(c_ref, w_ref, b_ref, o_ref):
    o_ref[...] = _mm(_silu(c_ref[...]), w_ref[...]) + b_ref[...]


def _ada(c_all, w, b):
    m, d = c_all.shape
    n = w.shape[1]
    tn = _pick(n, 1536, LANE)
    return pl.pallas_call(
        _ada_kernel,
        grid=(n // tn,),
        in_specs=[pl.BlockSpec((m, d), lambda j: (0, 0)),
                  pl.BlockSpec((d, tn), lambda j: (0, j)),
                  pl.BlockSpec((1, tn), lambda j: (0, j))],
        out_specs=pl.BlockSpec((m, tn), lambda j: (0, j)),
        out_shape=jax.ShapeDtypeStruct((m, n), F32),
        compiler_params=_cparams(("parallel",)),
        name="ada",
    )(c_all, w, b.reshape(1, n))


def _modulated_norm(x, g, sc, sh):
    return ((_rms_rows(x) * g) * (1.0 + sc) + sh).astype(BF16)


def _proj_kernel(x_ref, g_ref, sc_ref, sh_ref, w_ref, o_ref):
    h = _modulated_norm(x_ref[0], g_ref[...], sc_ref[0], sh_ref[0])
    o_ref[0] = jnp.dot(h, w_ref[...], preferred_element_type=F32)


def _mod_spec(mod, tm):
    if mod.shape[1] == 1:
        return pl.BlockSpec((1, 1, D_MODEL), lambda b, i: (b, 0, 0))
    return pl.BlockSpec((1, tm, D_MODEL), lambda b, i: (b, i, 0))


def _kvproj_kernel(x_ref, g_ref, sc_ref, sh_ref, w_ref, cmp_ref, sel_ref, win_ref, sm_ref):
    tm = x_ref.shape[1]
    h = _modulated_norm(x_ref[0], g_ref[...], sc_ref[0], sh_ref[0])
    res = jnp.dot(h, w_ref[...], preferred_element_type=F32)
    for br, out in enumerate((cmp_ref, sel_ref, win_ref)):
        for r in range(ROWS_PER_TOK):
            c0 = br * KV_ROW + r * NSA_DH
            out[pl.ds(r, tm, stride=ROWS_PER_TOK), :] = res[:, c0:c0 + NSA_DH]
    sm_ref[0] = res[:, 3 * KV_ROW:3 * KV_ROW + LANE]


def _kvproj(x3, g, sc, sh, w_kv):
    bb, t, d = x3.shape
    n = w_kv.shape[1]
    tm = _pick(t, 512)
    nt = t // tm
    rows = pl.BlockSpec((tm * ROWS_PER_TOK, NSA_DH), lambda b, i: (b * nt + i, 0))
    rows_shape = jax.ShapeDtypeStruct((bb * t * ROWS_PER_TOK, NSA_DH), F32)
    return pl.pallas_call(
        _kvproj_kernel,
        grid=(bb, nt),
        in_specs=[pl.BlockSpec((1, tm, d), lambda b, i: (b, i, 0)), _resident((1, d)),
                  _mod_spec(sc, tm), _mod_spec(sh, tm), _resident((d, n))],
        out_specs=[rows, rows, rows, pl.BlockSpec((1, tm, LANE), lambda b, i: (b, i, 0))],
        out_shape=[rows_shape, rows_shape, rows_shape, jax.ShapeDtypeStruct((bb, t, LANE), F32)],
        compiler_params=_cparams(("parallel", "parallel")),
        name="kvproj",
    )(x3, g.reshape(1, d), sc, sh, w_kv)


def _resident(shape, index=None):
    index = index or (0,) * len(shape)
    return pl.BlockSpec(shape, lambda *_: index, pipeline_mode=pl.Buffered(1))


def _proj(x3, g, sc, sh, w):
    bb, t, d = x3.shape
    n = w.shape[1]
    tm = _pick(t, 512)
    tn = _pick(n, 3584, LANE)
    rows = lambda spec: pl.BlockSpec(spec.block_shape, lambda j, b, i: spec.index_map(b, i))
    return pl.pallas_call(
        _proj_kernel,
        grid=(n // tn, bb, t // tm),
        in_specs=[pl.BlockSpec((1, tm, d), lambda j, b, i: (b, i, 0)),
                  pl.BlockSpec((1, d), lambda j, b, i: (0, 0)),
                  rows(_mod_spec(sc, tm)), rows(_mod_spec(sh, tm)),
                  pl.BlockSpec((d, tn), lambda j, b, i: (0, j))],
        out_specs=pl.BlockSpec((1, tm, tn), lambda j, b, i: (b, i, j)),
        out_shape=jax.ShapeDtypeStruct((bb, t, n), F32),
        compiler_params=_cparams(("parallel", "parallel", "parallel")),
        name="proj",
    )(x3, g.reshape(1, d), sc, sh, w)


def _dn_prompt_kernel(qkv_ref, z_ref, sm_ref, cw_ref, hp_ref, on_ref, o_ref, s_ref, buf_ref, st_ref):
    c = pl.program_id(1)
    C = DN_CHUNK
    nrow = qkv_ref.shape[0]
    hist = SUBLANE

    ntile = CONV_DIM // LANE

    @pl.when(c == 0)
    def _():
        buf_ref[:, :, 0:hist, :] = jnp.zeros((nrow, ntile, hist, LANE), F32)
        st_ref[...] = jnp.zeros(st_ref.shape, F32)

    for r in range(nrow):
        for ct in range(ntile):
            buf_ref[r, ct, hist:hist + C, :] = qkv_ref[r, :, ct * LANE:(ct + 1) * LANE]

    def conv_act(r, c0):
        acc = None
        for i in range(CONV_W):
            r0 = hist - (CONV_W - 1) + i
            term = buf_ref[r, c0 // LANE, r0:r0 + C, :] * cw_ref[i:i + 1, c0:c0 + LANE]
            acc = term if acc is None else acc + term
        return _silu(acc)

    ri = lax.broadcasted_iota(jnp.int32, (C, C), 0)
    ci = lax.broadcasted_iota(jnp.int32, (C, C), 1)
    incl = ri >= ci
    strict = ri > ci
    tri = jnp.where(incl, 1.0, 0.0).astype(BF16)
    eye = jnp.where(ri == ci, 1.0, 0.0)
    sm = [sm_ref[r] for r in range(nrow)]
    beta_all = [jax.nn.sigmoid(x) for x in sm]
    g_all = [-jnp.exp(hp_ref[0:1, :]) * _softplus(x + hp_ref[1:2, :]) for x in sm]
    gc_all = [_mm_exact_lhs(tri, x) for x in g_all]
    gc_t = [jnp.transpose(jnp.concatenate([x, jnp.zeros((LANE - C, LANE), F32)], axis=0)) for x in gc_all]
    eg_all = [jnp.exp(x) for x in gc_all]

    U = [(r, h) for r in range(nrow) for h in range(DN_HEADS)]
    NU = range(len(U))
    q = [conv_act(r, h * DN_DK) for r, h in U]
    k = [conv_act(r, DN_HEADS * DN_DK + h * DN_DK) for r, h in U]
    v = [conv_act(r, 2 * DN_HEADS * DN_DK + h * DN_DV) for r, h in U]
    q = [x * lax.rsqrt(jnp.sum(x * x, axis=-1, keepdims=True) + EPS) * (DN_DK ** -0.5) for x in q]
    k = [x * lax.rsqrt(jnp.sum(x * x, axis=-1, keepdims=True) + EPS) for x in k]
    beta = [beta_all[r][:, SM_BETA + h:SM_BETA + h + 1] for r, h in U]
    gc = [gc_all[r][:, SM_A + h:SM_A + h + 1] for r, h in U]
    eg = [eg_all[r][:, SM_A + h:SM_A + h + 1] for r, h in U]
    g_last = [gc_all[r][C - 1:C, SM_A + h:SM_A + h + 1] for r, h in U]
    decay = [jnp.exp(jnp.where(incl, gc[u] - gc_t[r][SM_A + h:SM_A + h + 1, 0:C], NEG_INF))
             for u, (r, h) in enumerate(U)]
    kb = [k[u] * beta[u] for u in NU]
    a = [jnp.where(strict, _mm(kb[u], k[u], NT) * decay[u], 0.0) for u in NU]
    qk = [_mm(q[u], k[u], NT) * decay[u] for u in NU]
    t = [eye - a[u] for u in NU]
    p = [_mm(a[u], a[u]) for u in NU]
    steps = int(math.log2(C)) - 1
    for s in range(steps):
        t = [t[u] + _mm(t[u], p[u]) for u in NU]
        if s + 1 < steps:
            p = [_mm(p[u], p[u]) for u in NU]
    resid = [eye - t[u] - _mm3(a[u], t[u]) for u in NU]
    t = [t[u] + _mm(t[u], resid[u]) for u in NU]
    vb = [_mm(t[u], v[u] * beta[u]) for u in NU]
    w = [_mm(t[u], kb[u] * eg[u]) for u in NU]
    s_old = [st_ref[r, h] for r, h in U]
    v_new = [vb[u] - _mm(w[u], s_old[u]) for u in NU]
    o = [_mm(q[u] * eg[u], s_old[u]) + _mm(qk[u], v_new[u]) for u in NU]
    for u, (r, h) in enumerate(U):
        st_ref[r, h] = s_old[u] * jnp.exp(g_last[u]) + _mm(k[u] * jnp.exp(g_last[u] - gc[u]), v_new[u], TN)
    for u, (r, h) in enumerate(U):
        z = z_ref[r, :, h * DN_DV:(h + 1) * DN_DV]
        o_ref[r, :, h * DN_DV:(h + 1) * DN_DV] = (_rms_rows(o[u]) * on_ref[...] * _silu(z)).astype(o_ref.dtype)

    buf_ref[:, :, 0:hist, :] = buf_ref[:, :, C:C + hist, :]

    @pl.when(c == pl.num_programs(1) - 1)
    def _():
        s_ref[...] = st_ref[...]


DN_PROMPT_ROWS = 4


def _dn_prompt(proj, small, cw, hp, on):
    b, t, _ = proj.shape
    C = DN_CHUNK
    nrow = DN_PROMPT_ROWS if b % DN_PROMPT_ROWS == 0 else 1
    assert t % C == 0
    return pl.pallas_call(
        _dn_prompt_kernel,
        grid=(b // nrow, t // C),
        in_specs=[pl.BlockSpec((nrow, C, CONV_DIM), lambda i, c: (i, c, C_QKV // CONV_DIM)),
                  pl.BlockSpec((nrow, C, 1024), lambda i, c: (i, c, C_Z // 1024)),
                  pl.BlockSpec((nrow, C, LANE), lambda i, c: (i, c, 0)),
                  _resident((CONV_W, CONV_DIM)), _resident((SUBLANE, LANE)), _resident((1, DN_DV))],
        out_specs=[pl.BlockSpec((nrow, C, DN_HEADS * DN_DV), lambda i, c: (i, c, 0)),
                   pl.BlockSpec((nrow, DN_HEADS, DN_DK, DN_DV), lambda i, c: (i, 0, 0, 0))],
        out_shape=[jax.ShapeDtypeStruct((b, t, DN_HEADS * DN_DV), BF16),
                   jax.ShapeDtypeStruct((b, DN_HEADS, DN_DK, DN_DV), F32)],
        scratch_shapes=[pltpu.VMEM((nrow, CONV_DIM // LANE, C + SUBLANE, LANE), F32),
                        pltpu.VMEM((nrow, DN_HEADS, DN_DK, DN_DV), F32)],
        compiler_params=_cparams(("parallel", "arbitrary")),
        name="dn_prompt",
    )(proj, proj, small, cw, hp, on)


def _dn_sample_kernel(qkv_ref, z_ref, sm_ref, cb_ref, cw_ref, hp_ref, on_ref, sin_ref, o_ref, sout_ref):
    xb = cb_ref[0, 0]
    conv = qkv_ref[0] * cw_ref[CONV_W - 1:CONV_W, :]
    for i in range(CONV_W - 1):
        conv = conv + xb[i:i + 1, :] * cw_ref[i:i + 1, :]
    act = _silu(conv)
    sm = sm_ref[0]
    beta_all = jax.nn.sigmoid(sm)
    eg_all = jnp.exp(-jnp.exp(hp_ref[0:1, :]) * _softplus(sm + hp_ref[1:2, :]))
    row = lax.broadcasted_iota(jnp.int32, (SUBLANE, DN_DK), 0)
    rows = lambda x: jnp.broadcast_to(x, row.shape)
    H = range(DN_HEADS)
    q = [act[:, h * DN_DK:(h + 1) * DN_DK] for h in H]
    k = [act[:, (DN_HEADS + h) * DN_DK:(DN_HEADS + h + 1) * DN_DK] for h in H]
    v = [act[:, 2 * DN_HEADS * DN_DK + h * DN_DV:2 * DN_HEADS * DN_DK + (h + 1) * DN_DV] for h in H]
    q = [x * lax.rsqrt(jnp.sum(x * x, axis=-1, keepdims=True) + EPS) * (DN_DK ** -0.5) for x in q]
    k = [x * lax.rsqrt(jnp.sum(x * x, axis=-1, keepdims=True) + EPS) for x in k]
    beta = [beta_all[:, SM_BETA + h:SM_BETA + h + 1] for h in H]
    eg = [eg_all[:, SM_A + h:SM_A + h + 1] for h in H]
    s_old = [sin_ref[0, 0, h] for h in H]
    kq = [jnp.where(row == 0, rows(k[h]), jnp.where(row == 1, rows(q[h]), 0.0)) for h in H]
    r = [_mm3(kq[h], s_old[h]) for h in H]
    v_new = [beta[h] * (v[h] - eg[h] * r[h][0:1, :]) for h in H]
    qk = [jnp.sum(q[h] * k[h], axis=-1, keepdims=True) for h in H]
    o = [eg[h] * r[h][1:2, :] + qk[h] * v_new[h] for h in H]
    outer = [_mm3(jnp.where(row == 0, rows(k[h]), 0.0), rows(v_new[h]), TN) for h in H]
    for h in H:
        sout_ref[0, h] = s_old[h] * eg[h] + outer[h]
    for h in H:
        z = z_ref[0, :, h * DN_DV:(h + 1) * DN_DV]
        o_ref[0, :, h * DN_DV:(h + 1) * DN_DV] = _rms_rows(o[h]) * on_ref[...] * _silu(z)


def _dn_sample(proj_rows, small_rows, conv_all, state_all, layer, cw, hp, on):
    db = proj_rows.shape[0]
    return pl.pallas_call(
        _dn_sample_kernel,
        grid=(db,),
        in_specs=[pl.BlockSpec((1, 1, CONV_DIM), lambda i: (i, 0, C_QKV // CONV_DIM)),
                  pl.BlockSpec((1, 1, 1024), lambda i: (i, 0, C_Z // 1024)),
                  pl.BlockSpec((1, 1, LANE), lambda i: (i, 0, 0)),
                  pl.BlockSpec((1, 1, CONV_W - 1, CONV_DIM), lambda i: (layer, i, 0, 0)),
                  pl.BlockSpec((CONV_W, CONV_DIM), lambda i: (0, 0)),
                  pl.BlockSpec((SUBLANE, LANE), lambda i: (0, 0)),
                  pl.BlockSpec((1, DN_DV), lambda i: (0, 0)),
                  pl.BlockSpec((1, 1, DN_HEADS, DN_DK, DN_DV), lambda i: (layer, i, 0, 0, 0))],
        out_specs=[pl.BlockSpec((1, 1, DN_HEADS * DN_DV), lambda i: (i, 0, 0)),
                   pl.BlockSpec((1, DN_HEADS, DN_DK, DN_DV), lambda i: (i, 0, 0, 0))],
        out_shape=[jax.ShapeDtypeStruct((db, 1, DN_HEADS * DN_DV), F32),
                   jax.ShapeDtypeStruct((db, DN_HEADS, DN_DK, DN_DV), F32)],
        compiler_params=_cparams(("parallel",)),
        name="dn_sample",
    )(proj_rows, proj_rows, small_rows, conv_all, cw, hp, on, state_all)


BLK_ROWS = NSA_BLK * ROWS_PER_TOK


def _compress_kernel(x_ref, pe_ref, w1_ref, w2_ref, o_ref):
    tb = x_ref.shape[0] // BLK_ROWS
    G = SUBLANE
    for ck in range(ROWS_PER_TOK):
        c = ck // NSA_KV
        acc = jnp.zeros((tb, NSA_CMP_HID), F32)
        for l0 in range(0, NSA_BLK, G):
            groups = []
            for bg in range(tb // G):
                parts = [x_ref[pl.ds((bg * G + b) * BLK_ROWS + l0 * ROWS_PER_TOK + ck, G, stride=ROWS_PER_TOK), :]
                         for b in range(G)]
                groups.append(jnp.swapaxes(jnp.stack(parts, axis=0), 0, 1))
            for p in range(G):
                l = l0 + p
                x = (jnp.concatenate([grp[p] for grp in groups], axis=0)
                     + pe_ref[:, (l * 2 + c) * NSA_DH:(l * 2 + c + 1) * NSA_DH])
                acc = acc + _mm(x, w1_ref[(l * 2 + c) * NSA_DH:(l * 2 + c + 1) * NSA_DH, :])
        o_ref[:, ck * NSA_DH:(ck + 1) * NSA_DH] = _mm(
            _silu(acc), w2_ref[c * NSA_CMP_HID:(c + 1) * NSA_CMP_HID, :])


def _compress(x_rows, row0, r, pe_flat, w1_r, w2_r):
    tb = _pick(r, 128)
    assert row0 % tb == 0 and tb % SUBLANE == 0
    return pl.pallas_call(
        _compress_kernel,
        grid=(r // tb,),
        in_specs=[pl.BlockSpec((tb * BLK_ROWS, NSA_DH), lambda i: (row0 // tb + i, 0)),
                  _resident((1, NSA_BLK * 2 * NSA_DH)), _resident((NSA_BLK * 2 * NSA_DH, NSA_CMP_HID)),
                  _resident((2 * NSA_CMP_HID, NSA_DH))],
        out_specs=pl.BlockSpec((tb, KV_ROW), lambda i: (i, 0)),
        out_shape=jax.ShapeDtypeStruct((r, KV_ROW), F32),
        compiler_params=_cparams(("parallel",)),
        name="compress",
    )(x_rows, pe_flat, w1_r, w2_r)


def _cmp_select_kernel(q_ref, kvc_ref, bias_ref, o_ref, selt_ref, *, n_top):
    qi = pl.program_id(1)
    tq = q_ref.shape[1]
    n = kvc_ref.shape[1]
    qpos = qi * tq + lax.broadcasted_iota(jnp.int32, (n, tq), 1)
    jn = lax.broadcasted_iota(jnp.int32, (n, tq), 0)
    valid = qpos >= jn * NSA_BLK + (NSA_BLK - 1)
    validf = jnp.where(valid, 1.0, 0.0)
    cur = qpos // NSA_BLK
    forced = jnp.where(jn == 0, 1.0, 0.0) + jnp.where(jn == cur, 1.0, 0.0) + jnp.where(jn == cur - 1, 1.0, 0.0)
    kvc = kvc_ref[0]
    kc = [kvc[:, kv * NSA_DH:(kv + 1) * NSA_DH] for kv in range(NSA_KV)]
    vc = [kvc[:, (NSA_KV + kv) * NSA_DH:(NSA_KV + kv + 1) * NSA_DH] for kv in range(NSA_KV)]
    HH = range(NSA_HEADS)
    lg = [_mm3(kc[h // NSA_G], q_ref[0, :, h * NSA_DH:(h + 1) * NSA_DH], NT) * NSA_SCALE + bias_ref[h]
          for h in HH]
    lg = [jnp.where(valid, x, NEG_INF) for x in lg]
    e = [jnp.exp(x - jnp.max(x, axis=0, keepdims=True)) for x in lg]
    p = [x / jnp.sum(x, axis=0, keepdims=True) * validf for x in e]
    for h in HH:
        o_ref[0, :, h * NSA_DH:(h + 1) * NSA_DH] = _mm(p[h], vc[h // NSA_G], TN)
    for kv in range(NSA_KV):
        imp = p[kv * NSA_G]
        for g in range(1, NSA_G):
            imp = imp + p[kv * NSA_G + g]
        score = jnp.where(jn <= cur, jnp.where(forced > 0.5, FORCE_SCORE, imp), -1.0)
        rank = jnp.zeros((n, tq), F32)
        for i in range(n):
            si = score[i:i + 1, :]
            rank = rank + jnp.where(si > score, 1.0, jnp.where(si == score, jnp.where(jn > i, 1.0, 0.0), 0.0))
        selt_ref[0, kv] = jnp.where(rank < n_top, 1.0, 0.0)


def _cmp_select(proj, kvc, bias_cmp):
    b, t, _ = proj.shape
    n = kvc.shape[1]
    tq = TQ
    kern = functools.partial(_cmp_select_kernel, n_top=min(NSA_TOPN, n))
    return pl.pallas_call(
        kern,
        grid=(b, t // tq),
        in_specs=[pl.BlockSpec((1, tq, 1024), lambda i, j: (i, j, C_QN // 1024)),
                  pl.BlockSpec((1, n, KV_ROW), lambda i, j: (i, 0, 0)),
                  pl.BlockSpec((NSA_HEADS, n, tq), lambda i, j: (0, 0, j))],
        out_specs=[pl.BlockSpec((1, tq, NSA_HEADS * NSA_DH), lambda i, j: (i, j, 0)),
                   pl.BlockSpec((1, NSA_KV, n, tq), lambda i, j: (i, 0, 0, j))],
        out_shape=[jax.ShapeDtypeStruct((b, t, NSA_HEADS * NSA_DH), F32),
                   jax.ShapeDtypeStruct((b, NSA_KV, n, t), F32)],
        compiler_params=_cparams(("parallel", "parallel")),
        name="cmp_select",
    )(proj, kvc, bias_cmp)


def _selwin_kernel(q_ref, sel_ref, win_ref, selt_ref, bt_ref, oc_ref, sm_ref, o_ref, os_ref, *, kbs):
    qi = pl.program_id(1)
    tq = TQ
    t = sel_ref.shape[0] // ROWS_PER_TOK
    nb = selt_ref.shape[2]
    sm = sm_ref[0]
    tok_rows = lambda ref, start, which, kv, ntok: ref[pl.ds(start + which * NSA_KV + kv, ntok,
                                                             stride=ROWS_PER_TOK), :].astype(BF16)
    q4 = [jnp.concatenate([q_ref[0, :, (kv * NSA_G + g) * NSA_DH:(kv * NSA_G + g + 1) * NSA_DH]
                           for g in range(NSA_G)], axis=0) * (NSA_SCALE * LOG2E) for kv in range(NSA_KV)]
    q4 = [x.astype(BF16) for x in q4]

    def attend(kv, kv_ref, start, ntile, allowed, first_tile):
        length = ntile * tq
        k = tok_rows(kv_ref, start, 0, kv, length)
        v = tok_rows(kv_ref, start, 1, kv, length)
        s = lax.dot_general(q4[kv], k, (NT, ((), ())), preferred_element_type=F32)
        outs = []
        for g in range(NSA_G):
            h = kv * NSA_G + g
            bias = jnp.concatenate([bt_ref[h, jnp.clip(first_tile - j, 0, 2)] for j in range(ntile)], axis=1)
            sg = jnp.where(allowed, s[g * tq:(g + 1) * tq] + bias, NEG_INF)
            e = jnp.exp2(sg - jnp.max(sg, axis=-1, keepdims=True))
            den = jnp.sum(e, axis=-1, keepdims=True)
            outs.append(jnp.dot(e.astype(BF16), v, preferred_element_type=F32) / den)
        return outs

    def sel_branch(length):
        qpos = qi * tq + lax.broadcasted_iota(jnp.int32, (tq, length), 0)
        kpos = lax.broadcasted_iota(jnp.int32, (tq, length), 1)
        eb = lax.broadcasted_iota(jnp.int32, (nb, length), 0)
        ec = lax.broadcasted_iota(jnp.int32, (nb, length), 1) // NSA_BLK
        onehot = jnp.where(eb == ec, 1.0, 0.0).astype(BF16)
        for kv in range(NSA_KV):
            picked = _mm(selt_ref[0, kv], onehot, TN)
            allowed = jnp.where(kpos <= qpos, picked, 0.0) > 0.5
            outs = attend(kv, sel_ref, 0, length // tq, allowed, qi)
            for g in range(NSA_G):
                os_ref[kv * NSA_G + g] = outs[g]

    for i in range(t // kbs):
        pl.when((qi * tq) // kbs == i)(functools.partial(sel_branch, (i + 1) * kbs))

    lw = WIN_TILES * tq
    st = jnp.clip(qi - (WIN_TILES - 1), 0, t // tq - WIN_TILES)
    off = pl.multiple_of(st * tq * ROWS_PER_TOK, tq * ROWS_PER_TOK)
    dist = (qi - st) * tq + (lax.broadcasted_iota(jnp.int32, (tq, lw), 0)
                             - lax.broadcasted_iota(jnp.int32, (tq, lw), 1))
    win_allowed = jnp.where(dist >= 0, jnp.where(dist < NSA_WINDOW, 1.0, 0.0), 0.0) > 0.5
    for kv in range(NSA_KV):
        o_win = attend(kv, win_ref, off, WIN_TILES, win_allowed, qi - st)
        for g in range(NSA_G):
            h = kv * NSA_G + g
            gate = [jax.nn.sigmoid(sm[:, SM_GATE + br * NSA_HEADS + h:SM_GATE + br * NSA_HEADS + h + 1])
                    for br in range(3)]
            cols = slice(h * NSA_DH, (h + 1) * NSA_DH)
            o_ref[0, :, cols] = (gate[0] * oc_ref[0, :, cols] + gate[1] * os_ref[h]
                                 + gate[2] * o_win[g]).astype(o_ref.dtype)


def _selwin(proj, sel_rows, win_rows, small, selt, bias_tiles, o_cmp):
    b, t, _ = proj.shape
    tq = TQ
    nb = selt.shape[2]
    assert t >= WIN_TILES * tq
    kbs = math.gcd(t, 2 * tq)
    kvspec = pl.BlockSpec((t * ROWS_PER_TOK, NSA_DH), lambda i, j: (i, 0))
    return pl.pallas_call(
        functools.partial(_selwin_kernel, kbs=kbs),
        grid=(b, t // tq),
        in_specs=[pl.BlockSpec((1, tq, 1024), lambda i, j: (i, j, C_QN // 1024)),
                  kvspec, kvspec,
                  pl.BlockSpec((1, NSA_KV, nb, tq), lambda i, j: (i, 0, 0, j)),
                  _resident((NSA_HEADS, 3, tq, tq)),
                  pl.BlockSpec((1, tq, 1024), lambda i, j: (i, j, 0)),
                  pl.BlockSpec((1, tq, LANE), lambda i, j: (i, j, 0))],
        out_specs=pl.BlockSpec((1, tq, NSA_HEADS * NSA_DH), lambda i, j: (i, j, 0)),
        out_shape=jax.ShapeDtypeStruct((b, t, NSA_HEADS * NSA_DH), BF16),
        scratch_shapes=[pltpu.VMEM((NSA_HEADS, tq, NSA_DH), F32)],
        compiler_params=_cparams(("parallel", "arbitrary")),
        name="selwin",
    )(proj, sel_rows, win_rows, selt, bias_tiles, o_cmp, small)


def _nsa_sample_kernel(pt_ref, q_ref, seln_ref, winn_ref, gt_ref, win_ref, bsel_ref, bwin_ref, bcmp_ref,
                       bself_ref, *rest, n_pages, n_top):
    del pt_ref
    new_row = lambda ref, r, which, kv: ref[r * ROWS_PER_TOK + which * NSA_KV + kv:
                                            r * ROWS_PER_TOK + which * NSA_KV + kv + 1, :]
    nrow = q_ref.shape[0]
    sel_pages = rest[:nrow * n_pages]
    kvc_pages = rest[nrow * n_pages:2 * nrow * n_pages]
    o_ref, wout_ref, kvc_s = rest[2 * nrow * n_pages:2 * nrow * n_pages + 3]
    n = n_pages * (PAGE_SIZE // NSA_BLK)
    past = n_pages * PAGE_SIZE
    wrows = win_ref.shape[0] // nrow
    wb = wrows // ROWS_PER_TOK
    R = SUBLANE
    U = [(r, kv) for r in range(nrow) for kv in range(NSA_KV)]
    NU = range(len(U))
    tok_rows = lambda ref, base, which, kv, ntok: ref[pl.ds(base + which * NSA_KV + kv, ntok,
                                                            stride=ROWS_PER_TOK), :]
    page = lambda r, i: sel_pages[r * n_pages + i]

    for r in range(nrow):
        wout_ref[r * wrows:(r + 1) * wrows - ROWS_PER_TOK, :] = win_ref[r * wrows + ROWS_PER_TOK:(r + 1) * wrows, :]
        wout_ref[(r + 1) * wrows - ROWS_PER_TOK:(r + 1) * wrows, :] = winn_ref[r * ROWS_PER_TOK:
                                                                                (r + 1) * ROWS_PER_TOK, :]

    q8 = [q_ref[r] for r in range(nrow)]
    rowkv = lax.broadcasted_iota(jnp.int32, (R, 1), 0) // NSA_G
    bself = bself_ref[:, 0:1]

    for r in range(nrow):
        for i in range(n_pages):
            kvc_s[r, 2 * i:2 * i + 2, :] = kvc_pages[r * n_pages + i][0]
        kvc_s[r, n:LANE, :] = jnp.zeros((LANE - n, KV_ROW), F32)
    lane = lax.broadcasted_iota(jnp.int32, (R, LANE), 1)
    ii = lax.broadcasted_iota(jnp.int32, (LANE, LANE), 0)
    jj = lax.broadcasted_iota(jnp.int32, (LANE, LANE), 1)
    jrow = lax.broadcasted_iota(jnp.int32, (1, LANE), 1)
    eb = lax.broadcasted_iota(jnp.int32, (LANE, past), 0)
    ec = lax.broadcasted_iota(jnp.int32, (LANE, past), 1) // NSA_BLK
    expand = jnp.where(eb == ec, 1.0, 0.0).astype(BF16)
    wr = lax.broadcasted_iota(jnp.int32, (R, wb), 1)
    win_valid = (wb - wr) < NSA_WINDOW

    kc = [kvc_s[r, :, kv * NSA_DH:(kv + 1) * NSA_DH] for r, kv in U]
    vc = [kvc_s[r, :, (NSA_KV + kv) * NSA_DH:(NSA_KV + kv + 1) * NSA_DH] for r, kv in U]
    lg = [_mm3(q8[r], kc[u], NT) * NSA_SCALE + bcmp_ref[...] for u, (r, kv) in enumerate(U)]
    lg = [jnp.where(lane < n, x, NEG_INF) for x in lg]
    e = [jnp.exp(x - jnp.max(x, axis=-1, keepdims=True)) for x in lg]
    p = [jnp.where(lane < n, x / jnp.sum(x, axis=-1, keepdims=True), 0.0) for x in e]
    o_cmp = [_mm(p[u], vc[u]) for u in NU]
    imp = [jnp.sum(jnp.where(rowkv == kv, p[u], 0.0), axis=0, keepdims=True) for u, (r, kv) in enumerate(U)]
    forced = jnp.where(jrow == 0, 1.0, 0.0) + jnp.where(jrow == n, 1.0, 0.0) + jnp.where(jrow == n - 1, 1.0, 0.0)
    score = [jnp.where(jrow <= n, jnp.where(forced > 0.5, FORCE_SCORE, x), -2.0) for x in imp]
    s_b = [jnp.broadcast_to(x, (LANE, LANE)) for x in score]
    s_t = [jnp.transpose(x) for x in s_b]
    ahead = [jnp.where(s_t[u] > s_b[u], 1.0, jnp.where(s_t[u] == s_b[u], jnp.where(ii < jj, 1.0, 0.0), 0.0))
             for u in NU]
    rank = [jnp.sum(x, axis=0, keepdims=True) for x in ahead]
    sel8 = [jnp.broadcast_to(jnp.where(x < n_top, jnp.where(jrow <= n, 1.0, 0.0), 0.0), (R, LANE)) for x in rank]

    lg = [jnp.concatenate([_mm(q8[r], tok_rows(page(r, i), 0, 0, kv, PAGE_SIZE), NT) for i in range(n_pages)],
                          axis=1) * NSA_SCALE + bsel_ref[...] for r, kv in U]
    picked = [jnp.dot(x.astype(BF16), expand, preferred_element_type=F32) for x in sel8]
    lg = [jnp.where(picked[u] > 0.5, lg[u], NEG_INF) for u in NU]
    k_new = [new_row(seln_ref, r, 0, kv) for r, kv in U]
    v_new = [new_row(seln_ref, r, 1, kv) for r, kv in U]
    l_new = [jnp.sum(q8[r] * k_new[u], axis=-1, keepdims=True) * NSA_SCALE + bself for u, (r, kv) in enumerate(U)]
    l_new = [jnp.where(sel8[u][:, n:n + 1] > 0.5, l_new[u], NEG_INF) for u in NU]
    m = [jnp.maximum(jnp.max(lg[u], axis=-1, keepdims=True), l_new[u]) for u in NU]
    e = [jnp.exp(lg[u] - m[u]) for u in NU]
    e_new = [jnp.exp(l_new[u] - m[u]) for u in NU]
    den = [jnp.sum(e[u], axis=-1, keepdims=True) + e_new[u] for u in NU]
    pv = [e_new[u] * v_new[u] for u in NU]
    for i in range(n_pages):
        pv = [pv[u] + _mm(e[u][:, i * PAGE_SIZE:(i + 1) * PAGE_SIZE], tok_rows(page(r, i), 0, 1, kv, PAGE_SIZE))
              for u, (r, kv) in enumerate(U)]
    o_sel = [pv[u] / den[u] for u in NU]

    lg = [_mm(q8[r], tok_rows(win_ref, r * wrows, 0, kv, wb), NT) * NSA_SCALE + bwin_ref[...] for r, kv in U]
    lg = [jnp.where(win_valid, x, NEG_INF) for x in lg]
    k_new = [new_row(winn_ref, r, 0, kv) for r, kv in U]
    v_new = [new_row(winn_ref, r, 1, kv) for r, kv in U]
    l_new = [jnp.sum(q8[r] * k_new[u], axis=-1, keepdims=True) * NSA_SCALE + bself for u, (r, kv) in enumerate(U)]
    m = [jnp.maximum(jnp.max(lg[u], axis=-1, keepdims=True), l_new[u]) for u in NU]
    e = [jnp.exp(lg[u] - m[u]) for u in NU]
    e_new = [jnp.exp(l_new[u] - m[u]) for u in NU]
    den = [jnp.sum(e[u], axis=-1, keepdims=True) + e_new[u] for u in NU]
    o_win = [(e_new[u] * v_new[u] + _mm(e[u], tok_rows(win_ref, r * wrows, 1, kv, wb))) / den[u]
             for u, (r, kv) in enumerate(U)]

    for r in range(nrow):
        gate = jax.nn.sigmoid(gt_ref[r])
        pick = lambda outs: jnp.where(rowkv == 0, outs[r * NSA_KV], outs[r * NSA_KV + 1])
        o_ref[r] = gate[:, 0:1] * pick(o_cmp) + gate[:, 1:2] * pick(o_sel) + gate[:, 2:3] * pick(o_win)


NSA_SAMPLE_ROWS = 2


def _nsa_sample(page_table, q8, sel_new, win_new, gates, win_rows, sel_rows, kvc_pool, layer, n_pool, wb,
                bsel, bwin, bcmp, bself):
    db, n_pages = page_table.shape
    nrow = NSA_SAMPLE_ROWS
    assert db % nrow == 0
    past = n_pages * PAGE_SIZE
    n_blocks = past // NSA_BLK + 1
    kern = functools.partial(_nsa_sample_kernel, n_pages=n_pages, n_top=min(NSA_TOPN, n_blocks))
    fixed = lambda shape: pl.BlockSpec(shape, lambda i, pt: (0,) * len(shape))
    wrows = wb * ROWS_PER_TOK
    prows = PAGE_SIZE * ROWS_PER_TOK
    in_specs = [pl.BlockSpec((nrow, NSA_HEADS, NSA_DH), lambda i, pt: (i, 0, 0)),
                pl.BlockSpec((nrow * ROWS_PER_TOK, NSA_DH), lambda i, pt: (i, 0)),
                pl.BlockSpec((nrow * ROWS_PER_TOK, NSA_DH), lambda i, pt: (i, 0)),
                pl.BlockSpec((nrow, NSA_HEADS, 3), lambda i, pt: (i, 0, 0)),
                pl.BlockSpec((nrow * wrows, NSA_DH), lambda i, pt: (layer * (db // nrow) + i, 0)),
                fixed((NSA_HEADS, past)), fixed((NSA_HEADS, wb)), fixed((NSA_HEADS, LANE)),
                fixed((NSA_HEADS, LANE))]
    for r in range(nrow):
        for p in range(n_pages):
            in_specs.append(pl.BlockSpec(
                (prows, NSA_DH), lambda i, pt, r=r, p=p: (layer * n_pool + pt[i * nrow + r, p], 0)))
    for r in range(nrow):
        for p in range(n_pages):
            in_specs.append(pl.BlockSpec(
                (1, PAGE_SIZE // NSA_BLK, KV_ROW), lambda i, pt, r=r, p=p: (pt[i * nrow + r, p], 0, 0)))
    return pl.pallas_call(
        kern,
        grid_spec=pltpu.PrefetchScalarGridSpec(
            num_scalar_prefetch=1, grid=(db // nrow,), in_specs=in_specs,
            out_specs=[pl.BlockSpec((nrow, NSA_HEADS, NSA_DH), lambda i, pt: (i, 0, 0)),
                       pl.BlockSpec((nrow * wrows, NSA_DH), lambda i, pt: (i, 0))],
            scratch_shapes=[pltpu.VMEM((nrow, LANE, KV_ROW), F32)]),
        out_shape=[jax.ShapeDtypeStruct((db, NSA_HEADS, NSA_DH), F32),
                   jax.ShapeDtypeStruct((db * wrows, NSA_DH), F32)],
        compiler_params=_cparams(("arbitrary",)),
        name="nsa_sample",
    )(page_table, q8, sel_new, win_new, gates, win_rows, bsel, bwin, bcmp, bself,
      *([sel_rows] * (nrow * n_pages)), *([kvc_pool] * (nrow * n_pages)))


def _mix_kernel(oa_ref, ob_ref, ga_ref, gb_ref, x_ref, gt_ref, g_ref, woa_ref, wob_ref, wout_ref, o_ref):
    a = jnp.dot(oa_ref[0], woa_ref[...], preferred_element_type=F32)
    b = jnp.dot(ob_ref[0], wob_ref[...], preferred_element_type=F32)
    m = jax.nn.sigmoid(ga_ref[0]) * a + jax.nn.sigmoid(gb_ref[0]) * b
    mixed = _mm(m, wout_ref[...])
    o_ref[0] = x_ref[0] + gt_ref[0] * (_rms_rows(mixed) * g_ref[...])


def _mix(o_a, o_b, proj, x3, gt, g, w_oa, w_ob, w_out):
    bb, t, d = x3.shape
    tm = _pick(t, 512)
    row = lambda c: pl.BlockSpec((1, tm, d), lambda b, i: (b, i, c))
    return pl.pallas_call(
        _mix_kernel,
        grid=(bb, t // tm),
        in_specs=[row(0), row(0), row(C_GM // d), row(C_GM // d + 1), row(0), _mod_spec(gt, tm),
                  _resident((1, d)), _resident((d, d)), _resident((d, d)), _resident((d, d))],
        out_specs=row(0),
        out_shape=jax.ShapeDtypeStruct((bb, t, d), F32),
        compiler_params=_cparams(("parallel", "parallel")),
        name="mix",
    )(o_a, o_b, proj, proj, x3, gt, g.reshape(1, d), w_oa, w_ob, w_out)


FFN_SPLIT = 2


def _ffn_kernel(x_ref, g2_ref, sc_ref, sh_ref, gt_ref, g3_ref, wg_ref, wu_ref, wd_ref, o_ref):
    x = x_ref[0]
    h = _modulated_norm(x, g2_ref[...], sc_ref[0], sh_ref[0])
    tf = D_FF // FFN_SPLIT
    y = None
    for f in range(FFN_SPLIT):
        cols = slice(f * tf, (f + 1) * tf)
        gate = jnp.dot(h, wg_ref[:, cols], preferred_element_type=F32)
        up = jnp.dot(h, wu_ref[:, cols], preferred_element_type=F32)
        part = _mm(_silu(gate) * up, wd_ref[cols, :])
        y = part if y is None else y + part
    o_ref[0] = x + gt_ref[0] * (_rms_rows(y) * g3_ref[...])


def _ffn(x3, g2, sc, sh, gt, g3, w_gu, w_down):
    bb, t, d = x3.shape
    tm = _pick(t, 512)
    assert (D_FF // FFN_SPLIT) % LANE == 0
    row = pl.BlockSpec((1, tm, d), lambda b, i: (b, i, 0))
    return pl.pallas_call(
        _ffn_kernel,
        grid=(bb, t // tm),
        in_specs=[row, _resident((1, d)), _mod_spec(sc, tm), _mod_spec(sh, tm), _mod_spec(gt, tm),
                  _resident((1, d)), _resident((d, D_FF), (0, 0)), _resident((d, D_FF), (0, 1)),
                  _resident((D_FF, d))],
        out_specs=row,
        out_shape=jax.ShapeDtypeStruct((bb, t, d), F32),
        compiler_params=_cparams(("parallel", "parallel")),
        name="ffn",
    )(x3, g2.reshape(1, d), sc, sh, gt, g3.reshape(1, d), w_gu, w_gu, w_down)


def _bucket_np(n):
    n = np.maximum(n, 0)
    exact = NUM_BUCKETS // 2
    big = exact + (np.log(np.maximum(n, 1).astype(np.float32) / np.float32(exact))
                   / np.float32(math.log(MAX_DISTANCE / exact))
                   * np.float32(NUM_BUCKETS - exact)).astype(np.int32)
    return np.where(n < exact, n, np.minimum(big, NUM_BUCKETS - 1)).astype(np.int32)


def _bias_tables(rel_bias, t, past, wb):
    def table(dist):
        idx = jnp.asarray(_bucket_np(dist))[None]
        heads = rel_bias.shape[1]
        out = jnp.zeros((heads,) + idx.shape[1:], rel_bias.dtype)
        for bucket in range(NUM_BUCKETS):
            val = rel_bias[bucket].reshape((heads,) + (1,) * (idx.ndim - 1))
            out = jnp.where(idx == bucket, val, out)
        return out

    n = t // NSA_BLK
    ends = np.arange(n) * NSA_BLK + (NSA_BLK - 1)
    cmp_p = table(np.arange(t)[None, :] - ends[:, None])
    ar = np.arange(TQ)
    tiles = table(np.stack([d * TQ + ar[:, None] - ar[None, :] for d in range(3)]))
    sel_s = table(past - np.arange(past))
    win_s = table(wb - np.arange(wb))
    n_s = past // NSA_BLK
    cmp_s = table(np.pad(past - (np.arange(n_s) * NSA_BLK + NSA_BLK - 1), (0, LANE - n_s)))
    self_s = table(np.zeros((LANE,), np.int64))
    return cmp_p, tiles, sel_s, win_s, cmp_s, self_s


def _regroup_w_in(w):
    o = np.cumsum((0, CONV_DIM, DN_HEADS * DN_DV, DN_HEADS, DN_HEADS, NSA_HEADS * NSA_DH,
                   6 * NSA_KV * NSA_DH, 3 * NSA_HEADS, 2 * D_MODEL))
    qkv_z, b, a, qn, kvn, gn, gm = (w[:, o[0]:o[2]], w[:, o[2]:o[3]], w[:, o[3]:o[4]], w[:, o[4]:o[5]],
                                    w[:, o[5]:o[6]], w[:, o[6]:o[7]], w[:, o[7]:o[8]])
    pad = jnp.zeros((w.shape[0], N_PROJ - C_SM - 5 * NSA_HEADS), w.dtype)
    main = jnp.concatenate([qkv_z, qn, gm], axis=1).astype(BF16)
    kv_small = jnp.concatenate([kvn, b, a, gn, pad], axis=1).astype(BF16)
    return main, kv_small


def _layer_params(l, w_in, conv_w, a_log, dt_bias, onorm_g, w_oa, w_ob, w_out, cmp_pe, cmp_w1, cmp_w2,
                  w_gu, w_down):
    hp = jnp.zeros((SUBLANE, LANE), F32)
    hp = hp.at[0, SM_A:SM_A + DN_HEADS].set(a_log[l]).at[1, SM_A:SM_A + DN_HEADS].set(dt_bias[l])
    w_main, w_kv = _regroup_w_in(w_in[l])
    return dict(
        w_in=w_main, w_kv=w_kv, cw=conv_w[l], hp=hp, on=onorm_g[l].reshape(1, DN_DV),
        w_oa=w_oa[l].astype(BF16), w_ob=w_ob[l].astype(BF16), w_out=w_out[l].astype(BF16),
        pe=jnp.reshape(cmp_pe[l], (1, NSA_BLK * 2 * NSA_DH)),
        w1=jnp.transpose(cmp_w1[l], (1, 0, 2, 3)).reshape(NSA_BLK * 2 * NSA_DH, NSA_CMP_HID).astype(BF16),
        w2=cmp_w2[l].reshape(2 * NSA_CMP_HID, NSA_DH).astype(BF16),
        w_gu=w_gu[l].astype(BF16), w_down=w_down[l].astype(BF16))


def _mods(mod, per_token):
    parts = [mod[:, i * D_MODEL:(i + 1) * D_MODEL] for i in range(6)]
    return [p[None, :, :] if per_token else p[:, None, :] for p in parts]


def _tail(x3, o_a, o_b, proj, mods, ng, P):
    sh1, sc1, gt1, sh2, sc2, gt2 = mods
    x1 = _mix(o_a, o_b, proj, x3, gt1, ng[1], P['w_oa'], P['w_ob'], P['w_out'])
    return _ffn(x1, ng[2], sc2, sh2, gt2, ng[3], P['w_gu'], P['w_down'])


def _prompt_layer(x3, mods, ng, P, tabs):
    b, t, _ = x3.shape
    proj = _proj(x3, ng[0], mods[1], mods[0], P['w_in'])
    cmp_rows, sel_rows, win_rows, small = _kvproj(x3, ng[0], mods[1], mods[0], P['w_kv'])
    o_a, s_new = _dn_prompt(proj, small, P['cw'], P['hp'], P['on'])
    n = t // NSA_BLK
    kvc = _compress(cmp_rows, 0, b * n, P['pe'], P['w1'], P['w2']).reshape(b, n, KV_ROW)
    o_cmp, selt = _cmp_select(proj, kvc, tabs[0])
    o_b = _selwin(proj, sel_rows, win_rows, small, selt, tabs[1] * LOG2E, o_cmp)
    y = _tail(x3, o_a, o_b, proj, mods, ng, P)
    conv_new = proj[:, t - (CONV_W - 1):, C_QKV:C_QKV + CONV_DIM]
    as_cache = lambda rows: rows.reshape(b, t, 2, NSA_KV, NSA_DH)
    return y, (as_cache(cmp_rows), as_cache(sel_rows), as_cache(win_rows)[:, t - min(NSA_WINDOW, t):],
               s_new, conv_new)


def _sample_layer(x3, mods, ng, P, tabs, layer, page_table, cmp_blocks, sel_rows, win_rows, n_pool, wb,
                  state_all, conv_all):
    db = x3.shape[1]
    proj = _proj(x3, ng[0], mods[1], mods[0], P['w_in'])
    cmp_new, sel_new, win_new, small = _kvproj(x3, ng[0], mods[1], mods[0], P['w_kv'])
    o_a, s_new = _dn_sample(proj.reshape(db, 1, N_MAIN), small.reshape(db, 1, LANE), conv_all, state_all,
                            layer, P['cw'], P['hp'], P['on'])
    halves = PAGE_SIZE // NSA_BLK
    kvc_pool = _compress(cmp_blocks, layer * n_pool * halves, n_pool * halves, P['pe'], P['w1'], P['w2'])
    q8 = proj[0, :, C_QN:C_QN + NSA_HEADS * NSA_DH].reshape(db, NSA_HEADS, NSA_DH)
    gates = jnp.transpose(small[0, :, SM_GATE:SM_GATE + 3 * NSA_HEADS].reshape(db, 3, NSA_HEADS), (0, 2, 1))
    o_b, win_cache = _nsa_sample(page_table, q8, sel_new, win_new, gates, win_rows, sel_rows,
                                 kvc_pool.reshape(n_pool, halves, KV_ROW), layer, n_pool, wb,
                                 tabs[2], tabs[3], tabs[4], tabs[5])
    y = _tail(x3, o_a.reshape(1, db, D_MODEL).astype(BF16), o_b.reshape(1, db, D_MODEL).astype(BF16),
              proj, mods, ng, P)
    qkv_new = proj[0, :, C_QKV:C_QKV + CONV_DIM]
    conv_new = jnp.concatenate([conv_all[layer][:, 1:], qkv_new[:, None, :]], axis=1)
    as_cache = lambda rows, ntok: rows.reshape(db, ntok, 2, NSA_KV, NSA_DH)
    return y, (as_cache(cmp_new, 1), as_cache(sel_new, 1), as_cache(win_cache, wb), s_new, conv_new)


def kernel(x_prompt, x_sample, cache_cmp_kv, cache_sel_kv, cache_win_kv, state_delta, state_conv, page_table, c_prompt, c_sample, w_ada, b_ada, norm_g, w_in, conv_w, a_log, dt_bias, onorm_g, w_oa, w_ob, w_out, cmp_pe, cmp_w1, cmp_w2, rel_bias, w_gu, w_down):
    depth = w_in.shape[0]
    b, t, d = x_prompt.shape
    db, ts, _ = x_sample.shape
    assert ts == 1 and d == D_MODEL and t % TQ == 0
    n_pool, n_pages = cache_cmp_kv.shape[1], page_table.shape[1]
    assert cache_cmp_kv.shape[2] == PAGE_SIZE
    past = n_pages * PAGE_SIZE
    wb = cache_win_kv.shape[2]
    tabs = _bias_tables(rel_bias, t, past, wb)
    cmp_blocks = cache_cmp_kv.reshape(depth * n_pool * PAGE_SIZE * ROWS_PER_TOK, NSA_DH)
    sel_rows = cache_sel_kv.reshape(depth * n_pool * PAGE_SIZE * ROWS_PER_TOK, NSA_DH)
    win_rows = cache_win_kv.reshape(depth * db * wb * ROWS_PER_TOK, NSA_DH)
    c_all = jnp.concatenate([c_prompt, c_sample], axis=0)
    pad_rows = (-c_all.shape[0]) % SUBLANE
    c_all = jnp.pad(c_all, ((0, pad_rows), (0, 0)))
    yp = x_prompt
    ys = x_sample.reshape(1, db, d)
    p_st, s_st = [], []
    for l in range(depth):
        P = _layer_params(l, w_in, conv_w, a_log, dt_bias, onorm_g, w_oa, w_ob, w_out, cmp_pe, cmp_w1,
                          cmp_w2, w_gu, w_down)
        mod = _ada(c_all, w_ada[l], b_ada[l])
        yp, st = _prompt_layer(yp, _mods(mod[:b], False), norm_g[l], P, tabs)
        p_st.append(st)
        ys, st = _sample_layer(ys, _mods(mod[b:b + db], True), norm_g[l], P, tabs, l, page_table,
                               cmp_blocks, sel_rows, win_rows, n_pool, wb, state_delta, state_conv)
        s_st.append(st)
    stk = lambda states, i: jnp.stack([s[i] for s in states])
    return (yp, ys.reshape(db, 1, d),
            stk(p_st, 0), stk(p_st, 1), stk(p_st, 2), stk(p_st, 3), stk(p_st, 4),
            stk(s_st, 0), stk(s_st, 1), stk(s_st, 2), stk(s_st, 3), stk(s_st, 4))
```

```python
import functools
import math

import numpy as np
import jax
import jax.numpy as jnp
from jax import lax
from jax.experimental import pallas as pl
from jax.experimental.pallas import tpu as pltpu

F32 = jnp.float32
BF16 = jnp.bfloat16

D_MODEL = 1024
EPS = 1e-6
DN_HEADS = 8
DN_DK = 128
DN_DV = 128
CONV_W = 4
CONV_DIM = 2 * DN_HEADS * DN_DK + DN_HEADS * DN_DV
DN_CHUNK = 64
NSA_HEADS = 8
NSA_KV = 2
NSA_G = NSA_HEADS // NSA_KV
NSA_DH = 128
NSA_SCALE = NSA_DH ** -0.5
NSA_BLK = 64
NSA_TOPN = 16
NSA_WINDOW = 512
NSA_CMP_HID = 128
FORCE_SCORE = 1e4
NEG_INF = -1e30
NUM_BUCKETS = 32
MAX_DISTANCE = 128
D_FF = 256 * (-(-8 * D_MODEL // (3 * 256)))
PAGE_SIZE = 128

C_QKV = 0
C_Z = 3072
C_QN = 4096
C_GM = 5120
C_KVN = 7168
C_SM = 8704
N_PROJ = 8832
N_MAIN = C_KVN
LOG2E = math.log2(math.e)
SM_BETA = 0
SM_A = 8
SM_GATE = 16
KV_ROW = 2 * NSA_KV * NSA_DH
ROWS_PER_TOK = 2 * NSA_KV

LANE = 128
SUBLANE = 8
VMEM_LIMIT = 48 * 1024 * 1024
TQ = 128
WIN_TILES = NSA_WINDOW // TQ + 1


def _cparams(sem):
    return pltpu.CompilerParams(dimension_semantics=sem, vmem_limit_bytes=VMEM_LIMIT)


def _pick(n, pref, mult=SUBLANE):
    for t in range(min(pref, n), 0, -1):
        if n % t == 0 and t % mult == 0:
            return t
    return n


def _mm(a, b, dims=((1,), (0,))):
    return lax.dot_general(a.astype(BF16), b.astype(BF16), (dims, ((), ())),
                           preferred_element_type=F32)


NN = ((1,), (0,))
NT = ((1,), (1,))
TN = ((0,), (0,))


def _hilo(a):
    hi = a.astype(BF16)
    lo = (a - hi.astype(F32)).astype(BF16)
    return hi, lo


def _mm3(a, b, dims=NN):
    ah, al = _hilo(a)
    bh, bl = _hilo(b)
    return _mm(ah, bh, dims) + (_mm(ah, bl, dims) + _mm(al, bh, dims))


def _mm_exact_lhs(a01, b):
    b0 = b.astype(BF16)
    r1 = b - b0.astype(F32)
    b1 = r1.astype(BF16)
    b2 = (r1 - b1.astype(F32)).astype(BF16)
    return _mm(a01, b0) + (_mm(a01, b1) + _mm(a01, b2))


def _silu(x):
    return x * jax.nn.sigmoid(x)


def _softplus(x):
    return jnp.maximum(x, 0.0) + jnp.log1p(jnp.exp(-jnp.abs(x)))


def _rms_rows(x):
    return x * lax.rsqrt(jnp.mean(x * x, axis=-1, keepdims=True) + EPS)


def _ada_kernel(c_ref, w_ref, b_ref, o_ref):
    o_ref[...] = _mm(_silu(c_ref[...]), w_ref[...]) + b_ref[...]


def _ada(c_all, w, b):
    m, d = c_all.shape
    n = w.shape[1]
    tn = _pick(n, 1536, LANE)
    return pl.pallas_call(
        _ada_kernel,
        grid=(n // tn,),
        in_specs=[pl.BlockSpec((m, d), lambda j: (0, 0)),
                  pl.BlockSpec((d, tn), lambda j: (0, j)),
                  pl.BlockSpec((1, tn), lambda j: (0, j))],
        out_specs=pl.BlockSpec((m, tn), lambda j: (0, j)),
        out_shape=jax.ShapeDtypeStruct((m, n), F32),
        compiler_params=_cparams(("parallel",)),
        name="ada",
    )(c_all, w, b.reshape(1, n))


def _modulated_norm(x, g, sc, sh):
    return ((_rms_rows(x) * g) * (1.0 + sc) + sh).astype(BF16)


def _proj_kernel(x_ref, g_ref, sc_ref, sh_ref, w_ref, o_ref):
    h = _modulated_norm(x_ref[0], g_ref[...], sc_ref[0], sh_ref[0])
    o_ref[0] = jnp.dot(h, w_ref[...], preferred_element_type=F32)


def _mod_spec(mod, tm):
    if mod.shape[1] == 1:
        return pl.BlockSpec((1, 1, D_MODEL), lambda b, i: (b, 0, 0))
    return pl.BlockSpec((1, tm, D_MODEL), lambda b, i: (b, i, 0))


def _kvproj_kernel(x_ref, g_ref, sc_ref, sh_ref, w_ref, cmp_ref, sel_ref, win_ref, sm_ref):
    tm = x_ref.shape[1]
    h = _modulated_norm(x_ref[0], g_ref[...], sc_ref[0], sh_ref[0])
    res = jnp.dot(h, w_ref[...], preferred_element_type=F32)
    for br, out in enumerate((cmp_ref, sel_ref, win_ref)):
        for r in range(ROWS_PER_TOK):
            c0 = br * KV_ROW + r * NSA_DH
            out[pl.ds(r, tm, stride=ROWS_PER_TOK), :] = res[:, c0:c0 + NSA_DH]
    sm_ref[0] = res[:, 3 * KV_ROW:3 * KV_ROW + LANE]


def _kvproj(x3, g, sc, sh, w_kv):
    bb, t, d = x3.shape
    n = w_kv.shape[1]
    tm = _pick(t, 512)
    nt = t // tm
    rows = pl.BlockSpec((tm * ROWS_PER_TOK, NSA_DH), lambda b, i: (b * nt + i, 0))
    rows_shape = jax.ShapeDtypeStruct((bb * t * ROWS_PER_TOK, NSA_DH), F32)
    return pl.pallas_call(
        _kvproj_kernel,
        grid=(bb, nt),
        in_specs=[pl.BlockSpec((1, tm, d), lambda b, i: (b, i, 0)), _resident((1, d)),
                  _mod_spec(sc, tm), _mod_spec(sh, tm), _resident((d, n))],
        out_specs=[rows, rows, rows, pl.BlockSpec((1, tm, LANE), lambda b, i: (b, i, 0))],
        out_shape=[rows_shape, rows_shape, rows_shape, jax.ShapeDtypeStruct((bb, t, LANE), F32)],
        compiler_params=_cparams(("parallel", "parallel")),
        name="kvproj",
    )(x3, g.reshape(1, d), sc, sh, w_kv)


def _resident(shape, index=None):
    index = index or (0,) * len(shape)
    return pl.BlockSpec(shape, lambda *_: index, pipeline_mode=pl.Buffered(1))


def _proj(x3, g, sc, sh, w):
    bb, t, d = x3.shape
    n = w.shape[1]
    tm = _pick(t, 512)
    tn = _pick(n, 3584, LANE)
    rows = lambda spec: pl.BlockSpec(spec.block_shape, lambda j, b, i: spec.index_map(b, i))
    return pl.pallas_call(
        _proj_kernel,
        grid=(n // tn, bb, t // tm),
        in_specs=[pl.BlockSpec((1, tm, d), lambda j, b, i: (b, i, 0)),
                  pl.BlockSpec((1, d), lambda j, b, i: (0, 0)),
                  rows(_mod_spec(sc, tm)), rows(_mod_spec(sh, tm)),
                  pl.BlockSpec((d, tn), lambda j, b, i: (0, j))],
        out_specs=pl.BlockSpec((1, tm, tn), lambda j, b, i: (b, i, j)),
        out_shape=jax.ShapeDtypeStruct((bb, t, n), F32),
        compiler_params=_cparams(("parallel", "parallel", "parallel")),
        name="proj",
    )(x3, g.reshape(1, d), sc, sh, w)


def _dn_prompt_kernel(qkv_ref, z_ref, sm_ref, cw_ref, hp_ref, on_ref, o_ref, s_ref, buf_ref, st_ref):
    c = pl.program_id(1)
    C = DN_CHUNK
    nrow = qkv_ref.shape[0]
    hist = SUBLANE

    ntile = CONV_DIM // LANE

    @pl.when(c == 0)
    def _():
        buf_ref[:, :, 0:hist, :] = jnp.zeros((nrow, ntile, hist, LANE), F32)
        st_ref[...] = jnp.zeros(st_ref.shape, F32)

    for r in range(nrow):
        for ct in range(ntile):
            buf_ref[r, ct, hist:hist + C, :] = qkv_ref[r, :, ct * LANE:(ct + 1) * LANE]

    def conv_act(r, c0):
        acc = None
        for i in range(CONV_W):
            r0 = hist - (CONV_W - 1) + i
            term = buf_ref[r, c0 // LANE, r0:r0 + C, :] * cw_ref[i:i + 1, c0:c0 + LANE]
            acc = term if acc is None else acc + term
        return _silu(acc)

    ri = lax.broadcasted_iota(jnp.int32, (C, C), 0)
    ci = lax.broadcasted_iota(jnp.int32, (C, C), 1)
    incl = ri >= ci
    strict = ri > ci
    tri = jnp.where(incl, 1.0, 0.0).astype(BF16)
    eye = jnp.where(ri == ci, 1.0, 0.0)
    sm = [sm_ref[r] for r in range(nrow)]
    beta_all = [jax.nn.sigmoid(x) for x in sm]
    g_all = [-jnp.exp(hp_ref[0:1, :]) * _softplus(x + hp_ref[1:2, :]) for x in sm]
    gc_all = [_mm_exact_lhs(tri, x) for x in g_all]
    gc_t = [jnp.transpose(jnp.concatenate([x, jnp.zeros((LANE - C, LANE), F32)], axis=0)) for x in gc_all]
    eg_all = [jnp.exp(x) for x in gc_all]

    U = [(r, h) for r in range(nrow) for h in range(DN_HEADS)]
    NU = range(len(U))
    q = [conv_act(r, h * DN_DK) for r, h in U]
    k = [conv_act(r, DN_HEADS * DN_DK + h * DN_DK) for r, h in U]
    v = [conv_act(r, 2 * DN_HEADS * DN_DK + h * DN_DV) for r, h in U]
    q = [x * lax.rsqrt(jnp.sum(x * x, axis=-1, keepdims=True) + EPS) * (DN_DK ** -0.5) for x in q]
    k = [x * lax.rsqrt(jnp.sum(x * x, axis=-1, keepdims=True) + EPS) for x in k]
    beta = [beta_all[r][:, SM_BETA + h:SM_BETA + h + 1] for r, h in U]
    gc = [gc_all[r][:, SM_A + h:SM_A + h + 1] for r, h in U]
    eg = [eg_all[r][:, SM_A + h:SM_A + h + 1] for r, h in U]
    g_last = [gc_all[r][C - 1:C, SM_A + h:SM_A + h + 1] for r, h in U]
    decay = [jnp.exp(jnp.where(incl, gc[u] - gc_t[r][SM_A + h:SM_A + h + 1, 0:C], NEG_INF))
             for u, (r, h) in enumerate(U)]
    kb = [k[u] * beta[u] for u in NU]
    a = [jnp.where(strict, _mm(kb[u], k[u], NT) * decay[u], 0.0) for u in NU]
    qk = [_mm(q[u], k[u], NT) * decay[u] for u in NU]
    t = [eye - a[u] for u in NU]
    p = [_mm(a[u], a[u]) for u in NU]
    steps = int(math.log2(C)) - 1
    for s in range(steps):
        t = [t[u] + _mm(t[u], p[u]) for u in NU]
        if s + 1 < steps:
            p = [_mm(p[u], p[u]) for u in NU]
    resid = [eye - t[u] - _mm3(a[u], t[u]) for u in NU]
    t = [t[u] + _mm(t[u], resid[u]) for u in NU]
    vb = [_mm(t[u], v[u] * beta[u]) for u in NU]
    w = [_mm(t[u], kb[u] * eg[u]) for u in NU]
    s_old = [st_ref[r, h] for r, h in U]
    v_new = [vb[u] - _mm(w[u], s_old[u]) for u in NU]
    o = [_mm(q[u] * eg[u], s_old[u]) + _mm(qk[u], v_new[u]) for u in NU]
    for u, (r, h) in enumerate(U):
        st_ref[r, h] = s_old[u] * jnp.exp(g_last[u]) + _mm(k[u] * jnp.exp(g_last[u] - gc[u]), v_new[u], TN)
    for u, (r, h) in enumerate(U):
        z = z_ref[r, :, h * DN_DV:(h + 1) * DN_DV]
        o_ref[r, :, h * DN_DV:(h + 1) * DN_DV] = (_rms_rows(o[u]) * on_ref[...] * _silu(z)).astype(o_ref.dtype)

    buf_ref[:, :, 0:hist, :] = buf_ref[:, :, C:C + hist, :]

    @pl.when(c == pl.num_programs(1) - 1)
    def _():
        s_ref[...] = st_ref[...]


DN_PROMPT_ROWS = 4


def _dn_prompt(proj, small, cw, hp, on):
    b, t, _ = proj.shape
    C = DN_CHUNK
    nrow = DN_PROMPT_ROWS if b % DN_PROMPT_ROWS == 0 else 1
    assert t % C == 0
    return pl.pallas_call(
        _dn_prompt_kernel,
        grid=(b // nrow, t // C),
        in_specs=[pl.BlockSpec((nrow, C, CONV_DIM), lambda i, c: (i, c, C_QKV // CONV_DIM)),
                  pl.BlockSpec((nrow, C, 1024), lambda i, c: (i, c, C_Z // 1024)),
                  pl.BlockSpec((nrow, C, LANE), lambda i, c: (i, c, 0)),
                  _resident((CONV_W, CONV_DIM)), _resident((SUBLANE, LANE)), _resident((1, DN_DV))],
        out_specs=[pl.BlockSpec((nrow, C, DN_HEADS * DN_DV), lambda i, c: (i, c, 0)),
                   pl.BlockSpec((nrow, DN_HEADS, DN_DK, DN_DV), lambda i, c: (i, 0, 0, 0))],
        out_shape=[jax.ShapeDtypeStruct((b, t, DN_HEADS * DN_DV), BF16),
                   jax.ShapeDtypeStruct((b, DN_HEADS, DN_DK, DN_DV), F32)],
        scratch_shapes=[pltpu.VMEM((nrow, CONV_DIM // LANE, C + SUBLANE, LANE), F32),
                        pltpu.VMEM((nrow, DN_HEADS, DN_DK, DN_DV), F32)],
        compiler_params=_cparams(("parallel", "arbitrary")),
        name="dn_prompt",
    )(proj, proj, small, cw, hp, on)


def _dn_sample_kernel(qkv_ref, z_ref, sm_ref, cb_ref, cw_ref, hp_ref, on_ref, sin_ref, o_ref, sout_ref):
    xb = cb_ref[0, 0]
    conv = qkv_ref[0] * cw_ref[CONV_W - 1:CONV_W, :]
    for i in range(CONV_W - 1):
        conv = conv + xb[i:i + 1, :] * cw_ref[i:i + 1, :]
    act = _silu(conv)
    sm = sm_ref[0]
    beta_all = jax.nn.sigmoid(sm)
    eg_all = jnp.exp(-jnp.exp(hp_ref[0:1, :]) * _softplus(sm + hp_ref[1:2, :]))
    row = lax.broadcasted_iota(jnp.int32, (SUBLANE, DN_DK), 0)
    rows = lambda x: jnp.broadcast_to(x, row.shape)
    H = range(DN_HEADS)
    q = [act[:, h * DN_DK:(h + 1) * DN_DK] for h in H]
    k = [act[:, (DN_HEADS + h) * DN_DK:(DN_HEADS + h + 1) * DN_DK] for h in H]
    v = [act[:, 2 * DN_HEADS * DN_DK + h * DN_DV:2 * DN_HEADS * DN_DK + (h + 1) * DN_DV] for h in H]
    q = [x * lax.rsqrt(jnp.sum(x * x, axis=-1, keepdims=True) + EPS) * (DN_DK ** -0.5) for x in q]
    k = [x * lax.rsqrt(jnp.sum(x * x, axis=-1, keepdims=True) + EPS) for x in k]
    beta = [beta_all[:, SM_BETA + h:SM_BETA + h + 1] for h in H]
    eg = [eg_all[:, SM_A + h:SM_A + h + 1] for h in H]
    s_old = [sin_ref[0, 0, h] for h in H]
    kq = [jnp.where(row == 0, rows(k[h]), jnp.where(row == 1, rows(q[h]), 0.0)) for h in H]
    r = [_mm3(kq[h], s_old[h]) for h in H]
    v_new = [beta[h] * (v[h] - eg[h] * r[h][0:1, :]) for h in H]
    qk = [jnp.sum(q[h] * k[h], axis=-1, keepdims=True) for h in H]
    o = [eg[h] * r[h][1:2, :] + qk[h] * v_new[h] for h in H]
    outer = [_mm3(jnp.where(row == 0, rows(k[h]), 0.0), rows(v_new[h]), TN) for h in H]
    for h in H:
        sout_ref[0, h] = s_old[h] * eg[h] + outer[h]
    for h in H:
        z = z_ref[0, :, h * DN_DV:(h + 1) * DN_DV]
        o_ref[0, :, h * DN_DV:(h + 1) * DN_DV] = _rms_rows(o[h]) * on_ref[...] * _silu(z)


def _dn_sample(proj_rows, small_rows, conv_all, state_all, layer, cw, hp, on):
    db = proj_rows.shape[0]
    return pl.pallas_call(
        _dn_sample_kernel,
        grid=(db,),
        in_specs=[pl.BlockSpec((1, 1, CONV_DIM), lambda i: (i, 0, C_QKV // CONV_DIM)),
                  pl.BlockSpec((1, 1, 1024), lambda i: (i, 0, C_Z // 1024)),
                  pl.BlockSpec((1, 1, LANE), lambda i: (i, 0, 0)),
                  pl.BlockSpec((1, 1, CONV_W - 1, CONV_DIM), lambda i: (layer, i, 0, 0)),
                  pl.BlockSpec((CONV_W, CONV_DIM), lambda i: (0, 0)),
                  pl.BlockSpec((SUBLANE, LANE), lambda i: (0, 0)),
                  pl.BlockSpec((1, DN_DV), lambda i: (0, 0)),
                  pl.BlockSpec((1, 1, DN_HEADS, DN_DK, DN_DV), lambda i: (layer, i, 0, 0, 0))],
        out_specs=[pl.BlockSpec((1, 1, DN_HEADS * DN_DV), lambda i: (i, 0, 0)),
                   pl.BlockSpec((1, DN_HEADS, DN_DK, DN_DV), lambda i: (i, 0, 0, 0))],
        out_shape=[jax.ShapeDtypeStruct((db, 1, DN_HEADS * DN_DV), F32),
                   jax.ShapeDtypeStruct((db, DN_HEADS, DN_DK, DN_DV), F32)],
        compiler_params=_cparams(("parallel",)),
        name="dn_sample",
    )(proj_rows, proj_rows, small_rows, conv_all, cw, hp, on, state_all)


BLK_ROWS = NSA_BLK * ROWS_PER_TOK


def _compress_kernel(x_ref, pe_ref, w1_ref, w2_ref, o_ref):
    tb = x_ref.shape[0] // BLK_ROWS
    G = SUBLANE
    for ck in range(ROWS_PER_TOK):
        c = ck // NSA_KV
        acc = jnp.zeros((tb, NSA_CMP_HID), F32)
        for l0 in range(0, NSA_BLK, G):
            groups = []
            for bg in range(tb // G):
                parts = [x_ref[pl.ds((bg * G + b) * BLK_ROWS + l0 * ROWS_PER_TOK + ck, G, stride=ROWS_PER_TOK), :]
                         for b in range(G)]
                groups.append(jnp.swapaxes(jnp.stack(parts, axis=0), 0, 1))
            for p in range(G):
                l = l0 + p
                x = (jnp.concatenate([grp[p] for grp in groups], axis=0)
                     + pe_ref[:, (l * 2 + c) * NSA_DH:(l * 2 + c + 1) * NSA_DH])
                acc = acc + _mm(x, w1_ref[(l * 2 + c) * NSA_DH:(l * 2 + c + 1) * NSA_DH, :])
        o_ref[:, ck * NSA_DH:(ck + 1) * NSA_DH] = _mm(
            _silu(acc), w2_ref[c * NSA_CMP_HID:(c + 1) * NSA_CMP_HID, :])


def _compress(x_rows, row0, r, pe_flat, w1_r, w2_r):
    tb = _pick(r, 128)
    assert row0 % tb == 0 and tb % SUBLANE == 0
    return pl.pallas_call(
        _compress_kernel,
        grid=(r // tb,),
        in_specs=[pl.BlockSpec((tb * BLK_ROWS, NSA_DH), lambda i: (row0 // tb + i, 0)),
                  _resident((1, NSA_BLK * 2 * NSA_DH)), _resident((NSA_BLK * 2 * NSA_DH, NSA_CMP_HID)),
                  _resident((2 * NSA_CMP_HID, NSA_DH))],
        out_specs=pl.BlockSpec((tb, KV_ROW), lambda i: (i, 0)),
        out_shape=jax.ShapeDtypeStruct((r, KV_ROW), F32),
        compiler_params=_cparams(("parallel",)),
        name="compress",
    )(x_rows, pe_flat, w1_r, w2_r)


def _cmp_select_kernel(q_ref, kvc_ref, bias_ref, o_ref, selt_ref, *, n_top):
    qi = pl.program_id(1)
    tq = q_ref.shape[1]
    n = kvc_ref.shape[1]
    qpos = qi * tq + lax.broadcasted_iota(jnp.int32, (n, tq), 1)
    jn = lax.broadcasted_iota(jnp.int32, (n, tq), 0)
    valid = qpos >= jn * NSA_BLK + (NSA_BLK - 1)
    validf = jnp.where(valid, 1.0, 0.0)
    cur = qpos // NSA_BLK
    forced = jnp.where(jn == 0, 1.0, 0.0) + jnp.where(jn == cur, 1.0, 0.0) + jnp.where(jn == cur - 1, 1.0, 0.0)
    kvc = kvc_ref[0]
    kc = [kvc[:, kv * NSA_DH:(kv + 1) * NSA_DH] for kv in range(NSA_KV)]
    vc = [kvc[:, (NSA_KV + kv) * NSA_DH:(NSA_KV + kv + 1) * NSA_DH] for kv in range(NSA_KV)]
    HH = range(NSA_HEADS)
    lg = [_mm3(kc[h // NSA_G], q_ref[0, :, h * NSA_DH:(h + 1) * NSA_DH], NT) * NSA_SCALE + bias_ref[h]
          for h in HH]
    lg = [jnp.where(valid, x, NEG_INF) for x in lg]
    e = [jnp.exp(x - jnp.max(x, axis=0, keepdims=True)) for x in lg]
    p = [x / jnp.sum(x, axis=0, keepdims=True) * validf for x in e]
    for h in HH:
        o_ref[0, :, h * NSA_DH:(h + 1) * NSA_DH] = _mm(p[h], vc[h // NSA_G], TN)
    for kv in range(NSA_KV):
        imp = p[kv * NSA_G]
        for g in range(1, NSA_G):
            imp = imp + p[kv * NSA_G + g]
        score = jnp.where(jn <= cur, jnp.where(forced > 0.5, FORCE_SCORE, imp), -1.0)
        rank = jnp.zeros((n, tq), F32)
        for i in range(n):
            si = score[i:i + 1, :]
            rank = rank + jnp.where(si > score, 1.0, jnp.where(si == score, jnp.where(jn > i, 1.0, 0.0), 0.0))
        selt_ref[0, kv] = jnp.where(rank < n_top, 1.0, 0.0)


def _cmp_select(proj, kvc, bias_cmp):
    b, t, _ = proj.shape
    n = kvc.shape[1]
    tq = _pick(t, 2 * TQ, TQ)
    kern = functools.partial(_cmp_select_kernel, n_top=min(NSA_TOPN, n))
    return pl.pallas_call(
        kern,
        grid=(b, t // tq),
        in_specs=[pl.BlockSpec((1, tq, 1024), lambda i, j: (i, j, C_QN // 1024)),
                  pl.BlockSpec((1, n, KV_ROW), lambda i, j: (i, 0, 0)),
                  pl.BlockSpec((NSA_HEADS, n, tq), lambda i, j: (0, 0, j))],
        out_specs=[pl.BlockSpec((1, tq, NSA_HEADS * NSA_DH), lambda i, j: (i, j, 0)),
                   pl.BlockSpec((1, NSA_KV, n, tq), lambda i, j: (i, 0, 0, j))],
        out_shape=[jax.ShapeDtypeStruct((b, t, NSA_HEADS * NSA_DH), F32),
                   jax.ShapeDtypeStruct((b, NSA_KV, n, t), F32)],
        compiler_params=_cparams(("parallel", "parallel")),
        name="cmp_select",
    )(proj, kvc, bias_cmp)


def _selwin_kernel(q_ref, sel_ref, win_ref, selt_ref, bt_ref, oc_ref, sm_ref, o_ref, os_ref, *, kbs):
    qi = pl.program_id(1)
    tq = TQ
    t = sel_ref.shape[0] // ROWS_PER_TOK
    nb = selt_ref.shape[2]
    sm = sm_ref[0]
    tok_rows = lambda ref, start, which, kv, ntok: ref[pl.ds(start + which * NSA_KV + kv, ntok,
                                                             stride=ROWS_PER_TOK), :].astype(BF16)
    q4 = [jnp.concatenate([q_ref[0, :, (kv * NSA_G + g) * NSA_DH:(kv * NSA_G + g + 1) * NSA_DH]
                           for g in range(NSA_G)], axis=0) * (NSA_SCALE * LOG2E) for kv in range(NSA_KV)]
    q4 = [x.astype(BF16) for x in q4]

    def attend(kv, kv_ref, start, ntile, allowed, first_tile):
        length = ntile * tq
        k = tok_rows(kv_ref, start, 0, kv, length)
        v = tok_rows(kv_ref, start, 1, kv, length)
        s = lax.dot_general(q4[kv], k, (NT, ((), ())), preferred_element_type=F32)
        outs = []
        for g in range(NSA_G):
            h = kv * NSA_G + g
            bias = jnp.concatenate([bt_ref[h, jnp.clip(first_tile - j, 0, 2)] for j in range(ntile)], axis=1)
            sg = jnp.where(allowed, s[g * tq:(g + 1) * tq] + bias, NEG_INF)
            e = jnp.exp2(sg - jnp.max(sg, axis=-1, keepdims=True))
            den = jnp.sum(e, axis=-1, keepdims=True)
            outs.append(jnp.dot(e.astype(BF16), v, preferred_element_type=F32) / den)
        return outs

    def sel_branch(length):
        qpos = qi * tq + lax.broadcasted_iota(jnp.int32, (tq, length), 0)
        kpos = lax.broadcasted_iota(jnp.int32, (tq, length), 1)
        eb = lax.broadcasted_iota(jnp.int32, (nb, length), 0)
        ec = lax.broadcasted_iota(jnp.int32, (nb, length), 1) // NSA_BLK
        onehot = jnp.where(eb == ec, 1.0, 0.0).astype(BF16)
        for kv in range(NSA_KV):
            picked = _mm(selt_ref[0, kv], onehot, TN)
            allowed = jnp.where(kpos <= qpos, picked, 0.0) > 0.5
            outs = attend(kv, sel_ref, 0, length // tq, allowed, qi)
            for g in range(NSA_G):
                os_ref[kv * NSA_G + g] = outs[g]

    for i in range(t // kbs):
        pl.when((qi * tq) // kbs == i)(functools.partial(sel_branch, (i + 1) * kbs))

    lw = WIN_TILES * tq
    st = jnp.clip(qi - (WIN_TILES - 1), 0, t // tq - WIN_TILES)
    off = pl.multiple_of(st * tq * ROWS_PER_TOK, tq * ROWS_PER_TOK)
    dist = (qi - st) * tq + (lax.broadcasted_iota(jnp.int32, (tq, lw), 0)
                             - lax.broadcasted_iota(jnp.int32, (tq, lw), 1))
    win_allowed = jnp.where(dist >= 0, jnp.where(dist < NSA_WINDOW, 1.0, 0.0), 0.0) > 0.5
    for kv in range(NSA_KV):
        o_win = attend(kv, win_ref, off, WIN_TILES, win_allowed, qi - st)
        for g in range(NSA_G):
            h = kv * NSA_G + g
            gate = [jax.nn.sigmoid(sm[:, SM_GATE + br * NSA_HEADS + h:SM_GATE + br * NSA_HEADS + h + 1])
                    for br in range(3)]
            cols = slice(h * NSA_DH, (h + 1) * NSA_DH)
            o_ref[0, :, cols] = (gate[0] * oc_ref[0, :, cols] + gate[1] * os_ref[h]
                                 + gate[2] * o_win[g]).astype(o_ref.dtype)


def _selwin(proj, sel_rows, win_rows, small, selt, bias_tiles, o_cmp):
    b, t, _ = proj.shape
    tq = TQ
    nb = selt.shape[2]
    assert t >= WIN_TILES * tq
    kbs = math.gcd(t, 2 * tq)
    kvspec = pl.BlockSpec((t * ROWS_PER_TOK, NSA_DH), lambda i, j: (i, 0))
    return pl.pallas_call(
        functools.partial(_selwin_kernel, kbs=kbs),
        grid=(b, t // tq),
        in_specs=[pl.BlockSpec((1, tq, 1024), lambda i, j: (i, j, C_QN // 1024)),
                  kvspec, kvspec,
                  pl.BlockSpec((1, NSA_KV, nb, tq), lambda i, j: (i, 0, 0, j)),
                  _resident((NSA_HEADS, 3, tq, tq)),
                  pl.BlockSpec((1, tq, 1024), lambda i, j: (i, j, 0)),
                  pl.BlockSpec((1, tq, LANE), lambda i, j: (i, j, 0))],
        out_specs=pl.BlockSpec((1, tq, NSA_HEADS * NSA_DH), lambda i, j: (i, j, 0)),
        out_shape=jax.ShapeDtypeStruct((b, t, NSA_HEADS * NSA_DH), BF16),
        scratch_shapes=[pltpu.VMEM((NSA_HEADS, tq, NSA_DH), F32)],
        compiler_params=_cparams(("parallel", "arbitrary")),
        name="selwin",
    )(proj, sel_rows, win_rows, selt, bias_tiles, o_cmp, small)


def _nsa_sample_kernel(pt_ref, q_ref, seln_ref, winn_ref, gt_ref, win_ref, bsel_ref, bwin_ref, bcmp_ref,
                       bself_ref, *rest, n_pages, n_top):
    del pt_ref
    new_row = lambda ref, r, which, kv: ref[r * ROWS_PER_TOK + which * NSA_KV + kv:
                                            r * ROWS_PER_TOK + which * NSA_KV + kv + 1, :]
    nrow = q_ref.shape[0]
    sel_pages = rest[:nrow * n_pages]
    kvc_pages = rest[nrow * n_pages:2 * nrow * n_pages]
    o_ref, wout_ref, kvc_s = rest[2 * nrow * n_pages:2 * nrow * n_pages + 3]
    n = n_pages * (PAGE_SIZE // NSA_BLK)
    past = n_pages * PAGE_SIZE
    wrows = win_ref.shape[0] // nrow
    wb = wrows // ROWS_PER_TOK
    R = SUBLANE
    U = [(r, kv) for r in range(nrow) for kv in range(NSA_KV)]
    NU = range(len(U))
    tok_rows = lambda ref, base, which, kv, ntok: ref[pl.ds(base + which * NSA_KV + kv, ntok,
                                                            stride=ROWS_PER_TOK), :]
    page = lambda r, i: sel_pages[r * n_pages + i]

    for r in range(nrow):
        wout_ref[r * wrows:(r + 1) * wrows - ROWS_PER_TOK, :] = win_ref[r * wrows + ROWS_PER_TOK:(r + 1) * wrows, :]
        wout_ref[(r + 1) * wrows - ROWS_PER_TOK:(r + 1) * wrows, :] = winn_ref[r * ROWS_PER_TOK:
                                                                                (r + 1) * ROWS_PER_TOK, :]

    q8 = [q_ref[r] for r in range(nrow)]
    rowkv = lax.broadcasted_iota(jnp.int32, (R, 1), 0) // NSA_G
    bself = bself_ref[:, 0:1]

    for r in range(nrow):
        for i in range(n_pages):
            kvc_s[r, 2 * i:2 * i + 2, :] = kvc_pages[r * n_pages + i][0]
        kvc_s[r, n:LANE, :] = jnp.zeros((LANE - n, KV_ROW), F32)
    lane = lax.broadcasted_iota(jnp.int32, (R, LANE), 1)
    ii = lax.broadcasted_iota(jnp.int32, (LANE, LANE), 0)
    jj = lax.broadcasted_iota(jnp.int32, (LANE, LANE), 1)
    jrow = lax.broadcasted_iota(jnp.int32, (1, LANE), 1)
    eb = lax.broadcasted_iota(jnp.int32, (LANE, past), 0)
    ec = lax.broadcasted_iota(jnp.int32, (LANE, past), 1) // NSA_BLK
    expand = jnp.where(eb == ec, 1.0, 0.0).astype(BF16)
    wr = lax.broadcasted_iota(jnp.int32, (R, wb), 1)
    win_valid = (wb - wr) < NSA_WINDOW

    kc = [kvc_s[r, :, kv * NSA_DH:(kv + 1) * NSA_DH] for r, kv in U]
    vc = [kvc_s[r, :, (NSA_KV + kv) * NSA_DH:(NSA_KV + kv + 1) * NSA_DH] for r, kv in U]
    lg = [_mm3(q8[r], kc[u], NT) * NSA_SCALE + bcmp_ref[...] for u, (r, kv) in enumerate(U)]
    lg = [jnp.where(lane < n, x, NEG_INF) for x in lg]
    e = [jnp.exp(x - jnp.max(x, axis=-1, keepdims=True)) for x in lg]
    p = [jnp.where(lane < n, x / jnp.sum(x, axis=-1, keepdims=True), 0.0) for x in e]
    o_cmp = [_mm(p[u], vc[u]) for u in NU]
    imp = [jnp.sum(jnp.where(rowkv == kv, p[u], 0.0), axis=0, keepdims=True) for u, (r, kv) in enumerate(U)]
    forced = jnp.where(jrow == 0, 1.0, 0.0) + jnp.where(jrow == n, 1.0, 0.0) + jnp.where(jrow == n - 1, 1.0, 0.0)
    score = [jnp.where(jrow <= n, jnp.where(forced > 0.5, FORCE_SCORE, x), -2.0) for x in imp]
    s_b = [jnp.broadcast_to(x, (LANE, LANE)) for x in score]
    s_t = [jnp.transpose(x) for x in s_b]
    ahead = [jnp.where(s_t[u] > s_b[u], 1.0, jnp.where(s_t[u] == s_b[u], jnp.where(ii < jj, 1.0, 0.0), 0.0))
             for u in NU]
    rank = [jnp.sum(x, axis=0, keepdims=True) for x in ahead]
    sel8 = [jnp.broadcast_to(jnp.where(x < n_top, jnp.where(jrow <= n, 1.0, 0.0), 0.0), (R, LANE)) for x in rank]

    lg = [jnp.concatenate([_mm(q8[r], tok_rows(page(r, i), 0, 0, kv, PAGE_SIZE), NT) for i in range(n_pages)],
                          axis=1) * NSA_SCALE + bsel_ref[...] for r, kv in U]
    picked = [jnp.dot(x.astype(BF16), expand, preferred_element_type=F32) for x in sel8]
    lg = [jnp.where(picked[u] > 0.5, lg[u], NEG_INF) for u in NU]
    k_new = [new_row(seln_ref, r, 0, kv) for r, kv in U]
    v_new = [new_row(seln_ref, r, 1, kv) for r, kv in U]
    l_new = [jnp.sum(q8[r] * k_new[u], axis=-1, keepdims=True) * NSA_SCALE + bself for u, (r, kv) in enumerate(U)]
    l_new = [jnp.where(sel8[u][:, n:n + 1] > 0.5, l_new[u], NEG_INF) for u in NU]
    m = [jnp.maximum(jnp.max(lg[u], axis=-1, keepdims=True), l_new[u]) for u in NU]
    e = [jnp.exp(lg[u] - m[u]) for u in NU]
    e_new = [jnp.exp(l_new[u] - m[u]) for u in NU]
    den = [jnp.sum(e[u], axis=-1, keepdims=True) + e_new[u] for u in NU]
    pv = [e_new[u] * v_new[u] for u in NU]
    for i in range(n_pages):
        pv = [pv[u] + _mm(e[u][:, i * PAGE_SIZE:(i + 1) * PAGE_SIZE], tok_rows(page(r, i), 0, 1, kv, PAGE_SIZE))
              for u, (r, kv) in enumerate(U)]
    o_sel = [pv[u] / den[u] for u in NU]

    lg = [_mm(q8[r], tok_rows(win_ref, r * wrows, 0, kv, wb), NT) * NSA_SCALE + bwin_ref[...] for r, kv in U]
    lg = [jnp.where(win_valid, x, NEG_INF) for x in lg]
    k_new = [new_row(winn_ref, r, 0, kv) for r, kv in U]
    v_new = [new_row(winn_ref, r, 1, kv) for r, kv in U]
    l_new = [jnp.sum(q8[r] * k_new[u], axis=-1, keepdims=True) * NSA_SCALE + bself for u, (r, kv) in enumerate(U)]
    m = [jnp.maximum(jnp.max(lg[u], axis=-1, keepdims=True), l_new[u]) for u in NU]
    e = [jnp.exp(lg[u] - m[u]) for u in NU]
    e_new = [jnp.exp(l_new[u] - m[u]) for u in NU]
    den = [jnp.sum(e[u], axis=-1, keepdims=True) + e_new[u] for u in NU]
    o_win = [(e_new[u] * v_new[u] + _mm(e[u], tok_rows(win_ref, r * wrows, 1, kv, wb))) / den[u]
             for u, (r, kv) in enumerate(U)]

    for r in range(nrow):
        gate = jax.nn.sigmoid(gt_ref[r])
        pick = lambda outs: jnp.where(rowkv == 0, outs[r * NSA_KV], outs[r * NSA_KV + 1])
        o_ref[r] = gate[:, 0:1] * pick(o_cmp) + gate[:, 1:2] * pick(o_sel) + gate[:, 2:3] * pick(o_win)


NSA_SAMPLE_ROWS = 2


def _nsa_sample(page_table, q8, sel_new, win_new, gates, win_rows, sel_rows, kvc_pool, layer, n_pool, wb,
                bsel, bwin, bcmp, bself):
    db, n_pages = page_table.shape
    nrow = NSA_SAMPLE_ROWS
    assert db % nrow == 0
    past = n_pages * PAGE_SIZE
    n_blocks = past // NSA_BLK + 1
    kern = functools.partial(_nsa_sample_kernel, n_pages=n_pages, n_top=min(NSA_TOPN, n_blocks))
    fixed = lambda shape: pl.BlockSpec(shape, lambda i, pt: (0,) * len(shape))
    wrows = wb * ROWS_PER_TOK
    prows = PAGE_SIZE * ROWS_PER_TOK
    in_specs = [pl.BlockSpec((nrow, NSA_HEADS, NSA_DH), lambda i, pt: (i, 0, 0)),
                pl.BlockSpec((nrow * ROWS_PER_TOK, NSA_DH), lambda i, pt: (i, 0)),
                pl.BlockSpec((nrow * ROWS_PER_TOK, NSA_DH), lambda i, pt: (i, 0)),
                pl.BlockSpec((nrow, NSA_HEADS, 3), lambda i, pt: (i, 0, 0)),
                pl.BlockSpec((nrow * wrows, NSA_DH), lambda i, pt: (layer * (db // nrow) + i, 0)),
                fixed((NSA_HEADS, past)), fixed((NSA_HEADS, wb)), fixed((NSA_HEADS, LANE)),
                fixed((NSA_HEADS, LANE))]
    for r in range(nrow):
        for p in range(n_pages):
            in_specs.append(pl.BlockSpec(
                (prows, NSA_DH), lambda i, pt, r=r, p=p: (layer * n_pool + pt[i * nrow + r, p], 0)))
    for r in range(nrow):
        for p in range(n_pages):
            in_specs.append(pl.BlockSpec(
                (1, PAGE_SIZE // NSA_BLK, KV_ROW), lambda i, pt, r=r, p=p: (pt[i * nrow + r, p], 0, 0)))
    return pl.pallas_call(
        kern,
        grid_spec=pltpu.PrefetchScalarGridSpec(
            num_scalar_prefetch=1, grid=(db // nrow,), in_specs=in_specs,
            out_specs=[pl.BlockSpec((nrow, NSA_HEADS, NSA_DH), lambda i, pt: (i, 0, 0)),
                       pl.BlockSpec((nrow * wrows, NSA_DH), lambda i, pt: (i, 0))],
            scratch_shapes=[pltpu.VMEM((nrow, LANE, KV_ROW), F32)]),
        out_shape=[jax.ShapeDtypeStruct((db, NSA_HEADS, NSA_DH), F32),
                   jax.ShapeDtypeStruct((db * wrows, NSA_DH), F32)],
        compiler_params=_cparams(("arbitrary",)),
        name="nsa_sample",
    )(page_table, q8, sel_new, win_new, gates, win_rows, bsel, bwin, bcmp, bself,
      *([sel_rows] * (nrow * n_pages)), *([kvc_pool] * (nrow * n_pages)))


def _mix_kernel(oa_ref, ob_ref, ga_ref, gb_ref, x_ref, gt_ref, g_ref, woa_ref, wob_ref, wout_ref, o_ref):
    a = jnp.dot(oa_ref[0], woa_ref[...], preferred_element_type=F32)
    b = jnp.dot(ob_ref[0], wob_ref[...], preferred_element_type=F32)
    m = jax.nn.sigmoid(ga_ref[0]) * a + jax.nn.sigmoid(gb_ref[0]) * b
    mixed = _mm(m, wout_ref[...])
    o_ref[0] = x_ref[0] + gt_ref[0] * (_rms_rows(mixed) * g_ref[...])


def _mix(o_a, o_b, proj, x3, gt, g, w_oa, w_ob, w_out):
    bb, t, d = x3.shape
    tm = _pick(t, 512)
    row = lambda c: pl.BlockSpec((1, tm, d), lambda b, i: (b, i, c))
    return pl.pallas_call(
        _mix_kernel,
        grid=(bb, t // tm),
        in_specs=[row(0), row(0), row(C_GM // d), row(C_GM // d + 1), row(0), _mod_spec(gt, tm),
                  _resident((1, d)), _resident((d, d)), _resident((d, d)), _resident((d, d))],
        out_specs=row(0),
        out_shape=jax.ShapeDtypeStruct((bb, t, d), F32),
        compiler_params=_cparams(("parallel", "parallel")),
        name="mix",
    )(o_a, o_b, proj, proj, x3, gt, g.reshape(1, d), w_oa, w_ob, w_out)


FFN_SPLIT = 2


def _ffn_kernel(x_ref, g2_ref, sc_ref, sh_ref, gt_ref, g3_ref, wg_ref, wu_ref, wd_ref, o_ref):
    x = x_ref[0]
    h = _modulated_norm(x, g2_ref[...], sc_ref[0], sh_ref[0])
    tf = D_FF // FFN_SPLIT
    y = None
    for f in range(FFN_SPLIT):
        cols = slice(f * tf, (f + 1) * tf)
        gate = jnp.dot(h, wg_ref[:, cols], preferred_element_type=F32)
        up = jnp.dot(h, wu_ref[:, cols], preferred_element_type=F32)
        part = _mm(_silu(gate) * up, wd_ref[cols, :])
        y = part if y is None else y + part
    o_ref[0] = x + gt_ref[0] * (_rms_rows(y) * g3_ref[...])


def _ffn(x3, g2, sc, sh, gt, g3, w_gu, w_down):
    bb, t, d = x3.shape
    tm = _pick(t, 512)
    assert (D_FF // FFN_SPLIT) % LANE == 0
    row = pl.BlockSpec((1, tm, d), lambda b, i: (b, i, 0))
    return pl.pallas_call(
        _ffn_kernel,
        grid=(bb, t // tm),
        in_specs=[row, _resident((1, d)), _mod_spec(sc, tm), _mod_spec(sh, tm), _mod_spec(gt, tm),
                  _resident((1, d)), _resident((d, D_FF), (0, 0)), _resident((d, D_FF), (0, 1)),
                  _resident((D_FF, d))],
        out_specs=row,
        out_shape=jax.ShapeDtypeStruct((bb, t, d), F32),
        compiler_params=_cparams(("parallel", "parallel")),
        name="ffn",
    )(x3, g2.reshape(1, d), sc, sh, gt, g3.reshape(1, d), w_gu, w_gu, w_down)


def _bucket_np(n):
    n = np.maximum(n, 0)
    exact = NUM_BUCKETS // 2
    big = exact + (np.log(np.maximum(n, 1).astype(np.float32) / np.float32(exact))
                   / np.float32(math.log(MAX_DISTANCE / exact))
                   * np.float32(NUM_BUCKETS - exact)).astype(np.int32)
    return np.where(n < exact, n, np.minimum(big, NUM_BUCKETS - 1)).astype(np.int32)


def _bias_tables(rel_bias, t, past, wb):
    def table(dist):
        idx = jnp.asarray(_bucket_np(dist))[None]
        heads = rel_bias.shape[1]
        out = jnp.zeros((heads,) + idx.shape[1:], rel_bias.dtype)
        for bucket in range(NUM_BUCKETS):
            val = rel_bias[bucket].reshape((heads,) + (1,) * (idx.ndim - 1))
            out = jnp.where(idx == bucket, val, out)
        return out

    n = t // NSA_BLK
    ends = np.arange(n) * NSA_BLK + (NSA_BLK - 1)
    cmp_p = table(np.arange(t)[None, :] - ends[:, None])
    ar = np.arange(TQ)
    tiles = table(np.stack([d * TQ + ar[:, None] - ar[None, :] for d in range(3)]))
    sel_s = table(past - np.arange(past))
    win_s = table(wb - np.arange(wb))
    n_s = past // NSA_BLK
    cmp_s = table(np.pad(past - (np.arange(n_s) * NSA_BLK + NSA_BLK - 1), (0, LANE - n_s)))
    self_s = table(np.zeros((LANE,), np.int64))
    return cmp_p, tiles, sel_s, win_s, cmp_s, self_s


def _regroup_w_in(w):
    o = np.cumsum((0, CONV_DIM, DN_HEADS * DN_DV, DN_HEADS, DN_HEADS, NSA_HEADS * NSA_DH,
                   6 * NSA_KV * NSA_DH, 3 * NSA_HEADS, 2 * D_MODEL))
    qkv_z, b, a, qn, kvn, gn, gm = (w[:, o[0]:o[2]], w[:, o[2]:o[3]], w[:, o[3]:o[4]], w[:, o[4]:o[5]],
                                    w[:, o[5]:o[6]], w[:, o[6]:o[7]], w[:, o[7]:o[8]])
    pad = jnp.zeros((w.shape[0], N_PROJ - C_SM - 5 * NSA_HEADS), w.dtype)
    main = jnp.concatenate([qkv_z, qn, gm], axis=1).astype(BF16)
    kv_small = jnp.concatenate([kvn, b, a, gn, pad], axis=1).astype(BF16)
    return main, kv_small


def _layer_params(l, w_in, conv_w, a_log, dt_bias, onorm_g, w_oa, w_ob, w_out, cmp_pe, cmp_w1, cmp_w2,
                  w_gu, w_down):
    hp = jnp.zeros((SUBLANE, LANE), F32)
    hp = hp.at[0, SM_A:SM_A + DN_HEADS].set(a_log[l]).at[1, SM_A:SM_A + DN_HEADS].set(dt_bias[l])
    w_main, w_kv = _regroup_w_in(w_in[l])
    return dict(
        w_in=w_main, w_kv=w_kv, cw=conv_w[l], hp=hp, on=onorm_g[l].reshape(1, DN_DV),
        w_oa=w_oa[l].astype(BF16), w_ob=w_ob[l].astype(BF16), w_out=w_out[l].astype(BF16),
        pe=jnp.reshape(cmp_pe[l], (1, NSA_BLK * 2 * NSA_DH)),
        w1=jnp.transpose(cmp_w1[l], (1, 0, 2, 3)).reshape(NSA_BLK * 2 * NSA_DH, NSA_CMP_HID).astype(BF16),
        w2=cmp_w2[l].reshape(2 * NSA_CMP_HID, NSA_DH).astype(BF16),
        w_gu=w_gu[l].astype(BF16), w_down=w_down[l].astype(BF16))


def _mods(mod, per_token):
    parts = [mod[:, i * D_MODEL:(i + 1) * D_MODEL] for i in range(6)]
    return [p[None, :, :] if per_token else p[:, None, :] for p in parts]


def _tail(x3, o_a, o_b, proj, mods, ng, P):
    sh1, sc1, gt1, sh2, sc2, gt2 = mods
    x1 = _mix(o_a, o_b, proj, x3, gt1, ng[1], P['w_oa'], P['w_ob'], P['w_out'])
    return _ffn(x1, ng[2], sc2, sh2, gt2, ng[3], P['w_gu'], P['w_down'])


def _prompt_layer(x3, mods, ng, P, tabs):
    b, t, _ = x3.shape
    proj = _proj(x3, ng[0], mods[1], mods[0], P['w_in'])
    cmp_rows, sel_rows, win_rows, small = _kvproj(x3, ng[0], mods[1], mods[0], P['w_kv'])
    o_a, s_new = _dn_prompt(proj, small, P['cw'], P['hp'], P['on'])
    n = t // NSA_BLK
    kvc = _compress(cmp_rows, 0, b * n, P['pe'], P['w1'], P['w2']).reshape(b, n, KV_ROW)
    o_cmp, selt = _cmp_select(proj, kvc, tabs[0])
    o_b = _selwin(proj, sel_rows, win_rows, small, selt, tabs[1] * LOG2E, o_cmp)
    y = _tail(x3, o_a, o_b, proj, mods, ng, P)
    conv_new = proj[:, t - (CONV_W - 1):, C_QKV:C_QKV + CONV_DIM]
    as_cache = lambda rows: rows.reshape(b, t, 2, NSA_KV, NSA_DH)
    return y, (as_cache(cmp_rows), as_cache(sel_rows), as_cache(win_rows)[:, t - min(NSA_WINDOW, t):],
               s_new, conv_new)


def _sample_layer(x3, mods, ng, P, tabs, layer, page_table, cmp_blocks, sel_rows, win_rows, n_pool, wb,
                  state_all, conv_all):
    db = x3.shape[1]
    proj = _proj(x3, ng[0], mods[1], mods[0], P['w_in'])
    cmp_new, sel_new, win_new, small = _kvproj(x3, ng[0], mods[1], mods[0], P['w_kv'])
    o_a, s_new = _dn_sample(proj.reshape(db, 1, N_MAIN), small.reshape(db, 1, LANE), conv_all, state_all,
                            layer, P['cw'], P['hp'], P['on'])
    halves = PAGE_SIZE // NSA_BLK
    kvc_pool = _compress(cmp_blocks, layer * n_pool * halves, n_pool * halves, P['pe'], P['w1'], P['w2'])
    q8 = proj[0, :, C_QN:C_QN + NSA_HEADS * NSA_DH].reshape(db, NSA_HEADS, NSA_DH)
    gates = jnp.transpose(small[0, :, SM_GATE:SM_GATE + 3 * NSA_HEADS].reshape(db, 3, NSA_HEADS), (0, 2, 1))
    o_b, win_cache = _nsa_sample(page_table, q8, sel_new, win_new, gates, win_rows, sel_rows,
                                 kvc_pool.reshape(n_pool, halves, KV_ROW), layer, n_pool, wb,
                                 tabs[2], tabs[3], tabs[4], tabs[5])
    y = _tail(x3, o_a.reshape(1, db, D_MODEL).astype(BF16), o_b.reshape(1, db, D_MODEL).astype(BF16),
              proj, mods, ng, P)
    qkv_new = proj[0, :, C_QKV:C_QKV + CONV_DIM]
    conv_new = jnp.concatenate([conv_all[layer][:, 1:], qkv_new[:, None, :]], axis=1)
    as_cache = lambda rows, ntok: rows.reshape(db, ntok, 2, NSA_KV, NSA_DH)
    return y, (as_cache(cmp_new, 1), as_cache(sel_new, 1), as_cache(win_cache, wb), s_new, conv_new)


def kernel(x_prompt, x_sample, cache_cmp_kv, cache_sel_kv, cache_win_kv, state_delta, state_conv, page_table, c_prompt, c_sample, w_ada, b_ada, norm_g, w_in, conv_w, a_log, dt_bias, onorm_g, w_oa, w_ob, w_out, cmp_pe, cmp_w1, cmp_w2, rel_bias, w_gu, w_down):
    depth = w_in.shape[0]
    b, t, d = x_prompt.shape
    db, ts, _ = x_sample.shape
    assert ts == 1 and d == D_MODEL and t % TQ == 0
    n_pool, n_pages = cache_cmp_kv.shape[1], page_table.shape[1]
    assert cache_cmp_kv.shape[2] == PAGE_SIZE
    past = n_pages * PAGE_SIZE
    wb = cache_win_kv.shape[2]
    tabs = _bias_tables(rel_bias, t, past, wb)
    cmp_blocks = cache_cmp_kv.reshape(depth * n_pool * PAGE_SIZE * ROWS_PER_TOK, NSA_DH)
    sel_rows = cache_sel_kv.reshape(depth * n_pool * PAGE_SIZE * ROWS_PER_TOK, NSA_DH)
    win_rows = cache_win_kv.reshape(depth * db * wb * ROWS_PER_TOK, NSA_DH)
    c_all = jnp.concatenate([c_prompt, c_sample], axis=0)
    pad_rows = (-c_all.shape[0]) % SUBLANE
    c_all = jnp.pad(c_all, ((0, pad_rows), (0, 0)))
    yp = x_prompt
    ys = x_sample.reshape(1, db, d)
    p_st, s_st = [], []
    for l in range(depth):
        P = _layer_params(l, w_in, conv_w, a_log, dt_bias, onorm_g, w_oa, w_ob, w_out, cmp_pe, cmp_w1,
                          cmp_w2, w_gu, w_down)
        mod = _ada(c_all, w_ada[l], b_ada[l])
        yp, st = _prompt_layer(yp, _mods(mod[:b], False), norm_g[l], P, tabs)
        p_st.append(st)
        ys, st = _sample_layer(ys, _mods(mod[b:b + db], True), norm_g[l], P, tabs, l, page_table,
                               cmp_blocks, sel_rows, win_rows, n_pool, wb, state_delta, state_conv)
        s_st.append(st)
    stk = lambda states, i: jnp.stack([s[i] for s in states])
    return (yp, ys.reshape(db, 1, d),
            stk(p_st, 0), stk(p_st, 1), stk(p_st, 2), stk(p_st, 3), stk(p_st, 4),
            stk(s_st, 0), stk(s_st, 1), stk(s_st, 2), stk(s_st, 3), stk(s_st, 4))
```
